```python
import functools
import jax
import jax.numpy as jnp
from jax import lax
import numpy as np

D_MODEL = 2048
BATCH = 8
SEQ = 2048
DEPTH = 1
DEC_BATCH = 128
DEC_SEQ = 8
PAST_LEN = 8192
PAGE_SIZE = 128

HEAD_DIM = 64
MIX_WIDTH = D_MODEL
ATTN_WIDTH = MIX_WIDTH // 2
RWKV_WIDTH = MIX_WIDTH - ATTN_WIDTH
ATTN_HEADS = ATTN_WIDTH // HEAD_DIM
KV_HEADS = ATTN_HEADS // 4
GQA_GROUP = ATTN_HEADS // KV_HEADS
KV_WIDTH = KV_HEADS * HEAD_DIM
WINDOW = 128
BAND_BLOCK = WINDOW
ROPE_THETA = 10000.0
RWKV_HEADS = RWKV_WIDTH // HEAD_DIM
DECAY_LORA = 64
ICLR_LORA = 64
GATE_LORA = 160
ATTN_PROJ = ATTN_WIDTH + 2 * KV_WIDTH
RWKV_PROJ = 3 * RWKV_WIDTH + DECAY_LORA + ICLR_LORA + GATE_LORA
IN_PROJ = ATTN_PROJ + RWKV_PROJ
N_EXPERTS = 32
TOP_K = 4
D_FF = D_MODEL
SWIGLU_ALPHA = 1.702
SWIGLU_LIMIT = 7.0
MOE_BLOCK = 128
NORM_EPS = 1e-5
LNX_EPS = 64e-5
NEG_BIG = -1e30

kernel_name = "hymba_swa_sink_rwkv7_moe_step"


def rms_norm(x, g):
    xf = x.astype(jnp.float32)
    xf = xf * lax.rsqrt(jnp.mean(xf * xf, axis=-1, keepdims=True) + NORM_EPS)
    return xf.astype(x.dtype) * g


def rope(x, pos):
    half = HEAD_DIM // 2
    inv_freq = ROPE_THETA ** (-jnp.arange(half, dtype=jnp.float32) / half)
    ang = pos.astype(jnp.float32)[:, None] * inv_freq[None, :]
    cos = jnp.cos(ang)[:, None, :]
    sin = jnp.sin(ang)[:, None, :]
    xf = x.astype(jnp.float32)
    x1, x2 = xf[..., :half], xf[..., half:]
    return jnp.concatenate([x1 * cos - x2 * sin, x2 * cos + x1 * sin], axis=-1).astype(x.dtype)


def attend_with_sinks(q, k, v, mask, sinks):
    q = q.reshape(*q.shape[:-2], KV_HEADS, GQA_GROUP, HEAD_DIM)
    s = jnp.einsum("...qkgd,...skd->...kgqs", q, k).astype(jnp.float32) * (HEAD_DIM ** -0.5)
    s = jnp.where(mask, s, NEG_BIG)
    sink = jnp.broadcast_to(sinks.astype(jnp.float32).reshape(KV_HEADS, GQA_GROUP, 1, 1), s.shape[:-1] + (1,))
    p = jax.nn.softmax(jnp.concatenate([s, sink], axis=-1), axis=-1)[..., :-1]
    o = jnp.einsum("...kgqs,...skd->...qkgd", p.astype(v.dtype), v)
    return o.reshape(*o.shape[:-3], ATTN_WIDTH)


def prompt_attention(q, k, v, sinks):
    b, s_len = q.shape[:2]
    nb = s_len // BAND_BLOCK
    qb = q.reshape(b, nb, BAND_BLOCK, ATTN_HEADS, HEAD_DIM)

    def band(t):
        tb = t.reshape(b, nb, BAND_BLOCK, KV_HEADS, HEAD_DIM)
        prev = jnp.concatenate([jnp.zeros_like(tb[:, :1]), tb[:, :-1]], axis=1)
        return jnp.concatenate([prev, tb], axis=2)

    qpos = jnp.arange(s_len, dtype=jnp.int32).reshape(nb, BAND_BLOCK)
    kpos = qpos[:, :1] - BAND_BLOCK + jnp.arange(2 * BAND_BLOCK, dtype=jnp.int32)[None, :]
    diff = qpos[:, :, None] - kpos[:, None, :]
    mask = (diff >= 0) & (diff < WINDOW) & (kpos[:, None, :] >= 0)
    o = attend_with_sinks(qb, band(k), band(v), mask[:, None, None], sinks)
    return o.reshape(b, s_len, ATTN_WIDTH), k[:, -WINDOW:], v[:, -WINDOW:]


def sample_attention(q, k, v, sinks, cache_k, cache_v):
    t_len = q.shape[1]
    wb = cache_k.shape[1]
    k_all = jnp.concatenate([cache_k.astype(k.dtype), k], axis=1)
    v_all = jnp.concatenate([cache_v.astype(v.dtype), v], axis=1)
    qpos = PAST_LEN + jnp.arange(t_len, dtype=jnp.int32)
    kpos = jnp.concatenate([PAST_LEN - wb + jnp.arange(wb, dtype=jnp.int32), qpos])
    diff = qpos[:, None] - kpos[None, :]
    mask = (diff >= 0) & (diff < WINDOW)
    o = attend_with_sinks(q, k_all, v_all, mask, sinks)
    return o, k_all[:, -wb:], v_all[:, -wb:]


def rwkv7_time_mix(p, shift_last, wkv0, mu_shift, decay_w0, decay_lora_up, iclr_a0, iclr_lora_up,
                   gate_lora_up, k_k, k_a, r_k, lnx_w, lnx_b):
    b, t_len, _ = p.shape
    p_prev = jnp.concatenate([shift_last[:, None].astype(p.dtype), p[:, :-1]], axis=1)
    xs = p + (p_prev - p) * mu_shift
    r, k, v, xw, xa, xg = jnp.split(
        xs, [RWKV_WIDTH, 2 * RWKV_WIDTH, 3 * RWKV_WIDTH, 3 * RWKV_WIDTH + DECAY_LORA,
             3 * RWKV_WIDTH + DECAY_LORA + ICLR_LORA], axis=-1)
    w_log = -jax.nn.softplus(-(decay_w0 + jnp.tanh(xw) @ decay_lora_up)) - 0.5
    decay = jnp.exp(-jnp.exp(w_log.astype(jnp.float32)))
    a = jax.nn.sigmoid(iclr_a0 + xa @ iclr_lora_up)
    g = jax.nn.sigmoid(xg) @ gate_lora_up

    def heads(u):
        return u.reshape(b, t_len, RWKV_HEADS, HEAD_DIM).astype(jnp.float32)

    kk = heads(k * k_k)
    kk = kk / jnp.maximum(jnp.sqrt(jnp.sum(kk * kk, axis=-1, keepdims=True)), 1e-12)
    k = k * (1 + (a - 1) * k_a)
    rh, kh, vh, ah, wh = heads(r), heads(k), heads(v), heads(a), heads(decay)

    def step(state, inp):
        r_t, w_t, k_t, v_t, kk_t, a_t = inp
        sa = jnp.einsum("bhvk,bhk->bhv", state, -kk_t)
        state = (state * w_t[:, :, None, :] + sa[..., None] * (kk_t * a_t)[:, :, None, :]
                 + v_t[..., None] * k_t[:, :, None, :])
        return state, jnp.einsum("bhvk,bhk->bhv", state, r_t)

    tm = lambda u: jnp.moveaxis(u, 1, 0)
    s_final, y = lax.scan(step, wkv0.astype(jnp.float32), (tm(rh), tm(wh), tm(kh), tm(vh), tm(kk), tm(ah)))
    y = jnp.moveaxis(y, 0, 1)
    mean = jnp.mean(y, axis=-1, keepdims=True)
    var = jnp.mean(jnp.square(y - mean), axis=-1, keepdims=True)
    y = ((y - mean) * lax.rsqrt(var + LNX_EPS)).reshape(b, t_len, RWKV_WIDTH) * lnx_w + lnx_b
    bonus = (jnp.sum(rh * kh * r_k, axis=-1, keepdims=True) * vh).reshape(b, t_len, RWKV_WIDTH)
    out = (y + bonus) * g
    return out.astype(p.dtype), s_final.astype(wkv0.dtype), p[:, -1]


def clamped_swiglu(h):
    x_glu, x_lin = h[..., ::2], h[..., 1::2]
    x_glu = jnp.minimum(x_glu, SWIGLU_LIMIT)
    x_lin = jnp.clip(x_lin, -SWIGLU_LIMIT, SWIGLU_LIMIT)
    return x_glu * jax.nn.sigmoid(SWIGLU_ALPHA * x_glu) * (x_lin + 1)


def moe_ffn(x, router_w, router_b, w_up, b_up, w_down, b_down):
    lead = x.shape[:-1]
    xt = x.reshape(-1, D_MODEL)
    n_tok = xt.shape[0]
    logits = (xt @ router_w + router_b).astype(jnp.float32)
    top_val, top_idx = lax.top_k(logits, TOP_K)
    gates = jax.nn.softmax(top_val, axis=-1)
    n_assign = n_tok * TOP_K
    e_flat = top_idx.reshape(-1)
    tok_flat = jnp.repeat(jnp.arange(n_tok, dtype=jnp.int32), TOP_K)
    g_flat = gates.reshape(-1)
    order = jnp.argsort(e_flat)
    e_sorted = e_flat[order]
    counts = jnp.bincount(e_flat, length=N_EXPERTS)
    padded = ((counts + MOE_BLOCK - 1) // MOE_BLOCK) * MOE_BLOCK
    start = jnp.cumsum(counts) - counts
    p_end = jnp.cumsum(padded)
    p_start = p_end - padded
    dest = p_start[e_sorted] + (jnp.arange(n_assign, dtype=jnp.int32) - start[e_sorted])
    n_blocks = -(-n_assign // MOE_BLOCK) + N_EXPERTS
    n_rows = n_blocks * MOE_BLOCK
    row_tok = jnp.zeros((n_rows,), jnp.int32).at[dest].set(tok_flat[order])
    row_gate = jnp.zeros((n_rows,), jnp.float32).at[dest].set(g_flat[order])
    blk_start = jnp.arange(n_blocks, dtype=jnp.int32) * MOE_BLOCK
    blk_expert = jnp.minimum(jnp.searchsorted(p_end, blk_start, side="right"), N_EXPERTS - 1)

    def expert_block(args):
        tok, gate, e = args
        h = clamped_swiglu(xt[tok] @ w_up[e] + b_up[e])
        o = h @ w_down[e] + b_down[e]
        return o * gate[:, None].astype(o.dtype)

    out_rows = lax.map(expert_block, (row_tok.reshape(n_blocks, MOE_BLOCK),
                                      row_gate.reshape(n_blocks, MOE_BLOCK), blk_expert))
    y = jax.ops.segment_sum(out_rows.reshape(n_rows, D_MODEL), row_tok, num_segments=n_tok)
    return y.reshape(*lead, D_MODEL).astype(x.dtype)


def hybrid_layer(x, pos, attend, shift_last, wkv0, attn_norm_g, w_in, attn_sinks, mu_shift, decay_w0,
                 decay_lora_up, iclr_a0, iclr_lora_up, gate_lora_up, k_k, k_a, r_k, lnx_w, lnx_b, w_out,
                 ffn_norm_g, router_w, router_b, w_up, b_up, w_down, b_down):
    b, t_len, _ = x.shape
    proj = rms_norm(x, attn_norm_g) @ w_in
    q, k, v, p_rwkv = jnp.split(proj, [ATTN_WIDTH, ATTN_WIDTH + KV_WIDTH, ATTN_PROJ], axis=-1)
    q = rope(q.reshape(b, t_len, ATTN_HEADS, HEAD_DIM), pos)
    k = rope(k.reshape(b, t_len, KV_HEADS, HEAD_DIM), pos)
    v = v.reshape(b, t_len, KV_HEADS, HEAD_DIM)
    o_attn, k_rows, v_rows = attend(q, k, v, attn_sinks)
    o_rwkv, wkv_new, shift_new = rwkv7_time_mix(p_rwkv, shift_last, wkv0, mu_shift, decay_w0, decay_lora_up,
                                                 iclr_a0, iclr_lora_up, gate_lora_up, k_k, k_a, r_k,
                                                 lnx_w, lnx_b)
    h = x + jnp.concatenate([o_attn, o_rwkv.astype(o_attn.dtype)], axis=-1) @ w_out
    h = h + moe_ffn(rms_norm(h, ffn_norm_g), router_w, router_b, w_up, b_up, w_down, b_down)
    return h, k_rows, v_rows, wkv_new, shift_new


def setup_inputs(seed: int = 0) -> dict:
    key = jax.random.key(seed)
    ks = jax.random.split(key, 32)
    f32 = jnp.float32
    L = DEPTH
    wb = min(WINDOW, PAST_LEN)

    def nrm(k, shape, scale):
        return jax.random.normal(k, shape, f32) * scale

    return {
        "x_prompt": nrm(ks[0], (BATCH, SEQ, D_MODEL), 1.0),
        "x_sample": nrm(ks[1], (DEC_BATCH, DEC_SEQ, D_MODEL), 1.0),
        "cache_k": nrm(ks[2], (L, DEC_BATCH, wb, KV_HEADS, HEAD_DIM), 1.0),
        "cache_v": nrm(ks[3], (L, DEC_BATCH, wb, KV_HEADS, HEAD_DIM), 1.0),
        "state_wkv": nrm(ks[4], (L, DEC_BATCH, RWKV_HEADS, HEAD_DIM, HEAD_DIM), 0.3),
        "state_shift": nrm(ks[5], (L, DEC_BATCH, RWKV_PROJ), 1.0),
        "attn_norm_g": 1.0 + nrm(ks[6], (L, D_MODEL), 0.02),
        "w_in": nrm(ks[7], (L, D_MODEL, IN_PROJ), D_MODEL ** -0.5),
        "attn_sinks": nrm(ks[8], (L, ATTN_HEADS), 1.0),
        "mu_shift": jax.random.uniform(ks[9], (L, RWKV_PROJ), f32),
        "decay_w0": jax.random.uniform(ks[10], (L, RWKV_WIDTH), f32, -3.0, 1.0),
        "decay_lora_up": nrm(ks[11], (L, DECAY_LORA, RWKV_WIDTH), 0.1 * DECAY_LORA ** -0.5),
        "iclr_a0": nrm(ks[12], (L, RWKV_WIDTH), 0.5),
        "iclr_lora_up": nrm(ks[13], (L, ICLR_LORA, RWKV_WIDTH), ICLR_LORA ** -0.5),
        "gate_lora_up": nrm(ks[14], (L, GATE_LORA, RWKV_WIDTH), GATE_LORA ** -0.5),
        "k_k": 0.85 + nrm(ks[15], (L, RWKV_WIDTH), 0.02),
        "k_a": 1.0 + nrm(ks[16], (L, RWKV_WIDTH), 0.02),
        "r_k": nrm(ks[17], (L, RWKV_HEADS, HEAD_DIM), 0.1),
        "lnx_w": 1.0 + nrm(ks[18], (L, RWKV_WIDTH), 0.02),
        "lnx_b": nrm(ks[19], (L, RWKV_WIDTH), 0.02),
        "w_out": nrm(ks[20], (L, MIX_WIDTH, D_MODEL), MIX_WIDTH ** -0.5),
        "ffn_norm_g": 1.0 + nrm(ks[21], (L, D_MODEL), 0.02),
        "router_w": nrm(ks[22], (L, D_MODEL, N_EXPERTS), D_MODEL ** -0.5),
        "router_b": nrm(ks[23], (L, N_EXPERTS), 0.01),
        "w_up": nrm(ks[24], (L, N_EXPERTS, D_MODEL, 2 * D_FF), D_MODEL ** -0.5),
        "b_up": nrm(ks[25], (L, N_EXPERTS, 2 * D_FF), 0.01),
        "w_down": nrm(ks[26], (L, N_EXPERTS, D_FF, D_MODEL), D_FF ** -0.5),
        "b_down": nrm(ks[27], (L, N_EXPERTS, D_MODEL), 0.01),
        "final_norm_g": 1.0 + nrm(ks[28], (D_MODEL,), 0.02),
    }


def reference(x_prompt, x_sample, cache_k, cache_v, state_wkv, state_shift, attn_norm_g, w_in, attn_sinks,
              mu_shift, decay_w0, decay_lora_up, iclr_a0, iclr_lora_up, gate_lora_up, k_k, k_a, r_k, lnx_w,
              lnx_b, w_out, ffn_norm_g, router_w, router_b, w_up, b_up, w_down, b_down, final_norm_g):
    pos_p = jnp.arange(x_prompt.shape[1], dtype=jnp.int32)
    pos_s = PAST_LEN + jnp.arange(x_sample.shape[1], dtype=jnp.int32)
    hp, hs = x_prompt, x_sample
    kp, vp, wp, sp = [], [], [], []
    ksm, vsm, wsm, ssm = [], [], [], []
    for l in range(DEPTH):
        lw = (attn_norm_g[l], w_in[l], attn_sinks[l], mu_shift[l], decay_w0[l], decay_lora_up[l], iclr_a0[l],
              iclr_lora_up[l], gate_lora_up[l], k_k[l], k_a[l], r_k[l], lnx_w[l], lnx_b[l], w_out[l],
              ffn_norm_g[l], router_w[l], router_b[l], w_up[l], b_up[l], w_down[l], b_down[l])
        nb = hp.shape[0]
        hp, k_r, v_r, wkv_r, sh_r = hybrid_layer(
            hp, pos_p, prompt_attention, jnp.zeros((nb, RWKV_PROJ), hp.dtype),
            jnp.zeros((nb, RWKV_HEADS, HEAD_DIM, HEAD_DIM), jnp.float32), *lw)
        kp.append(k_r)
        vp.append(v_r)
        wp.append(wkv_r)
        sp.append(sh_r)
        attend_s = functools.partial(sample_attention, cache_k=cache_k[l], cache_v=cache_v[l])
        hs, k_r, v_r, wkv_r, sh_r = hybrid_layer(hs, pos_s, attend_s, state_shift[l], state_wkv[l], *lw)
        ksm.append(k_r)
        vsm.append(v_r)
        wsm.append(wkv_r)
        ssm.append(sh_r)
    y_prompt = rms_norm(hp, final_norm_g)
    y_sample = rms_norm(hs, final_norm_g)
    return (y_prompt, y_sample, jnp.stack(kp), jnp.stack(vp), jnp.stack(wp), jnp.stack(sp),
            jnp.stack(ksm), jnp.stack(vsm), jnp.stack(wsm), jnp.stack(ssm))
```

```python
import functools

import jax
import jax.numpy as jnp
from jax import lax
from jax.experimental import pallas as pl
from jax.experimental.pallas import tpu as pltpu

F32 = jnp.float32
BF16 = jnp.bfloat16

D_MODEL = 2048
HEAD_DIM = 64
LANES = 128
ATTN_WIDTH = 1024
KV_WIDTH = 256
ATTN_HEADS = 16
RWKV_WIDTH = 1024
DECAY_LORA = 64
ICLR_LORA = 64
GATE_LORA = 160
RWKV_PROJ = 3 * RWKV_WIDTH + DECAY_LORA + ICLR_LORA + GATE_LORA
XW_OFF = 3 * RWKV_WIDTH
XA_OFF = XW_OFF + LANES
XG_OFF = XA_OFF + LANES
XG_PAD = 2 * LANES
RWKV_PROJ_PAD = XG_OFF + XG_PAD
WINDOW = 128
ROPE_THETA = 10000.0
PAST_LEN = 8192
N_EXPERTS = 32
TOP_K = 4
D_FF = 2048
SWIGLU_ALPHA = 1.702
SWIGLU_LIMIT = 7.0
NORM_EPS = 1e-5
LNX_EPS = 64e-5
NEG_BIG = -1e30
VMEM_LIMIT = 56 * 1024 * 1024


def _cparams(sem):
    return pltpu.CompilerParams(dimension_semantics=sem, vmem_limit_bytes=VMEM_LIMIT)


_NN = (((1,), (0,)), ((), ()))
_NT = (((1,), (1,)), ((), ()))
_TN = (((0,), (0,)), ((), ()))


def _mm(a, b, dims=_NN):
    return lax.dot_general(a, b, dims, preferred_element_type=F32)


def _split2(x):
    hi = x.astype(BF16)
    lo = (x - hi.astype(F32)).astype(BF16)
    return hi, lo


def _split3(x):
    hi = x.astype(BF16)
    r1 = x - hi.astype(F32)
    mid = r1.astype(BF16)
    lo = (r1 - mid.astype(F32)).astype(BF16)
    return hi, mid, lo


def _mm_x3(a, b, dims=_NN):
    ah, al = _split2(a)
    bh, bl = _split2(b)
    return _mm(ah, bh, dims) + (_mm(ah, bl, dims) + _mm(al, bh, dims))


def _mm_exact_b(a, b_bf16, dims=_NN):
    h, m, l = _split3(a)
    return _mm(h, b_bf16, dims) + (_mm(m, b_bf16, dims) + _mm(l, b_bf16, dims))


def _mm_exact_a(a_bf16, b, dims=_NN):
    h, m, l = _split3(b)
    return _mm(a_bf16, h, dims) + (_mm(a_bf16, m, dims) + _mm(a_bf16, l, dims))


def _rms(x, g):
    ms = jnp.mean(x * x, axis=-1, keepdims=True)
    return (x * lax.rsqrt(ms + NORM_EPS)) * g


def _attn_proj_kernel(x_ref, g_ref, w_ref, cos_ref, sin_ref, q_ref, k_ref, v_ref):
    xn = _rms(x_ref[...], g_ref[...]).astype(BF16)
    a = _mm(xn, w_ref[...])
    cos = cos_ref[...]
    sin = sin_ref[...]
    lane = lax.broadcasted_iota(jnp.int32, cos.shape, 1)
    first_half = (lane % HEAD_DIM) < (HEAD_DIM // 2)
    n_rot = (ATTN_WIDTH + KV_WIDTH) // LANES
    for j in range(n_rot):
        t = a[:, j * LANES:(j + 1) * LANES]
        swapped = jnp.where(first_half, pltpu.roll(t, LANES - HEAD_DIM // 2, 1), pltpu.roll(t, HEAD_DIM // 2, 1))
        rot = t * cos + swapped * sin
        if j < ATTN_WIDTH // LANES:
            q_ref[:, j * LANES:(j + 1) * LANES] = (rot * (HEAD_DIM ** -0.5)).astype(BF16)
        else:
            jj = j - ATTN_WIDTH // LANES
            k_ref[:, jj * LANES:(jj + 1) * LANES] = rot
    v_ref[...] = a[:, ATTN_WIDTH + KV_WIDTH:]


def _attn_proj(x, g, w_attn, cos_t, sin_t, tm):
    t_tok = x.shape[0]
    n_pos_blocks = cos_t.shape[0] // tm
    return pl.pallas_call(
        _attn_proj_kernel,
        grid=(t_tok // tm,),
        in_specs=[
            pl.BlockSpec((tm, D_MODEL), lambda i: (i, 0)),
            pl.BlockSpec((1, D_MODEL), lambda i: (0, 0)),
            pl.BlockSpec((D_MODEL, ATTN_WIDTH + 2 * KV_WIDTH), lambda i: (0, 0)),
            pl.BlockSpec((tm, LANES), lambda i: (i % n_pos_blocks, 0)),
            pl.BlockSpec((tm, LANES), lambda i: (i % n_pos_blocks, 0)),
        ],
        out_specs=[
            pl.BlockSpec((tm, ATTN_WIDTH), lambda i: (i, 0)),
            pl.BlockSpec((tm, KV_WIDTH), lambda i: (i, 0)),
            pl.BlockSpec((tm, KV_WIDTH), lambda i: (i, 0)),
        ],
        out_shape=[
            jax.ShapeDtypeStruct((t_tok, ATTN_WIDTH), BF16),
            jax.ShapeDtypeStruct((t_tok, KV_WIDTH), F32),
            jax.ShapeDtypeStruct((t_tok, KV_WIDTH), F32),
        ],
        compiler_params=_cparams(("parallel",)),
        name="attn_proj",
    )(x, g, w_attn, cos_t, sin_t)


def _rwkv_proj_kernel(x_ref, g_ref, w_ref, p_ref, xn_ref):
    @pl.when(pl.program_id(1) == 0)
    def _():
        xn_ref[...] = _rms(x_ref[...], g_ref[...]).astype(BF16)

    p_ref[...] = _mm(xn_ref[...], w_ref[...])


def _rwkv_proj(x, g, w_rwkv, tm, tn):
    t_tok = x.shape[0]
    return pl.pallas_call(
        _rwkv_proj_kernel,
        grid=(t_tok // tm, RWKV_PROJ_PAD // tn),
        in_specs=[
            pl.BlockSpec((tm, D_MODEL), lambda i, n: (i, 0)),
            pl.BlockSpec((1, D_MODEL), lambda i, n: (0, 0)),
            pl.BlockSpec((D_MODEL, tn), lambda i, n: (0, n)),
        ],
        out_specs=pl.BlockSpec((tm, tn), lambda i, n: (i, n)),
        out_shape=jax.ShapeDtypeStruct((t_tok, RWKV_PROJ_PAD), F32),
        scratch_shapes=[pltpu.VMEM((tm, D_MODEL), BF16)],
        compiler_params=_cparams(("parallel", "arbitrary")),
        name="rwkv_proj",
    )(x, g, w_rwkv)


def _softmax_pv(s, mask, sink, vt):
    s = jnp.where(mask, s, NEG_BIG)
    m = jnp.maximum(jnp.max(s, axis=-1, keepdims=True), sink)
    p = jnp.exp(s - m)
    denom = jnp.sum(p, axis=-1, keepdims=True) + jnp.exp(sink - m)
    return _mm(p.astype(BF16), vt) * (1.0 / denom)


def _attn_prompt_kernel(q_ref, kp_ref, kc_ref, vp_ref, vc_ref, sink_ref, o_ref):
    i = pl.program_id(1)
    blk = q_ref.shape[0]
    kk = jnp.concatenate([kp_ref[...], kc_ref[...]], axis=0).astype(BF16)
    vv = jnp.concatenate([vp_ref[...], vc_ref[...]], axis=0).astype(BF16)
    a = lax.broadcasted_iota(jnp.int32, (blk, 2 * blk), 0)
    c = lax.broadcasted_iota(jnp.int32, (blk, 2 * blk), 1)
    mask = (c > a) & (c <= a + blk) & ((c >= blk) | (i > 0))
    lane_lo = lax.broadcasted_iota(jnp.int32, (blk, LANES), 1) < HEAD_DIM
    for pair in range(KV_WIDTH // LANES):
        kt = kk[:, pair * LANES:(pair + 1) * LANES]
        vt = vv[:, pair * LANES:(pair + 1) * LANES]
        for j in range(4):
            tile = pair * 4 + j
            qt = q_ref[:, tile * LANES:(tile + 1) * LANES]
            halves = []
            for e in range(2):
                qm = jnp.where(lane_lo if e == 0 else jnp.logical_not(lane_lo), qt, jnp.zeros_like(qt))
                s = _mm(qm, kt, _NT)
                halves.append(_softmax_pv(s, mask, sink_ref[8 * pair + 4 * e + j], vt))
            o_ref[:, tile * LANES:(tile + 1) * LANES] = jnp.where(lane_lo, halves[0], halves[1]).astype(BF16)


def _attn_prompt(q, k, v, sinks, batch, seq):
    blk = WINDOW
    nb = seq // blk
    cur = lambda b, i: (b * nb + i, 0)
    prev = lambda b, i: (b * nb + jnp.maximum(i - 1, 0), 0)
    return pl.pallas_call(
        _attn_prompt_kernel,
        grid=(batch, nb),
        in_specs=[
            pl.BlockSpec((blk, ATTN_WIDTH), cur),
            pl.BlockSpec((blk, KV_WIDTH), prev),
            pl.BlockSpec((blk, KV_WIDTH), cur),
            pl.BlockSpec((blk, KV_WIDTH), prev),
            pl.BlockSpec((blk, KV_WIDTH), cur),
            pl.BlockSpec(memory_space=pltpu.SMEM),
        ],
        out_specs=pl.BlockSpec((blk, ATTN_WIDTH), cur),
        out_shape=jax.ShapeDtypeStruct((batch * seq, ATTN_WIDTH), BF16),
        compiler_params=_cparams(("parallel", "arbitrary")),
        name="attn_prompt",
    )(q, k, k, v, v, sinks)


def _attn_sample_kernel(q_ref, kn_ref, vn_ref, ck_ref, cv_ref, sink_ref, o_ref, nk_ref, nv_ref):
    bb, t_len, _ = q_ref.shape
    wb = ck_ref.shape[1]
    n_keys = wb + t_len
    rows = 8 * t_len
    a = lax.broadcasted_iota(jnp.int32, (rows, n_keys), 0) % t_len
    c = lax.broadcasted_iota(jnp.int32, (rows, n_keys), 1)
    mask = ((c < wb) & (c > a + (wb - WINDOW))) | ((c >= wb) & (c - wb <= a))
    lane_lo = lax.broadcasted_iota(jnp.int32, (t_len, LANES), 1) < HEAD_DIM

    def body(b, carry):
        ck = ck_ref[b]
        cv = cv_ref[b]
        kn = kn_ref[b]
        vn = vn_ref[b]
        nk_ref[b, 0:wb - t_len, :] = ck[t_len:, :]
        nk_ref[b, wb - t_len:wb, :] = kn
        nv_ref[b, 0:wb - t_len, :] = cv[t_len:, :]
        nv_ref[b, wb - t_len:wb, :] = vn
        k_all = jnp.concatenate([ck, kn], axis=0).astype(BF16)
        v_all = jnp.concatenate([cv, vn], axis=0).astype(BF16)
        qb = q_ref[b]
        for pair in range(KV_WIDTH // LANES):
            kt = k_all[:, pair * LANES:(pair + 1) * LANES]
            vt = v_all[:, pair * LANES:(pair + 1) * LANES]
            stack = []
            for e in range(2):
                for j in range(4):
                    tile = pair * 4 + j
                    qt = qb[:, tile * LANES:(tile + 1) * LANES]
                    stack.append(jnp.where(lane_lo if e == 0 else jnp.logical_not(lane_lo), qt, jnp.zeros_like(qt)))
            qs = jnp.concatenate(stack, axis=0)
            s = _mm(qs, kt, _NT)
            o = _softmax_pv(s, mask, sink_ref[pair], vt)
            for j in range(4):
                tile = pair * 4 + j
                lo = o[j * t_len:(j + 1) * t_len, :]
                hi = o[(4 + j) * t_len:(5 + j) * t_len, :]
                o_ref[b, :, tile * LANES:(tile + 1) * LANES] = jnp.where(lane_lo, lo, hi).astype(BF16)
        return carry

    lax.fori_loop(0, bb, body, 0)


def _attn_sample(q3, k3, v3, ck, cv, sink_rows, bb):
    dec_b, t_len, _ = q3.shape
    wb = ck.shape[1]
    blk3 = lambda w: pl.BlockSpec((bb, t_len, w), lambda i: (i, 0, 0))
    cache = pl.BlockSpec((bb, wb, KV_WIDTH), lambda i: (i, 0, 0))
    return pl.pallas_call(
        _attn_sample_kernel,
        grid=(dec_b // bb,),
        in_specs=[blk3(ATTN_WIDTH), blk3(KV_WIDTH), blk3(KV_WIDTH), cache, cache,
                  pl.BlockSpec((2, 8 * t_len, 1), lambda i: (0, 0, 0))],
        out_specs=[blk3(ATTN_WIDTH), cache, cache],
        out_shape=[
            jax.ShapeDtypeStruct((dec_b, t_len, ATTN_WIDTH), BF16),
            jax.ShapeDtypeStruct(ck.shape, F32),
            jax.ShapeDtypeStruct(cv.shape, F32),
        ],
        compiler_params=_cparams(("parallel",)),
        name="attn_sample",
    )(q3, k3, v3, ck, cv, sink_rows)


def _seg_sum(x, seg_mat):
    outs = []
    for t in range(x.shape[1] // LANES):
        outs.append(_mm_exact_b(x[:, t * LANES:(t + 1) * LANES], seg_mat))
    return jnp.concatenate(outs, axis=1)


def _chunk_pair(r, lw, k, v, kap, b, s0, chunk):
    c = chunk
    row = lax.broadcasted_iota(jnp.int32, (c, c), 0)
    col = lax.broadcasted_iota(jnp.int32, (c, c), 1)
    incl = row >= col
    strict = row > col
    tri = jnp.where(incl, 1.0, 0.0).astype(BF16)
    cum = _mm_exact_a(tri, lw)
    eg = jnp.exp(cum)
    egp = jnp.exp(cum - lw)
    einv = jnp.exp(-cum)
    kap_t = kap * egp
    r_t = r * eg
    b_t = b * einv
    k_t = k * einv
    g_end = eg[c - 1:c, :]
    lo = lax.broadcasted_iota(jnp.int32, (c, LANES), 1) < HEAD_DIM
    zero = jnp.zeros_like(kap_t)
    l4 = jnp.concatenate([jnp.where(lo, kap_t, zero), jnp.where(lo, zero, kap_t),
                          jnp.where(lo, r_t, zero), jnp.where(lo, zero, r_t)], axis=0)
    gb = _mm_x3(l4, b_t, _NT)
    gk = _mm_x3(l4, k_t, _NT)
    zc = jnp.zeros((c, c), F32)
    eye = jnp.where(row == col, 1.0, 0.0).astype(F32)
    t_inv = []
    for h in range(2):
        n = -jnp.where(strict, gb[h * c:(h + 1) * c], zc)
        p = eye + n
        steps = max(c.bit_length() - 2, 0)
        for _ in range(steps):
            n = _mm_x3(n, n)
            p = p + _mm_x3(p, n)
        t_inv.append(p)
    ak = [jnp.where(strict, gk[h * c:(h + 1) * c], zc) for h in range(2)]
    rb = [jnp.where(incl, gb[(2 + h) * c:(3 + h) * c], zc) for h in range(2)]
    rk = [jnp.where(incl, gk[(2 + h) * c:(3 + h) * c], zc) for h in range(2)]
    sk = _mm_x3(jnp.concatenate([kap_t, r_t], axis=0), s0, _NT)
    k_s = sk[0:c]
    r_s = sk[c:2 * c]
    w = jnp.where(lo, _mm_x3(ak[0], v), _mm_x3(ak[1], v))
    rhs = -(k_s + w)
    u = jnp.where(lo, _mm_x3(t_inv[0], rhs), _mm_x3(t_inv[1], rhs))
    y_lo = _mm_x3(rb[0], u) + _mm_x3(rk[0], v)
    y_hi = _mm_x3(rb[1], u) + _mm_x3(rk[1], v)
    y = r_s + jnp.where(lo, y_lo, y_hi)
    uv = jnp.concatenate([u, v], axis=0)
    bk = jnp.concatenate([b_t, k_t], axis=0)
    d = _mm_x3(uv, bk, _TN)
    rr = lax.broadcasted_iota(jnp.int32, (LANES, LANES), 0) < HEAD_DIM
    cc = lax.broadcasted_iota(jnp.int32, (LANES, LANES), 1) < HEAD_DIM
    s_new = (s0 + jnp.where(rr == cc, d, jnp.zeros_like(d))) * g_end
    return y, s_new


def _rwkv_kernel(p_ref, s0_ref, sh_ref, mu_ref, w0_ref, wd_ref, a0_ref, wa_ref, wg_ref, kk_ref, ka_ref,
                 rk_ref, lw_ref, lb_ref, seg_ref, o_ref, s_out_ref, sh_out_ref, state_scr, prev_scr):
    ci = pl.program_id(1)
    chunk = p_ref.shape[1]

    @pl.when(ci == 0)
    def _():
        state_scr[...] = s0_ref[0]
        prev_scr[...] = sh_ref[0]

    p = p_ref[0]
    row = lax.broadcasted_iota(jnp.int32, p.shape, 0)
    p_prev = jnp.where(row == 0, prev_scr[...], pltpu.roll(p, 1, 0))
    prev_scr[...] = p[chunk - 1:chunk, :]
    xs = p + (p_prev - p) * mu_ref[...]
    r = xs[:, 0:RWKV_WIDTH]
    k = xs[:, RWKV_WIDTH:2 * RWKV_WIDTH]
    v = xs[:, 2 * RWKV_WIDTH:3 * RWKV_WIDTH]
    xw = xs[:, XW_OFF:XW_OFF + LANES]
    xa = xs[:, XA_OFF:XA_OFF + LANES]
    xg = xs[:, XG_OFF:XG_OFF + XG_PAD]
    z = w0_ref[...] + _mm_x3(jnp.tanh(xw), wd_ref[...])
    w_log = -jax.nn.softplus(-z) - 0.5
    lw = -jnp.exp(w_log)
    a = jax.nn.sigmoid(a0_ref[...] + _mm_x3(xa, wa_ref[...]))
    g = _mm_x3(jax.nn.sigmoid(xg), wg_ref[...])
    seg = seg_ref[...]
    kk = k * kk_ref[...]
    kap = kk / jnp.maximum(jnp.sqrt(_seg_sum(kk * kk, seg)), 1e-12)
    k2 = k * (1.0 + (a - 1.0) * ka_ref[...])
    b = kap * a
    bonus = _seg_sum(r * k2 * rk_ref[...], seg) * v
    ys = []
    for t in range(RWKV_WIDTH // LANES):
        sl = slice(t * LANES, (t + 1) * LANES)
        y_t, s_new = _chunk_pair(r[:, sl], lw[:, sl], k2[:, sl], v[:, sl], kap[:, sl], b[:, sl],
                                 state_scr[t], chunk)
        state_scr[t] = s_new
        ys.append(y_t)
    y = jnp.concatenate(ys, axis=1)
    inv_n = 1.0 / HEAD_DIM
    mean = _seg_sum(y, seg) * inv_n
    yc = y - mean
    var = _seg_sum(yc * yc, seg) * inv_n
    yn = yc * lax.rsqrt(var + LNX_EPS) * lw_ref[...] + lb_ref[...]
    o_ref[0] = ((yn + bonus) * g).astype(o_ref.dtype)

    @pl.when(ci == pl.num_programs(1) - 1)
    def _():
        s_out_ref[0] = state_scr[...]
        sh_out_ref[0] = prev_scr[...]


def _rwkv_mix(p3, s0_bd, shift3, prm, chunk):
    batch, t_len, _ = p3.shape
    n_pairs = RWKV_WIDTH // LANES
    const2 = lambda shape: pl.BlockSpec(shape, lambda b, c: (0, 0))
    vec = const2((1, RWKV_WIDTH))
    return pl.pallas_call(
        _rwkv_kernel,
        grid=(batch, t_len // chunk),
        in_specs=[
            pl.BlockSpec((1, chunk, RWKV_PROJ_PAD), lambda b, c: (b, c, 0)),
            pl.BlockSpec((1, n_pairs, LANES, LANES), lambda b, c: (b, 0, 0, 0)),
            pl.BlockSpec((1, 1, RWKV_PROJ_PAD), lambda b, c: (b, 0, 0)),
            const2((1, RWKV_PROJ_PAD)),
            vec,
            const2((LANES, RWKV_WIDTH)),
            vec,
            const2((LANES, RWKV_WIDTH)),
            const2((XG_PAD, RWKV_WIDTH)),
            vec, vec, vec, vec, vec,
            const2((LANES, LANES)),
        ],
        out_specs=[
            pl.BlockSpec((1, chunk, RWKV_WIDTH), lambda b, c: (b, c, 0)),
            pl.BlockSpec((1, n_pairs, LANES, LANES), lambda b, c: (b, 0, 0, 0)),
            pl.BlockSpec((1, 1, RWKV_PROJ_PAD), lambda b, c: (b, 0, 0)),
        ],
        out_shape=[
            jax.ShapeDtypeStruct((batch, t_len, RWKV_WIDTH), BF16),
            jax.ShapeDtypeStruct((batch, n_pairs, LANES, LANES), F32),
            jax.ShapeDtypeStruct((batch, 1, RWKV_PROJ_PAD), F32),
        ],
        scratch_shapes=[pltpu.VMEM((n_pairs, LANES, LANES), F32), pltpu.VMEM((1, RWKV_PROJ_PAD), F32)],
        compiler_params=_cparams(("parallel", "arbitrary")),
        name="rwkv_mix_c%d" % chunk,
    )(p3, s0_bd, shift3, prm["mu"], prm["w0"], prm["wd"], prm["a0"], prm["wa"], prm["wg"], prm["k_k"],
      prm["k_a"], prm["r_k"], prm["lnx_w"], prm["lnx_b"], prm["seg"])


def _out_router_kernel(x_ref, oa_ref, or_ref, wa_ref, wr_ref, g_ref, rw_ref, rb_ref, h_ref, hn_ref, idx_ref,
                       gate_ref):
    h = x_ref[...] + _mm(oa_ref[...], wa_ref[...]) + _mm(or_ref[...], wr_ref[...])
    h_ref[...] = h
    hn = _rms(h, g_ref[...])
    hn_ref[...] = hn
    hh, hl = _split2(hn)
    logits = _mm(hh, rw_ref[0]) + (_mm(hh, rw_ref[1]) + _mm(hl, rw_ref[0])) + rb_ref[...]
    lane = lax.broadcasted_iota(jnp.int32, logits.shape, 1)
    vals = []
    idxs = []
    cur = logits
    for _ in range(TOP_K):
        m = jnp.max(cur, axis=-1, keepdims=True)
        sel = jnp.min(jnp.where(cur == m, lane, LANES), axis=-1, keepdims=True)
        vals.append(m)
        idxs.append(sel)
        cur = jnp.where(lane == sel, -jnp.inf, cur)
    es = [jnp.exp(vj - vals[0]) for vj in vals]
    tot = es[0] + es[1] + es[2] + es[3]
    idx_out = jnp.zeros(logits.shape, jnp.int32)
    gate_out = jnp.zeros(logits.shape, F32)
    for j in range(TOP_K):
        idx_out = jnp.where(lane == j, idxs[j], idx_out)
        gate_out = jnp.where(lane == j, es[j] / tot, gate_out)
    idx_ref[...] = idx_out
    gate_ref[...] = gate_out


def _out_router(x, o_attn, o_rwkv, w_oa, w_or, g, rw_split, rb_pad, tm):
    t_tok = x.shape[0]
    row = lambda w: pl.BlockSpec((tm, w), lambda i: (i, 0))
    full = lambda shape: pl.BlockSpec(shape, lambda i: tuple(0 for _ in shape))
    return pl.pallas_call(
        _out_router_kernel,
        grid=(t_tok // tm,),
        in_specs=[row(D_MODEL), row(ATTN_WIDTH), row(RWKV_WIDTH), full((ATTN_WIDTH, D_MODEL)),
                  full((RWKV_WIDTH, D_MODEL)), full((1, D_MODEL)), full((2, D_MODEL, LANES)), full((1, LANES))],
        out_specs=[row(D_MODEL), row(D_MODEL), row(LANES), row(LANES)],
        out_shape=[
            jax.ShapeDtypeStruct((t_tok, D_MODEL), F32),
            jax.ShapeDtypeStruct((t_tok, D_MODEL), F32),
            jax.ShapeDtypeStruct((t_tok, LANES), jnp.int32),
            jax.ShapeDtypeStruct((t_tok, LANES), F32),
        ],
        compiler_params=_cparams(("parallel",)),
        name="out_router",
    )(x, o_attn, o_rwkv, w_oa, w_or, g, rw_split, rb_pad)


def _gather_kernel(tok_ref, hn_ref, o_ref, buf, sem):
    rows = buf.shape[0]

    def issue(r, carry):
        pltpu.make_async_copy(hn_ref.at[pl.ds(tok_ref[r], 1)], buf.at[pl.ds(r, 1)], sem).start()
        return carry

    lax.fori_loop(0, rows, issue, 0)

    def drain(r, carry):
        pltpu.make_async_copy(hn_ref.at[pl.ds(0, 1)], buf.at[pl.ds(r, 1)], sem).wait()
        return carry

    lax.fori_loop(0, rows, drain, 0)
    o_ref[...] = buf[...].astype(o_ref.dtype)


def _gather_rows(row_tok, hn, rows_per_step):
    n_rows = row_tok.shape[0]
    return pl.pallas_call(
        _gather_kernel,
        grid=(n_rows // rows_per_step,),
        in_specs=[
            pl.BlockSpec((rows_per_step,), lambda i: (i,), memory_space=pltpu.SMEM),
            pl.BlockSpec(memory_space=pl.ANY),
        ],
        out_specs=pl.BlockSpec((rows_per_step, D_MODEL), lambda i: (i, 0)),
        out_shape=jax.ShapeDtypeStruct((n_rows, D_MODEL), BF16),
        scratch_shapes=[pltpu.VMEM((rows_per_step, D_MODEL), F32), pltpu.SemaphoreType.DMA(())],
        compiler_params=_cparams(("arbitrary",)),
        name="moe_gather",
    )(row_tok, hn)


def _expert_kernel(be_ref, x_ref, wg_ref, wl_ref, bg_ref, bl_ref, wd_ref, bd_ref, o_ref, acc_ref):
    f = pl.program_id(1)

    @pl.when(f == 0)
    def _():
        acc_ref[...] = jnp.broadcast_to(bd_ref[0], acc_ref.shape)

    x = x_ref[...]
    hg = jnp.minimum(_mm(x, wg_ref[0]) + bg_ref[0], SWIGLU_LIMIT)
    hl = jnp.clip(_mm(x, wl_ref[0]) + bl_ref[0], -SWIGLU_LIMIT, SWIGLU_LIMIT)
    act = hg * jax.nn.sigmoid(SWIGLU_ALPHA * hg) * (hl + 1.0)
    acc_ref[...] += _mm(act.astype(BF16), wd_ref[0].astype(BF16))

    @pl.when(f == pl.num_programs(1) - 1)
    def _():
        o_ref[...] = acc_ref[...]


def _experts(blk_expert, x_sorted, w_glu, w_lin, b_glu, b_lin, w_down, b_down, bm, tf):
    n_rows = x_sorted.shape[0]
    grid_spec = pltpu.PrefetchScalarGridSpec(
        num_scalar_prefetch=1,
        grid=(n_rows // bm, D_FF // tf),
        in_specs=[
            pl.BlockSpec((bm, D_MODEL), lambda i, f, be: (i, 0)),
            pl.BlockSpec((1, D_MODEL, tf), lambda i, f, be: (be[i], 0, f)),
            pl.BlockSpec((1, D_MODEL, tf), lambda i, f, be: (be[i], 0, f)),
            pl.BlockSpec((1, 1, tf), lambda i, f, be: (be[i], 0, f)),
            pl.BlockSpec((1, 1, tf), lambda i, f, be: (be[i], 0, f)),
            pl.BlockSpec((1, tf, D_MODEL), lambda i, f, be: (be[i], f, 0)),
            pl.BlockSpec((1, 1, D_MODEL), lambda i, f, be: (be[i], 0, 0)),
        ],
        out_specs=pl.BlockSpec((bm, D_MODEL), lambda i, f, be: (i, 0)),
        scratch_shapes=[pltpu.VMEM((bm, D_MODEL), F32)],
    )
    return pl.pallas_call(
        _expert_kernel,
        grid_spec=grid_spec,
        out_shape=jax.ShapeDtypeStruct((n_rows, D_MODEL), F32),
        compiler_params=_cparams(("arbitrary", "arbitrary")),
        name="moe_experts",
    )(blk_expert, x_sorted, w_glu, w_lin, b_glu, b_lin, w_down, b_down)


def _combine_kernel(dest_ref, rows_ref, h_ref, gate_ref, g_ref, o_ref, buf, sem):
    tm = h_ref.shape[0]

    def issue(t, carry):
        for j in range(TOP_K):
            pltpu.make_async_copy(rows_ref.at[pl.ds(dest_ref[t * TOP_K + j], 1)], buf.at[j, pl.ds(t, 1)],
                                  sem).start()
        return carry

    lax.fori_loop(0, tm, issue, 0)

    def drain(t, carry):
        for j in range(TOP_K):
            pltpu.make_async_copy(rows_ref.at[pl.ds(0, 1)], buf.at[j, pl.ds(t, 1)], sem).wait()
        return carry

    lax.fori_loop(0, tm, drain, 0)
    gates = gate_ref[...]
    y = h_ref[...]
    for j in range(TOP_K):
        y = y + buf[j] * gates[:, j:j + 1]
    o_ref[...] = _rms(y, g_ref[...])


def _combine(dest, out_rows, h, gates, g, tm):
    t_tok = h.shape[0]
    return pl.pallas_call(
        _combine_kernel,
        grid=(t_tok // tm,),
        in_specs=[
            pl.BlockSpec((tm * TOP_K,), lambda i: (i,), memory_space=pltpu.SMEM),
            pl.BlockSpec(memory_space=pl.ANY),
            pl.BlockSpec((tm, D_MODEL), lambda i: (i, 0)),
            pl.BlockSpec((tm, LANES), lambda i: (i, 0)),
            pl.BlockSpec((1, D_MODEL), lambda i: (0, 0)),
        ],
        out_specs=pl.BlockSpec((tm, D_MODEL), lambda i: (i, 0)),
        out_shape=jax.ShapeDtypeStruct((t_tok, D_MODEL), F32),
        scratch_shapes=[pltpu.VMEM((TOP_K, tm, D_MODEL), F32), pltpu.SemaphoreType.DMA(())],
        compiler_params=_cparams(("arbitrary",)),
        name="moe_combine",
    )(dest, out_rows, h, gates, g)


def _q_head_order():
    order = []
    for pair in range(2):
        for j in range(4):
            order += [8 * pair + j, 8 * pair + 4 + j]
    return order


def _q_col_perm():
    cols = []
    for h in _q_head_order():
        cols += list(range(h * HEAD_DIM, (h + 1) * HEAD_DIM))
    return jnp.asarray(cols, jnp.int32)


def _pad_rwkv_cols(m):
    def padw(a, w):
        return jnp.pad(a, [(0, 0)] * (a.ndim - 1) + [(0, w - a.shape[-1])])
    return jnp.concatenate([
        m[..., :XW_OFF],
        padw(m[..., XW_OFF:XW_OFF + DECAY_LORA], LANES),
        padw(m[..., XW_OFF + DECAY_LORA:XW_OFF + DECAY_LORA + ICLR_LORA], LANES),
        padw(m[..., XW_OFF + DECAY_LORA + ICLR_LORA:], XG_PAD),
    ], axis=-1)


def _unpad_rwkv_cols(m):
    return jnp.concatenate([
        m[..., :XW_OFF + DECAY_LORA],
        m[..., XA_OFF:XA_OFF + ICLR_LORA],
        m[..., XG_OFF:XG_OFF + GATE_LORA],
    ], axis=-1)


def _pad_rows(m, rows):
    return jnp.pad(m, ((0, rows - m.shape[0]), (0, 0)))


def _rope_tables(pos):
    half = HEAD_DIM // 2
    inv_freq = ROPE_THETA ** (-jnp.arange(half, dtype=F32) / half)
    ang = pos.astype(F32)[:, None] * inv_freq[None, :]
    cos = jnp.cos(ang)
    sin = jnp.sin(ang)
    cos_t = jnp.tile(jnp.concatenate([cos, cos], axis=-1), (1, LANES // HEAD_DIM))
    sin_t = jnp.tile(jnp.concatenate([-sin, sin], axis=-1), (1, LANES // HEAD_DIM))
    return cos_t, sin_t


def _state_to_blockdiag(s):
    b = s.shape[0]
    s = s.reshape(b, RWKV_WIDTH // LANES, 2, HEAD_DIM, HEAD_DIM)
    z = jnp.zeros_like(s[:, :, 0])
    top = jnp.concatenate([s[:, :, 0], z], axis=-1)
    bot = jnp.concatenate([z, s[:, :, 1]], axis=-1)
    return jnp.concatenate([top, bot], axis=-2)


def _blockdiag_to_state(s):
    b = s.shape[0]
    lo = s[:, :, :HEAD_DIM, :HEAD_DIM]
    hi = s[:, :, HEAD_DIM:, HEAD_DIM:]
    return jnp.stack([lo, hi], axis=2).reshape(b, RWKV_WIDTH // HEAD_DIM, HEAD_DIM, HEAD_DIM)


def _route(top_idx, bm):
    n_tok = top_idx.shape[0]
    e_flat = top_idx.reshape(-1)
    onehot = (e_flat[:, None] == jnp.arange(N_EXPERTS, dtype=jnp.int32)[None, :]).astype(jnp.int32)
    csum = jnp.cumsum(onehot, axis=0)
    rank = jnp.sum(csum * onehot, axis=1) - 1
    counts = csum[-1]
    padded = ((counts + bm - 1) // bm) * bm
    p_end = jnp.cumsum(padded)
    p_start = p_end - padded
    dest = (p_start[e_flat] + rank).astype(jnp.int32)
    n_assign = n_tok * TOP_K
    n_blocks = -(-n_assign // bm) + N_EXPERTS
    n_rows = n_blocks * bm
    tok_flat = jnp.repeat(jnp.arange(n_tok, dtype=jnp.int32), TOP_K)
    row_tok = jnp.zeros((n_rows,), jnp.int32).at[dest].set(tok_flat)
    blk_start = jnp.arange(n_blocks, dtype=jnp.int32) * bm
    blk_expert = jnp.minimum(jnp.searchsorted(p_end, blk_start, side="right"), N_EXPERTS - 1).astype(jnp.int32)
    return dest, row_tok, blk_expert


def _pick(n, prefs):
    for p in prefs:
        if n % p == 0:
            return p
    return n


def kernel(x_prompt, x_sample, cache_k, cache_v, state_wkv, state_shift, attn_norm_g, w_in, attn_sinks, mu_shift, decay_w0, decay_lora_up, iclr_a0, iclr_lora_up, gate_lora_up, k_k, k_a, r_k, lnx_w, lnx_b, w_out, ffn_norm_g, router_w, router_b, w_up, b_up, w_down, b_down, final_norm_g):
    depth = w_in.shape[0]
    assert depth == 1
    batch, seq, _ = x_prompt.shape
    dec_b, dec_t, _ = x_sample.shape
    n_p = batch * seq
    n_s = dec_b * dec_t
    l = 0

    qperm = _q_col_perm()
    w_in_l = w_in[l]
    w_attn = jnp.concatenate([w_in_l[:, :ATTN_WIDTH][:, qperm], w_in_l[:, ATTN_WIDTH:ATTN_WIDTH + 2 * KV_WIDTH]],
                             axis=1).astype(BF16)
    w_rwkv = _pad_rwkv_cols(w_in_l[:, ATTN_WIDTH + 2 * KV_WIDTH:]).astype(BF16)
    w_oa = w_out[l][:ATTN_WIDTH][qperm].astype(BF16)
    w_or = w_out[l][ATTN_WIDTH:].astype(BF16)
    g_attn = attn_norm_g[l][None, :]
    g_ffn = ffn_norm_g[l][None, :]
    order = jnp.asarray(_q_head_order(), jnp.int32)
    sinks_perm = attn_sinks[l][order].astype(F32)
    sinks_true = attn_sinks[l].astype(F32)
    seg = (jnp.arange(LANES)[:, None] // HEAD_DIM == jnp.arange(LANES)[None, :] // HEAD_DIM).astype(BF16)
    prm = dict(
        mu=_pad_rwkv_cols(mu_shift[l][None, :]),
        w0=decay_w0[l][None, :], wd=_pad_rows(decay_lora_up[l], LANES),
        a0=iclr_a0[l][None, :], wa=_pad_rows(iclr_lora_up[l], LANES),
        wg=_pad_rows(gate_lora_up[l], XG_PAD),
        k_k=k_k[l][None, :], k_a=k_a[l][None, :], r_k=r_k[l].reshape(1, RWKV_WIDTH),
        lnx_w=lnx_w[l][None, :], lnx_b=lnx_b[l][None, :], seg=seg)
    rw_pad = jnp.pad(router_w[l], ((0, 0), (0, LANES - N_EXPERTS)))
    rw_hi = rw_pad.astype(BF16)
    rw_lo = (rw_pad - rw_hi.astype(F32)).astype(BF16)
    rw_split = jnp.stack([rw_hi, rw_lo])
    rb_pad = jnp.concatenate([router_b[l], jnp.full((LANES - N_EXPERTS,), NEG_BIG, F32)])[None, :]
    w_glu = w_up[l][:, :, 0::2].astype(BF16)
    w_lin = w_up[l][:, :, 1::2].astype(BF16)
    b_glu = b_up[l][:, None, 0::2]
    b_lin = b_up[l][:, None, 1::2]
    b_dn = b_down[l][:, None, :]

    tm_p = _pick(seq, (512, 256, 128))
    cos_p, sin_p = _rope_tables(jnp.arange(seq, dtype=jnp.int32))
    q_p, k_p, v_p = _attn_proj(x_prompt.reshape(n_p, D_MODEL), g_attn, w_attn, cos_p, sin_p, tm_p)
    tm_s = _pick(n_s, (512, 256, 128, 64, 32, 16, 8))
    pos_s = PAST_LEN + (jnp.arange(n_s, dtype=jnp.int32) % dec_t)
    cos_s, sin_s = _rope_tables(pos_s)
    q_s, k_s, v_s = _attn_proj(x_sample.reshape(n_s, D_MODEL), g_attn, w_attn, cos_s, sin_s, tm_s)
    tn = 512
    pr_p = _rwkv_proj(x_prompt.reshape(n_p, D_MODEL), g_attn, w_rwkv, _pick(n_p, (1024, 512, 256, 128)), tn)
    pr_s = _rwkv_proj(x_sample.reshape(n_s, D_MODEL), g_attn, w_rwkv, tm_s, tn)

    oa_p = _attn_prompt(q_p, k_p, v_p, sinks_true, batch, seq)
    wb = cache_k.shape[2]
    sink_rows = jnp.stack([
        jnp.repeat(jnp.stack([sinks_true[8 * pair + 4 * e + j] for e in range(2) for j in range(4)]), dec_t)
        for pair in range(2)])[:, :, None]
    oa_s, nk_s, nv_s = _attn_sample(
        q_s.reshape(dec_b, dec_t, ATTN_WIDTH), k_s.reshape(dec_b, dec_t, KV_WIDTH),
        v_s.reshape(dec_b, dec_t, KV_WIDTH), cache_k[l].reshape(dec_b, wb, KV_WIDTH),
        cache_v[l].reshape(dec_b, wb, KV_WIDTH), sink_rows, _pick(dec_b, (8, 4, 2, 1)))
    del sinks_perm

    zero_state = jnp.zeros((batch, RWKV_WIDTH // LANES, LANES, LANES), F32)
    zero_shift = jnp.zeros((batch, 1, RWKV_PROJ_PAD), F32)
    or_p, st_p, sh_p = _rwkv_mix(pr_p.reshape(batch, seq, RWKV_PROJ_PAD), zero_state, zero_shift, prm,
                                 _pick(seq, (64, 32, 16, 8)))
    or_s, st_s, sh_s = _rwkv_mix(pr_s.reshape(dec_b, dec_t, RWKV_PROJ_PAD), _state_to_blockdiag(state_wkv[l]),
                                 _pad_rwkv_cols(state_shift[l])[:, None, :], prm, dec_t)

    x_all = jnp.concatenate([x_prompt.reshape(n_p, D_MODEL), x_sample.reshape(n_s, D_MODEL)], axis=0)
    oa_all = jnp.concatenate([oa_p, oa_s.reshape(n_s, ATTN_WIDTH)], axis=0)
    or_all = jnp.concatenate([or_p.reshape(n_p, RWKV_WIDTH), or_s.reshape(n_s, RWKV_WIDTH)], axis=0)
    n_all = n_p + n_s
    tm_o = _pick(n_all, (512, 256, 128, 64, 32, 16, 8))
    h, hn, idx_pad, gate_pad = _out_router(x_all, oa_all, or_all, w_oa, w_or, g_ffn, rw_split, rb_pad, tm_o)

    bm = 512 if n_all * TOP_K >= 512 * N_EXPERTS else 128
    dest, row_tok, blk_expert = _route(idx_pad[:, :TOP_K], bm)
    x_sorted = _gather_rows(row_tok, hn, min(bm, 256))
    out_rows = _experts(blk_expert, x_sorted, w_glu, w_lin, b_glu, b_lin, w_down[l], b_dn, bm, 512)
    y_all = _combine(dest, out_rows, h, gate_pad, final_norm_g[None, :], _pick(n_all, (128, 64, 32, 16, 8)))

    y_prompt = y_all[:n_p].reshape(batch, seq, D_MODEL)
    y_sample = y_all[n_p:].reshape(dec_b, dec_t, D_MODEL)
    kp = k_p.reshape(batch, seq, 4, HEAD_DIM)[:, -WINDOW:][None]
    vp = v_p.reshape(batch, seq, 4, HEAD_DIM)[:, -WINDOW:][None]
    wp = _blockdiag_to_state(st_p)[None]
    sp = _unpad_rwkv_cols(sh_p[:, 0, :])[None]
    ks = nk_s.reshape(dec_b, wb, 4, HEAD_DIM)[None]
    vs = nv_s.reshape(dec_b, wb, 4, HEAD_DIM)[None]
    ws = _blockdiag_to_state(st_s)[None]
    ss = _unpad_rwkv_cols(sh_s[:, 0, :])[None]
    return (y_prompt, y_sample, kp, vp, wp, sp, ks, vs, ws, ss)
```

```python
import functools

import jax
import jax.numpy as jnp
from jax import lax
from jax.experimental import pallas as pl
from jax.experimental.pallas import tpu as pltpu

F32 = jnp.float32
BF16 = jnp.bfloat16

D_MODEL = 2048
HEAD_DIM = 64
LANES = 128
ATTN_WIDTH = 1024
KV_WIDTH = 256
ATTN_HEADS = 16
RWKV_WIDTH = 1024
DECAY_LORA = 64
ICLR_LORA = 64
GATE_LORA = 160
RWKV_PROJ = 3 * RWKV_WIDTH + DECAY_LORA + ICLR_LORA + GATE_LORA
XW_OFF = 3 * RWKV_WIDTH
XA_OFF = XW_OFF + LANES
XG_OFF = XA_OFF + LANES
XG_PAD = 2 * LANES
RWKV_PROJ_PAD = XG_OFF + XG_PAD
WINDOW = 128
ROPE_THETA = 10000.0
PAST_LEN = 8192
N_EXPERTS = 32
TOP_K = 4
D_FF = 2048
SWIGLU_ALPHA = 1.702
SWIGLU_LIMIT = 7.0
NORM_EPS = 1e-5
LNX_EPS = 64e-5
NEG_BIG = -1e30
VMEM_LIMIT = 56 * 1024 * 1024


def _cparams(sem):
    return pltpu.CompilerParams(dimension_semantics=sem, vmem_limit_bytes=VMEM_LIMIT)


_NN = (((1,), (0,)), ((), ()))
_NT = (((1,), (1,)), ((), ()))
_TN = (((0,), (0,)), ((), ()))


def _mm(a, b, dims=_NN):
    return lax.dot_general(a, b, dims, preferred_element_type=F32)


def _split2(x):
    hi = x.astype(BF16)
    lo = (x - hi.astype(F32)).astype(BF16)
    return hi, lo


def _split3(x):
    hi = x.astype(BF16)
    r1 = x - hi.astype(F32)
    mid = r1.astype(BF16)
    lo = (r1 - mid.astype(F32)).astype(BF16)
    return hi, mid, lo


def _mm_x3(a, b, dims=_NN):
    ah, al = _split2(a)
    bh, bl = _split2(b)
    return _mm(ah, bh, dims) + (_mm(ah, bl, dims) + _mm(al, bh, dims))


def _mm_exact_b(a, b_bf16, dims=_NN):
    h, m, l = _split3(a)
    return _mm(h, b_bf16, dims) + (_mm(m, b_bf16, dims) + _mm(l, b_bf16, dims))


def _mm_exact_a(a_bf16, b, dims=_NN):
    h, m, l = _split3(b)
    return _mm(a_bf16, h, dims) + (_mm(a_bf16, m, dims) + _mm(a_bf16, l, dims))


def _rms(x, g):
    ms = jnp.mean(x * x, axis=-1, keepdims=True)
    return (x * lax.rsqrt(ms + NORM_EPS)) * g


def _attn_proj_kernel(x_ref, g_ref, w_ref, cos_ref, sin_ref, q_ref, k_ref, v_ref):
    xn = _rms(x_ref[...], g_ref[...]).astype(BF16)
    a = _mm(xn, w_ref[...])
    cos = cos_ref[...]
    sin = sin_ref[...]
    lane = lax.broadcasted_iota(jnp.int32, cos.shape, 1)
    first_half = (lane % HEAD_DIM) < (HEAD_DIM // 2)
    n_rot = (ATTN_WIDTH + KV_WIDTH) // LANES
    for j in range(n_rot):
        t = a[:, j * LANES:(j + 1) * LANES]
        swapped = jnp.where(first_half, pltpu.roll(t, LANES - HEAD_DIM // 2, 1), pltpu.roll(t, HEAD_DIM // 2, 1))
        rot = t * cos + swapped * sin
        if j < ATTN_WIDTH // LANES:
            q_ref[:, j * LANES:(j + 1) * LANES] = (rot * (HEAD_DIM ** -0.5)).astype(BF16)
        else:
            jj = j - ATTN_WIDTH // LANES
            k_ref[:, jj * LANES:(jj + 1) * LANES] = rot
    v_ref[...] = a[:, ATTN_WIDTH + KV_WIDTH:]


def _attn_proj(x, g, w_attn, cos_t, sin_t, tm):
    t_tok = x.shape[0]
    n_pos_blocks = cos_t.shape[0] // tm
    return pl.pallas_call(
        _attn_proj_kernel,
        grid=(t_tok // tm,),
        in_specs=[
            pl.BlockSpec((tm, D_MODEL), lambda i: (i, 0)),
            pl.BlockSpec((1, D_MODEL), lambda i: (0, 0)),
            pl.BlockSpec((D_MODEL, ATTN_WIDTH + 2 * KV_WIDTH), lambda i: (0, 0)),
            pl.BlockSpec((tm, LANES), lambda i: (i % n_pos_blocks, 0)),
            pl.BlockSpec((tm, LANES), lambda i: (i % n_pos_blocks, 0)),
        ],
        out_specs=[
            pl.BlockSpec((tm, ATTN_WIDTH), lambda i: (i, 0)),
            pl.BlockSpec((tm, KV_WIDTH), lambda i: (i, 0)),
            pl.BlockSpec((tm, KV_WIDTH), lambda i: (i, 0)),
        ],
        out_shape=[
            jax.ShapeDtypeStruct((t_tok, ATTN_WIDTH), BF16),
            jax.ShapeDtypeStruct((t_tok, KV_WIDTH), F32),
            jax.ShapeDtypeStruct((t_tok, KV_WIDTH), F32),
        ],
        compiler_params=_cparams(("parallel",)),
        name="attn_proj",
    )(x, g, w_attn, cos_t, sin_t)


def _rwkv_proj_kernel(x_ref, g_ref, w_ref, p_ref, xn_ref):
    @pl.when(pl.program_id(1) == 0)
    def _():
        xn_ref[...] = _rms(x_ref[...], g_ref[...]).astype(BF16)

    p_ref[...] = _mm(xn_ref[...], w_ref[...])


def _rwkv_proj(x, g, w_rwkv, tm, tn):
    t_tok = x.shape[0]
    return pl.pallas_call(
        _rwkv_proj_kernel,
        grid=(t_tok // tm, RWKV_PROJ_PAD // tn),
        in_specs=[
            pl.BlockSpec((tm, D_MODEL), lambda i, n: (i, 0)),
            pl.BlockSpec((1, D_MODEL), lambda i, n: (0, 0)),
            pl.BlockSpec((D_MODEL, tn), lambda i, n: (0, n)),
        ],
        out_specs=pl.BlockSpec((tm, tn), lambda i, n: (i, n)),
        out_shape=jax.ShapeDtypeStruct((t_tok, RWKV_PROJ_PAD), F32),
        scratch_shapes=[pltpu.VMEM((tm, D_MODEL), BF16)],
        compiler_params=_cparams(("parallel", "arbitrary")),
        name="rwkv_proj",
    )(x, g, w_rwkv)


def _softmax_pv(s, mask, sink, vt):
    s = jnp.where(mask, s, NEG_BIG)
    m = jnp.maximum(jnp.max(s, axis=-1, keepdims=True), sink)
    p = jnp.exp(s - m)
    denom = jnp.sum(p, axis=-1, keepdims=True) + jnp.exp(sink - m)
    return _mm(p.astype(BF16), vt) * (1.0 / denom)


def _attn_prompt_kernel(q_ref, kp_ref, kc_ref, vp_ref, vc_ref, sink_ref, o_ref):
    i = pl.program_id(1)
    blk = q_ref.shape[0]
    kk = jnp.concatenate([kp_ref[...], kc_ref[...]], axis=0).astype(BF16)
    vv = jnp.concatenate([vp_ref[...], vc_ref[...]], axis=0).astype(BF16)
    a = lax.broadcasted_iota(jnp.int32, (blk, 2 * blk), 0)
    c = lax.broadcasted_iota(jnp.int32, (blk, 2 * blk), 1)
    mask = (c > a) & (c <= a + blk) & ((c >= blk) | (i > 0))
    lane_lo = lax.broadcasted_iota(jnp.int32, (blk, LANES), 1) < HEAD_DIM
    for pair in range(KV_WIDTH // LANES):
        kt = kk[:, pair * LANES:(pair + 1) * LANES]
        vt = vv[:, pair * LANES:(pair + 1) * LANES]
        for j in range(4):
            tile = pair * 4 + j
            qt = q_ref[:, tile * LANES:(tile + 1) * LANES]
            halves = []
            for e in range(2):
                qm = jnp.where(lane_lo if e == 0 else jnp.logical_not(lane_lo), qt, jnp.zeros_like(qt))
                s = _mm(qm, kt, _NT)
                halves.append(_softmax_pv(s, mask, sink_ref[8 * pair + 4 * e + j], vt))
            o_ref[:, tile * LANES:(tile + 1) * LANES] = jnp.where(lane_lo, halves[0], halves[1]).astype(BF16)


def _attn_prompt(q, k, v, sinks, batch, seq):
    blk = WINDOW
    nb = seq // blk
    cur = lambda b, i: (b * nb + i, 0)
    prev = lambda b, i: (b * nb + jnp.maximum(i - 1, 0), 0)
    return pl.pallas_call(
        _attn_prompt_kernel,
        grid=(batch, nb),
        in_specs=[
            pl.BlockSpec((blk, ATTN_WIDTH), cur),
            pl.BlockSpec((blk, KV_WIDTH), prev),
            pl.BlockSpec((blk, KV_WIDTH), cur),
            pl.BlockSpec((blk, KV_WIDTH), prev),
            pl.BlockSpec((blk, KV_WIDTH), cur),
            pl.BlockSpec(memory_space=pltpu.SMEM),
        ],
        out_specs=pl.BlockSpec((blk, ATTN_WIDTH), cur),
        out_shape=jax.ShapeDtypeStruct((batch * seq, ATTN_WIDTH), BF16),
        compiler_params=_cparams(("parallel", "arbitrary")),
        name="attn_prompt",
    )(q, k, k, v, v, sinks)


def _attn_sample_kernel(q_ref, kn_ref, vn_ref, ck_ref, cv_ref, sink_ref, o_ref, nk_ref, nv_ref):
    bb, t_len, _ = q_ref.shape
    wb = ck_ref.shape[1]
    n_keys = wb + t_len
    rows = 8 * t_len
    a = lax.broadcasted_iota(jnp.int32, (rows, n_keys), 0) % t_len
    c = lax.broadcasted_iota(jnp.int32, (rows, n_keys), 1)
    mask = ((c < wb) & (c > a + (wb - WINDOW))) | ((c >= wb) & (c - wb <= a))
    lane_lo = lax.broadcasted_iota(jnp.int32, (t_len, LANES), 1) < HEAD_DIM

    def body(b, carry):
        ck = ck_ref[b]
        cv = cv_ref[b]
        kn = kn_ref[b]
        vn = vn_ref[b]
        nk_ref[b, 0:wb - t_len, :] = ck[t_len:, :]
        nk_ref[b, wb - t_len:wb, :] = kn
        nv_ref[b, 0:wb - t_len, :] = cv[t_len:, :]
        nv_ref[b, wb - t_len:wb, :] = vn
        k_all = jnp.concatenate([ck, kn], axis=0).astype(BF16)
        v_all = jnp.concatenate([cv, vn], axis=0).astype(BF16)
        qb = q_ref[b]
        for pair in range(KV_WIDTH // LANES):
            kt = k_all[:, pair * LANES:(pair + 1) * LANES]
            vt = v_all[:, pair * LANES:(pair + 1) * LANES]
            stack = []
            for e in range(2):
                for j in range(4):
                    tile = pair * 4 + j
                    qt = qb[:, tile * LANES:(tile + 1) * LANES]
                    stack.append(jnp.where(lane_lo if e == 0 else jnp.logical_not(lane_lo), qt, jnp.zeros_like(qt)))
            qs = jnp.concatenate(stack, axis=0)
            s = _mm(qs, kt, _NT)
            o = _softmax_pv(s, mask, sink_ref[pair], vt)
            for j in range(4):
                tile = pair * 4 + j
                lo = o[j * t_len:(j + 1) * t_len, :]
                hi = o[(4 + j) * t_len:(5 + j) * t_len, :]
                o_ref[b, :, tile * LANES:(tile + 1) * LANES] = jnp.where(lane_lo, lo, hi).astype(BF16)
        return carry

    lax.fori_loop(0, bb, body, 0)


def _attn_sample(q3, k3, v3, ck, cv, sink_rows, bb):
    dec_b, t_len, _ = q3.shape
    wb = ck.shape[1]
    blk3 = lambda w: pl.BlockSpec((bb, t_len, w), lambda i: (i, 0, 0))
    cache = pl.BlockSpec((bb, wb, KV_WIDTH), lambda i: (i, 0, 0))
    return pl.pallas_call(
        _attn_sample_kernel,
        grid=(dec_b // bb,),
        in_specs=[blk3(ATTN_WIDTH), blk3(KV_WIDTH), blk3(KV_WIDTH), cache, cache,
                  pl.BlockSpec((2, 8 * t_len, 1), lambda i: (0, 0, 0))],
        out_specs=[blk3(ATTN_WIDTH), cache, cache],
        out_shape=[
            jax.ShapeDtypeStruct((dec_b, t_len, ATTN_WIDTH), BF16),
            jax.ShapeDtypeStruct(ck.shape, F32),
            jax.ShapeDtypeStruct(cv.shape, F32),
        ],
        compiler_params=_cparams(("parallel",)),
        name="attn_sample",
    )(q3, k3, v3, ck, cv, sink_rows)


def _seg_sum(x, seg_mat):
    outs = []
    for t in range(x.shape[1] // LANES):
        outs.append(_mm_exact_b(x[:, t * LANES:(t + 1) * LANES], seg_mat))
    return jnp.concatenate(outs, axis=1)


def _chunk_pair(r, lw, k, v, kap, b, s0, chunk):
    c = chunk
    row = lax.broadcasted_iota(jnp.int32, (c, c), 0)
    col = lax.broadcasted_iota(jnp.int32, (c, c), 1)
    incl = row >= col
    strict = row > col
    tri = jnp.where(incl, 1.0, 0.0).astype(BF16)
    cum = _mm_exact_a(tri, lw)
    eg = jnp.exp(cum)
    egp = jnp.exp(cum - lw)
    einv = jnp.exp(-cum)
    kap_t = kap * egp
    r_t = r * eg
    b_t = b * einv
    k_t = k * einv
    g_end = eg[c - 1:c, :]
    lo = lax.broadcasted_iota(jnp.int32, (c, LANES), 1) < HEAD_DIM
    zero = jnp.zeros_like(kap_t)
    l4 = jnp.concatenate([jnp.where(lo, kap_t, zero), jnp.where(lo, zero, kap_t),
                          jnp.where(lo, r_t, zero), jnp.where(lo, zero, r_t)], axis=0)
    gb = _mm_x3(l4, b_t, _NT)
    gk = _mm_x3(l4, k_t, _NT)
    zc = jnp.zeros((c, c), F32)
    eye = jnp.where(row == col, 1.0, 0.0).astype(F32)
    t_inv = []
    for h in range(2):
        n = -jnp.where(strict, gb[h * c:(h + 1) * c], zc)
        p = eye + n
        steps = max(c.bit_length() - 2, 0)
        for _ in range(steps):
            n = _mm_x3(n, n)
            p = p + _mm_x3(p, n)
        t_inv.append(p)
    ak = [jnp.where(strict, gk[h * c:(h + 1) * c], zc) for h in range(2)]
    rb = [jnp.where(incl, gb[(2 + h) * c:(3 + h) * c], zc) for h in range(2)]
    rk = [jnp.where(incl, gk[(2 + h) * c:(3 + h) * c], zc) for h in range(2)]
    sk = _mm_x3(jnp.concatenate([kap_t, r_t], axis=0), s0, _NT)
    k_s = sk[0:c]
    r_s = sk[c:2 * c]
    w = jnp.where(lo, _mm_x3(ak[0], v), _mm_x3(ak[1], v))
    rhs = -(k_s + w)
    u = jnp.where(lo, _mm_x3(t_inv[0], rhs), _mm_x3(t_inv[1], rhs))
    y_lo = _mm_x3(rb[0], u) + _mm_x3(rk[0], v)
    y_hi = _mm_x3(rb[1], u) + _mm_x3(rk[1], v)
    y = r_s + jnp.where(lo, y_lo, y_hi)
    uv = jnp.concatenate([u, v], axis=0)
    bk = jnp.concatenate([b_t, k_t], axis=0)
    d = _mm_x3(uv, bk, _TN)
    rr = lax.broadcasted_iota(jnp.int32, (LANES, LANES), 0) < HEAD_DIM
    cc = lax.broadcasted_iota(jnp.int32, (LANES, LANES), 1) < HEAD_DIM
    s_new = (s0 + jnp.where(rr == cc, d, jnp.zeros_like(d))) * g_end
    return y, s_new


def _rwkv_kernel(p_ref, s0_ref, sh_ref, mu_ref, w0_ref, wd_ref, a0_ref, wa_ref, wg_ref, kk_ref, ka_ref,
                 rk_ref, lw_ref, lb_ref, seg_ref, o_ref, s_out_ref, sh_out_ref, state_scr, prev_scr):
    ci = pl.program_id(1)
    chunk = p_ref.shape[1]

    @pl.when(ci == 0)
    def _():
        state_scr[...] = s0_ref[0]
        prev_scr[...] = sh_ref[0]

    p = p_ref[0]
    row = lax.broadcasted_iota(jnp.int32, p.shape, 0)
    p_prev = jnp.where(row == 0, prev_scr[...], pltpu.roll(p, 1, 0))
    prev_scr[...] = p[chunk - 1:chunk, :]
    xs = p + (p_prev - p) * mu_ref[...]
    r = xs[:, 0:RWKV_WIDTH]
    k = xs[:, RWKV_WIDTH:2 * RWKV_WIDTH]
    v = xs[:, 2 * RWKV_WIDTH:3 * RWKV_WIDTH]
    xw = xs[:, XW_OFF:XW_OFF + LANES]
    xa = xs[:, XA_OFF:XA_OFF + LANES]
    xg = xs[:, XG_OFF:XG_OFF + XG_PAD]
    z = w0_ref[...] + _mm_x3(jnp.tanh(xw), wd_ref[...])
    w_log = -jax.nn.softplus(-z) - 0.5
    lw = -jnp.exp(w_log)
    a = jax.nn.sigmoid(a0_ref[...] + _mm_x3(xa, wa_ref[...]))
    g = _mm_x3(jax.nn.sigmoid(xg), wg_ref[...])
    seg = seg_ref[...]
    kk = k * kk_ref[...]
    kap = kk / jnp.maximum(jnp.sqrt(_seg_sum(kk * kk, seg)), 1e-12)
    k2 = k * (1.0 + (a - 1.0) * ka_ref[...])
    b = kap * a
    bonus = _seg_sum(r * k2 * rk_ref[...], seg) * v
    ys = []
    for t in range(RWKV_WIDTH // LANES):
        sl = slice(t * LANES, (t + 1) * LANES)
        y_t, s_new = _chunk_pair(r[:, sl], lw[:, sl], k2[:, sl], v[:, sl], kap[:, sl], b[:, sl],
                                 state_scr[t], chunk)
        state_scr[t] = s_new
        ys.append(y_t)
    y = jnp.concatenate(ys, axis=1)
    inv_n = 1.0 / HEAD_DIM
    mean = _seg_sum(y, seg) * inv_n
    yc = y - mean
    var = _seg_sum(yc * yc, seg) * inv_n
    yn = yc * lax.rsqrt(var + LNX_EPS) * lw_ref[...] + lb_ref[...]
    o_ref[0] = ((yn + bonus) * g).astype(o_ref.dtype)

    @pl.when(ci == pl.num_programs(1) - 1)
    def _():
        s_out_ref[0] = state_scr[...]
        sh_out_ref[0] = prev_scr[...]


def _rwkv_mix(p3, s0_bd, shift3, prm, chunk):
    batch, t_len, _ = p3.shape
    n_pairs = RWKV_WIDTH // LANES
    const2 = lambda shape: pl.BlockSpec(shape, lambda b, c: (0, 0))
    vec = const2((1, RWKV_WIDTH))
    return pl.pallas_call(
        _rwkv_kernel,
        grid=(batch, t_len // chunk),
        in_specs=[
            pl.BlockSpec((1, chunk, RWKV_PROJ_PAD), lambda b, c: (b, c, 0)),
            pl.BlockSpec((1, n_pairs, LANES, LANES), lambda b, c: (b, 0, 0, 0)),
            pl.BlockSpec((1, 1, RWKV_PROJ_PAD), lambda b, c: (b, 0, 0)),
            const2((1, RWKV_PROJ_PAD)),
            vec,
            const2((LANES, RWKV_WIDTH)),
            vec,
            const2((LANES, RWKV_WIDTH)),
            const2((XG_PAD, RWKV_WIDTH)),
            vec, vec, vec, vec, vec,
            const2((LANES, LANES)),
        ],
        out_specs=[
            pl.BlockSpec((1, chunk, RWKV_WIDTH), lambda b, c: (b, c, 0)),
            pl.BlockSpec((1, n_pairs, LANES, LANES), lambda b, c: (b, 0, 0, 0)),
            pl.BlockSpec((1, 1, RWKV_PROJ_PAD), lambda b, c: (b, 0, 0)),
        ],
        out_shape=[
            jax.ShapeDtypeStruct((batch, t_len, RWKV_WIDTH), BF16),
            jax.ShapeDtypeStruct((batch, n_pairs, LANES, LANES), F32),
            jax.ShapeDtypeStruct((batch, 1, RWKV_PROJ_PAD), F32),
        ],
        scratch_shapes=[pltpu.VMEM((n_pairs, LANES, LANES), F32), pltpu.VMEM((1, RWKV_PROJ_PAD), F32)],
        compiler_params=_cparams(("parallel", "arbitrary")),
        name="rwkv_mix_c%d" % chunk,
    )(p3, s0_bd, shift3, prm["mu"], prm["w0"], prm["wd"], prm["a0"], prm["wa"], prm["wg"], prm["k_k"],
      prm["k_a"], prm["r_k"], prm["lnx_w"], prm["lnx_b"], prm["seg"])


def _out_router_kernel(x_ref, oa_ref, or_ref, wa_ref, wr_ref, g_ref, rw_ref, rb_ref, h_ref, hn_ref, idx_ref,
                       gate_ref):
    h = x_ref[...] + _mm(oa_ref[...], wa_ref[...]) + _mm(or_ref[...], wr_ref[...])
    h_ref[...] = h
    hn = _rms(h, g_ref[...])
    hn_ref[...] = hn
    hh, hl = _split2(hn)
    logits = _mm(hh, rw_ref[0]) + (_mm(hh, rw_ref[1]) + _mm(hl, rw_ref[0])) + rb_ref[...]
    lane = lax.broadcasted_iota(jnp.int32, logits.shape, 1)
    vals = []
    idxs = []
    cur = logits
    for _ in range(TOP_K):
        m = jnp.max(cur, axis=-1, keepdims=True)
        sel = jnp.min(jnp.where(cur == m, lane, LANES), axis=-1, keepdims=True)
        vals.append(m)
        idxs.append(sel)
        cur = jnp.where(lane == sel, -jnp.inf, cur)
    es = [jnp.exp(vj - vals[0]) for vj in vals]
    tot = es[0] + es[1] + es[2] + es[3]
    idx_out = jnp.zeros(logits.shape, jnp.int32)
    gate_out = jnp.zeros(logits.shape, F32)
    for j in range(TOP_K):
        idx_out = jnp.where(lane == j, idxs[j], idx_out)
        gate_out = jnp.where(lane == j, es[j] / tot, gate_out)
    idx_ref[...] = idx_out
    gate_ref[...] = gate_out


def _out_router(x, o_attn, o_rwkv, w_oa, w_or, g, rw_split, rb_pad, tm):
    t_tok = x.shape[0]
    row = lambda w: pl.BlockSpec((tm, w), lambda i: (i, 0))
    full = lambda shape: pl.BlockSpec(shape, lambda i: tuple(0 for _ in shape))
    return pl.pallas_call(
        _out_router_kernel,
        grid=(t_tok // tm,),
        in_specs=[row(D_MODEL), row(ATTN_WIDTH), row(RWKV_WIDTH), full((ATTN_WIDTH, D_MODEL)),
                  full((RWKV_WIDTH, D_MODEL)), full((1, D_MODEL)), full((2, D_MODEL, LANES)), full((1, LANES))],
        out_specs=[row(D_MODEL), row(D_MODEL), row(LANES), row(LANES)],
        out_shape=[
            jax.ShapeDtypeStruct((t_tok, D_MODEL), F32),
            jax.ShapeDtypeStruct((t_tok, D_MODEL), F32),
            jax.ShapeDtypeStruct((t_tok, LANES), jnp.int32),
            jax.ShapeDtypeStruct((t_tok, LANES), F32),
        ],
        compiler_params=_cparams(("parallel",)),
        name="out_router",
    )(x, o_attn, o_rwkv, w_oa, w_or, g, rw_split, rb_pad)


GATHER_UNROLL = 8
MOE_SUB = 256


def _gather_kernel(nv_ref, tok_ref, hn_ref, o_ref, buf, sem):
    rows = buf.shape[0]

    @pl.when(nv_ref[pl.program_id(0)] > 0)
    def _():
        def issue(r8, carry):
            for u in range(GATHER_UNROLL):
                r = r8 * GATHER_UNROLL + u
                pltpu.make_async_copy(hn_ref.at[pl.ds(tok_ref[r], 1)], buf.at[pl.ds(r, 1)], sem).start()
            return carry

        lax.fori_loop(0, rows // GATHER_UNROLL, issue, 0)
        pltpu.make_async_copy(hn_ref.at[pl.ds(0, rows)], buf, sem).wait()
        o_ref[...] = buf[...].astype(o_ref.dtype)

    @pl.when(nv_ref[pl.program_id(0)] == 0)
    def _():
        o_ref[...] = jnp.zeros_like(o_ref)


def _gather_rows(sub_valid, row_tok, hn, rows_per_step):
    n_rows = row_tok.shape[0]
    grid_spec = pltpu.PrefetchScalarGridSpec(
        num_scalar_prefetch=1,
        grid=(n_rows // rows_per_step,),
        in_specs=[
            pl.BlockSpec((rows_per_step,), lambda i, nv: (i,), memory_space=pltpu.SMEM),
            pl.BlockSpec(memory_space=pl.ANY),
        ],
        out_specs=pl.BlockSpec((rows_per_step, D_MODEL), lambda i, nv: (i, 0)),
        scratch_shapes=[pltpu.VMEM((rows_per_step, D_MODEL), F32), pltpu.SemaphoreType.DMA(())],
    )
    return pl.pallas_call(
        _gather_kernel,
        grid_spec=grid_spec,
        out_shape=jax.ShapeDtypeStruct((n_rows, D_MODEL), BF16),
        compiler_params=_cparams(("arbitrary",)),
        name="moe_gather",
    )(sub_valid, row_tok, hn)


def _expert_kernel(be_ref, nv_ref, nr_ref, x_ref, wu_ref, bu_ref, wd_ref, bd_ref, sel_ref, o_ref):
    i = pl.program_id(0)
    f = pl.program_id(1)
    n_sub = x_ref.shape[0] // MOE_SUB
    used_sub = (nv_ref[i] + (MOE_SUB - 1)) // MOE_SUB

    def body(m_rows):
        rows = slice(0, m_rows)
        z = _mm(x_ref[rows, :], wu_ref[0].astype(BF16)) + bu_ref[0]
        zn = pltpu.roll(z, z.shape[1] - 1, 1)
        glu = jnp.minimum(z, SWIGLU_LIMIT)
        lin = jnp.clip(zn, -SWIGLU_LIMIT, SWIGLU_LIMIT)
        act = (glu * jax.nn.sigmoid(SWIGLU_ALPHA * glu) * (lin + 1.0)).astype(BF16)
        actc = _mm(act, sel_ref[...]).astype(BF16)
        contrib = _mm(actc, wd_ref[0].astype(BF16))

        @pl.when(f == 0)
        def _():
            o_ref[rows, :] = contrib + bd_ref[0]
            if m_rows < o_ref.shape[0]:
                o_ref[m_rows:, :] = jnp.zeros((o_ref.shape[0] - m_rows, o_ref.shape[1]), o_ref.dtype)

        @pl.when(f > 0)
        def _():
            o_ref[rows, :] += contrib

    for k in range(1, n_sub + 1):
        pl.when(used_sub == k)(functools.partial(body, k * MOE_SUB))

    @pl.when((used_sub == 0) & (f == 0))
    def _():
        o_ref[...] = jnp.zeros_like(o_ref)


def _experts(blk_expert, blk_valid, n_real, x_sorted, w_up, b_up3, w_down, b_down3, sel, bm, tf):
    n_rows = x_sorted.shape[0]
    n_f = D_FF // tf

    def real(i, nr):
        return jnp.minimum(i, nr[0] - 1)

    def f_eff(i, f, nr):
        return jnp.where(i < nr[0], f, n_f - 1)

    grid_spec = pltpu.PrefetchScalarGridSpec(
        num_scalar_prefetch=3,
        grid=(n_rows // bm, n_f),
        in_specs=[
            pl.BlockSpec((bm, D_MODEL), lambda i, f, be, nv, nr: (real(i, nr), 0)),
            pl.BlockSpec((1, D_MODEL, 2 * tf), lambda i, f, be, nv, nr: (be[i], 0, f_eff(i, f, nr))),
            pl.BlockSpec((1, 1, 2 * tf), lambda i, f, be, nv, nr: (be[i], 0, f_eff(i, f, nr))),
            pl.BlockSpec((1, tf, D_MODEL), lambda i, f, be, nv, nr: (be[i], f_eff(i, f, nr), 0)),
            pl.BlockSpec((1, 1, D_MODEL), lambda i, f, be, nv, nr: (be[i], 0, 0)),
            pl.BlockSpec((2 * tf, tf), lambda i, f, be, nv, nr: (0, 0)),
        ],
        out_specs=pl.BlockSpec((bm, D_MODEL), lambda i, f, be, nv, nr: (i, 0)),
    )
    return pl.pallas_call(
        _expert_kernel,
        grid_spec=grid_spec,
        out_shape=jax.ShapeDtypeStruct((n_rows, D_MODEL), F32),
        compiler_params=_cparams(("arbitrary", "arbitrary")),
        name="moe_experts",
    )(blk_expert, blk_valid, n_real, x_sorted, w_up, b_up3, w_down, b_down3, sel)


def _combine_kernel(dest_ref, rows_ref, h_ref, gate_ref, g_ref, o_ref, buf, sem):
    tm = h_ref.shape[0]

    def issue(t2, carry):
        for u in range(2):
            t = t2 * 2 + u
            for j in range(TOP_K):
                pltpu.make_async_copy(rows_ref.at[pl.ds(dest_ref[t * TOP_K + j], 1)],
                                      buf.at[pl.ds(j * tm + t, 1)], sem).start()
        return carry

    lax.fori_loop(0, tm // 2, issue, 0)
    pltpu.make_async_copy(rows_ref.at[pl.ds(0, TOP_K * tm)], buf, sem).wait()
    gates = gate_ref[...]
    y = h_ref[...]
    for j in range(TOP_K):
        y = y + buf[j * tm:(j + 1) * tm, :] * gates[:, j:j + 1]
    o_ref[...] = _rms(y, g_ref[...])


def _combine(dest, out_rows, h, gates, g, tm):
    t_tok = h.shape[0]
    return pl.pallas_call(
        _combine_kernel,
        grid=(t_tok // tm,),
        in_specs=[
            pl.BlockSpec((tm * TOP_K,), lambda i: (i,), memory_space=pltpu.SMEM),
            pl.BlockSpec(memory_space=pl.ANY),
            pl.BlockSpec((tm, D_MODEL), lambda i: (i, 0)),
            pl.BlockSpec((tm, LANES), lambda i: (i, 0)),
            pl.BlockSpec((1, D_MODEL), lambda i: (0, 0)),
        ],
        out_specs=pl.BlockSpec((tm, D_MODEL), lambda i: (i, 0)),
        out_shape=jax.ShapeDtypeStruct((t_tok, D_MODEL), F32),
        scratch_shapes=[pltpu.VMEM((TOP_K * tm, D_MODEL), F32), pltpu.SemaphoreType.DMA(())],
        compiler_params=_cparams(("arbitrary",)),
        name="moe_combine",
    )(dest, out_rows, h, gates, g)


def _q_head_order():
    order = []
    for pair in range(2):
        for j in range(4):
            order += [8 * pair + j, 8 * pair + 4 + j]
    return order


def _q_col_perm():
    cols = []
    for h in _q_head_order():
        cols += list(range(h * HEAD_DIM, (h + 1) * HEAD_DIM))
    return jnp.asarray(cols, jnp.int32)


def _pad_rwkv_cols(m):
    def padw(a, w):
        return jnp.pad(a, [(0, 0)] * (a.ndim - 1) + [(0, w - a.shape[-1])])
    return jnp.concatenate([
        m[..., :XW_OFF],
        padw(m[..., XW_OFF:XW_OFF + DECAY_LORA], LANES),
        padw(m[..., XW_OFF + DECAY_LORA:XW_OFF + DECAY_LORA + ICLR_LORA], LANES),
        padw(m[..., XW_OFF + DECAY_LORA + ICLR_LORA:], XG_PAD),
    ], axis=-1)


def _unpad_rwkv_cols(m):
    return jnp.concatenate([
        m[..., :XW_OFF + DECAY_LORA],
        m[..., XA_OFF:XA_OFF + ICLR_LORA],
        m[..., XG_OFF:XG_OFF + GATE_LORA],
    ], axis=-1)


def _pad_rows(m, rows):
    return jnp.pad(m, ((0, rows - m.shape[0]), (0, 0)))


def _rope_tables(pos):
    half = HEAD_DIM // 2
    inv_freq = ROPE_THETA ** (-jnp.arange(half, dtype=F32) / half)
    ang = pos.astype(F32)[:, None] * inv_freq[None, :]
    cos = jnp.cos(ang)
    sin = jnp.sin(ang)
    cos_t = jnp.tile(jnp.concatenate([cos, cos], axis=-1), (1, LANES // HEAD_DIM))
    sin_t = jnp.tile(jnp.concatenate([-sin, sin], axis=-1), (1, LANES // HEAD_DIM))
    return cos_t, sin_t


def _state_to_blockdiag(s):
    b = s.shape[0]
    s = s.reshape(b, RWKV_WIDTH // LANES, 2, HEAD_DIM, HEAD_DIM)
    z = jnp.zeros_like(s[:, :, 0])
    top = jnp.concatenate([s[:, :, 0], z], axis=-1)
    bot = jnp.concatenate([z, s[:, :, 1]], axis=-1)
    return jnp.concatenate([top, bot], axis=-2)


def _blockdiag_to_state(s):
    b = s.shape[0]
    lo = s[:, :, :HEAD_DIM, :HEAD_DIM]
    hi = s[:, :, HEAD_DIM:, HEAD_DIM:]
    return jnp.stack([lo, hi], axis=2).reshape(b, RWKV_WIDTH // HEAD_DIM, HEAD_DIM, HEAD_DIM)


def _route(top_idx, bm):
    n_tok = top_idx.shape[0]
    e_flat = top_idx.reshape(-1)
    onehot = (e_flat[:, None] == jnp.arange(N_EXPERTS, dtype=jnp.int32)[None, :]).astype(jnp.int32)
    csum = jnp.cumsum(onehot, axis=0)
    rank = jnp.sum(csum * onehot, axis=1) - 1
    counts = csum[-1]
    padded = ((counts + bm - 1) // bm) * bm
    p_end = jnp.cumsum(padded)
    p_start = p_end - padded
    dest = (p_start[e_flat] + rank).astype(jnp.int32)
    n_assign = n_tok * TOP_K
    n_blocks = -(-n_assign // bm) + N_EXPERTS
    n_rows = n_blocks * bm
    tok_flat = jnp.repeat(jnp.arange(n_tok, dtype=jnp.int32), TOP_K)
    row_tok = jnp.zeros((n_rows,), jnp.int32).at[dest].set(tok_flat)
    blk_start = jnp.arange(n_blocks, dtype=jnp.int32) * bm
    blk_expert = jnp.minimum(jnp.searchsorted(p_end, blk_start, side="right"), N_EXPERTS - 1).astype(jnp.int32)
    blk_valid = jnp.clip(counts[blk_expert] - (blk_start - p_start[blk_expert]), 0, bm).astype(jnp.int32)
    sub_start = jnp.arange(n_rows // MOE_SUB, dtype=jnp.int32) * MOE_SUB
    sub_expert = blk_expert[sub_start // bm]
    sub_valid = jnp.clip(counts[sub_expert] - (sub_start - p_start[sub_expert]), 0, MOE_SUB).astype(jnp.int32)
    n_real = (p_end[-1:] // bm).astype(jnp.int32)
    return dest, row_tok, blk_expert, blk_valid, sub_valid, n_real


def _pick(n, prefs):
    for p in prefs:
        if n % p == 0:
            return p
    return n


def kernel(x_prompt, x_sample, cache_k, cache_v, state_wkv, state_shift, attn_norm_g, w_in, attn_sinks, mu_shift, decay_w0, decay_lora_up, iclr_a0, iclr_lora_up, gate_lora_up, k_k, k_a, r_k, lnx_w, lnx_b, w_out, ffn_norm_g, router_w, router_b, w_up, b_up, w_down, b_down, final_norm_g):
    depth = w_in.shape[0]
    assert depth == 1
    batch, seq, _ = x_prompt.shape
    dec_b, dec_t, _ = x_sample.shape
    n_p = batch * seq
    n_s = dec_b * dec_t
    l = 0

    qperm = _q_col_perm()
    w_in_l = w_in[l]
    w_attn = jnp.concatenate([w_in_l[:, :ATTN_WIDTH][:, qperm], w_in_l[:, ATTN_WIDTH:ATTN_WIDTH + 2 * KV_WIDTH]],
                             axis=1).astype(BF16)
    w_rwkv = _pad_rwkv_cols(w_in_l[:, ATTN_WIDTH + 2 * KV_WIDTH:]).astype(BF16)
    w_oa = w_out[l][:ATTN_WIDTH][qperm].astype(BF16)
    w_or = w_out[l][ATTN_WIDTH:].astype(BF16)
    g_attn = attn_norm_g[l][None, :]
    g_ffn = ffn_norm_g[l][None, :]
    order = jnp.asarray(_q_head_order(), jnp.int32)
    sinks_perm = attn_sinks[l][order].astype(F32)
    sinks_true = attn_sinks[l].astype(F32)
    seg = (jnp.arange(LANES)[:, None] // HEAD_DIM == jnp.arange(LANES)[None, :] // HEAD_DIM).astype(BF16)
    prm = dict(
        mu=_pad_rwkv_cols(mu_shift[l][None, :]),
        w0=decay_w0[l][None, :], wd=_pad_rows(decay_lora_up[l], LANES),
        a0=iclr_a0[l][None, :], wa=_pad_rows(iclr_lora_up[l], LANES),
        wg=_pad_rows(gate_lora_up[l], XG_PAD),
        k_k=k_k[l][None, :], k_a=k_a[l][None, :], r_k=r_k[l].reshape(1, RWKV_WIDTH),
        lnx_w=lnx_w[l][None, :], lnx_b=lnx_b[l][None, :], seg=seg)
    rw_pad = jnp.pad(router_w[l], ((0, 0), (0, LANES - N_EXPERTS)))
    rw_hi = rw_pad.astype(BF16)
    rw_lo = (rw_pad - rw_hi.astype(F32)).astype(BF16)
    rw_split = jnp.stack([rw_hi, rw_lo])
    rb_pad = jnp.concatenate([router_b[l], jnp.full((LANES - N_EXPERTS,), NEG_BIG, F32)])[None, :]
    b_up3 = b_up[l][:, None, :]
    b_dn = b_down[l][:, None, :]
    tf = 256
    sel = (jnp.arange(2 * tf)[:, None] == 2 * jnp.arange(tf)[None, :]).astype(BF16)

    tm_p = _pick(seq, (512, 256, 128))
    cos_p, sin_p = _rope_tables(jnp.arange(seq, dtype=jnp.int32))
    q_p, k_p, v_p = _attn_proj(x_prompt.reshape(n_p, D_MODEL), g_attn, w_attn, cos_p, sin_p, tm_p)
    tm_s = _pick(n_s, (512, 256, 128, 64, 32, 16, 8))
    pos_s = PAST_LEN + (jnp.arange(n_s, dtype=jnp.int32) % dec_t)
    cos_s, sin_s = _rope_tables(pos_s)
    q_s, k_s, v_s = _attn_proj(x_sample.reshape(n_s, D_MODEL), g_attn, w_attn, cos_s, sin_s, tm_s)
    tn = 512
    pr_p = _rwkv_proj(x_prompt.reshape(n_p, D_MODEL), g_attn, w_rwkv, _pick(n_p, (1024, 512, 256, 128)), tn)
    pr_s = _rwkv_proj(x_sample.reshape(n_s, D_MODEL), g_attn, w_rwkv, tm_s, tn)

    oa_p = _attn_prompt(q_p, k_p, v_p, sinks_true, batch, seq)
    wb = cache_k.shape[2]
    sink_rows = jnp.stack([
        jnp.repeat(jnp.stack([sinks_true[8 * pair + 4 * e + j] for e in range(2) for j in range(4)]), dec_t)
        for pair in range(2)])[:, :, None]
    oa_s, nk_s, nv_s = _attn_sample(
        q_s.reshape(dec_b, dec_t, ATTN_WIDTH), k_s.reshape(dec_b, dec_t, KV_WIDTH),
        v_s.reshape(dec_b, dec_t, KV_WIDTH), cache_k[l].reshape(dec_b, wb, KV_WIDTH),
        cache_v[l].reshape(dec_b, wb, KV_WIDTH), sink_rows, _pick(dec_b, (8, 4, 2, 1)))
    del sinks_perm

    zero_state = jnp.zeros((batch, RWKV_WIDTH // LANES, LANES, LANES), F32)
    zero_shift = jnp.zeros((batch, 1, RWKV_PROJ_PAD), F32)
    or_p, st_p, sh_p = _rwkv_mix(pr_p.reshape(batch, seq, RWKV_PROJ_PAD), zero_state, zero_shift, prm,
                                 _pick(seq, (64, 32, 16, 8)))
    or_s, st_s, sh_s = _rwkv_mix(pr_s.reshape(dec_b, dec_t, RWKV_PROJ_PAD), _state_to_blockdiag(state_wkv[l]),
                                 _pad_rwkv_cols(state_shift[l])[:, None, :], prm, dec_t)

    x_all = jnp.concatenate([x_prompt.reshape(n_p, D_MODEL), x_sample.reshape(n_s, D_MODEL)], axis=0)
    oa_all = jnp.concatenate([oa_p, oa_s.reshape(n_s, ATTN_WIDTH)], axis=0)
    or_all = jnp.concatenate([or_p.reshape(n_p, RWKV_WIDTH), or_s.reshape(n_s, RWKV_WIDTH)], axis=0)
    n_all = n_p + n_s
    tm_o = _pick(n_all, (512, 256, 128, 64, 32, 16, 8))
    h, hn, idx_pad, gate_pad = _out_router(x_all, oa_all, or_all, w_oa, w_or, g_ffn, rw_split, rb_pad, tm_o)

    bm = 1024 if n_all * TOP_K >= 1024 * N_EXPERTS else MOE_SUB
    dest, row_tok, blk_expert, blk_valid, sub_valid, n_real = _route(idx_pad[:, :TOP_K], bm)
    x_sorted = _gather_rows(sub_valid, row_tok, hn, MOE_SUB)
    out_rows = _experts(blk_expert, blk_valid, n_real, x_sorted, w_up[l], b_up3, w_down[l], b_dn, sel, bm, tf)
    y_all = _combine(dest, out_rows, h, gate_pad, final_norm_g[None, :], _pick(n_all, (128, 64, 32, 16, 8)))

    y_prompt = y_all[:n_p].reshape(batch, seq, D_MODEL)
    y_sample = y_all[n_p:].reshape(dec_b, dec_t, D_MODEL)
    kp = k_p.reshape(batch, seq, 4, HEAD_DIM)[:, -WINDOW:][None]
    vp = v_p.reshape(batch, seq, 4, HEAD_DIM)[:, -WINDOW:][None]
    wp = _blockdiag_to_state(st_p)[None]
    sp = _unpad_rwkv_cols(sh_p[:, 0, :])[None]
    ks = nk_s.reshape(dec_b, wb, 4, HEAD_DIM)[None]
    vs = nv_s.reshape(dec_b, wb, 4, HEAD_DIM)[None]
    ws = _blockdiag_to_state(st_s)[None]
    ss = _unpad_rwkv_cols(sh_s[:, 0, :])[None]
    return (y_prompt, y_sample, kp, vp, wp, sp, ks, vs, ws, ss)
```

```python
import functools
import math

import jax
import jax.numpy as jnp
from jax import lax
from jax.experimental import pallas as pl
from jax.experimental.pallas import tpu as pltpu

F32 = jnp.float32
BF16 = jnp.bfloat16

D_MODEL = 2048
HEAD_DIM = 64
LANES = 128
ATTN_WIDTH = 1024
KV_WIDTH = 256
ATTN_HEADS = 16
RWKV_WIDTH = 1024
DECAY_LORA = 64
ICLR_LORA = 64
GATE_LORA = 160
RWKV_PROJ = 3 * RWKV_WIDTH + DECAY_LORA + ICLR_LORA + GATE_LORA
XW_OFF = 3 * RWKV_WIDTH
XA_OFF = XW_OFF + LANES
XG_OFF = XA_OFF + LANES
XG_PAD = 2 * LANES
RWKV_PROJ_PAD = XG_OFF + XG_PAD
WINDOW = 128
ROPE_THETA = 10000.0
PAST_LEN = 8192
N_EXPERTS = 32
TOP_K = 4
D_FF = 2048
SWIGLU_ALPHA = 1.702
SWIGLU_LIMIT = 7.0
NORM_EPS = 1e-5
LNX_EPS = 64e-5
NEG_BIG = -1e30
VMEM_LIMIT = 56 * 1024 * 1024


def _cparams(sem):
    return pltpu.CompilerParams(dimension_semantics=sem, vmem_limit_bytes=VMEM_LIMIT)


_NN = (((1,), (0,)), ((), ()))
_NT = (((1,), (1,)), ((), ()))
_TN = (((0,), (0,)), ((), ()))


def _mm(a, b, dims=_NN):
    return lax.dot_general(a, b, dims, preferred_element_type=F32)


def _split2(x):
    hi = x.astype(BF16)
    lo = (x - hi.astype(F32)).astype(BF16)
    return hi, lo


def _split3(x):
    hi = x.astype(BF16)
    r1 = x - hi.astype(F32)
    mid = r1.astype(BF16)
    lo = (r1 - mid.astype(F32)).astype(BF16)
    return hi, mid, lo


def _mm_exact_b(a, b_bf16, dims=_NN):
    h, m, l = _split3(a)
    return _mm(h, b_bf16, dims) + (_mm(m, b_bf16, dims) + _mm(l, b_bf16, dims))


def _mm_exact_a(a_bf16, b, dims=_NN):
    h, m, l = _split3(b)
    return _mm(a_bf16, h, dims) + (_mm(a_bf16, m, dims) + _mm(a_bf16, l, dims))


def _rms(x, g):
    ms = jnp.mean(x * x, axis=-1, keepdims=True)
    return (x * lax.rsqrt(ms + NORM_EPS)) * g


def _attn_proj_kernel(x_ref, g_ref, w_ref, cos_ref, sin_ref, q_ref, k_ref, v_ref):
    xn = _rms(x_ref[...], g_ref[...]).astype(BF16)
    a = _mm(xn, w_ref[...])
    cos = cos_ref[...]
    sin = sin_ref[...]
    lane = lax.broadcasted_iota(jnp.int32, cos.shape, 1)
    first_half = (lane % HEAD_DIM) < (HEAD_DIM // 2)
    n_rot = (ATTN_WIDTH + KV_WIDTH) // LANES
    for j in range(n_rot):
        t = a[:, j * LANES:(j + 1) * LANES]
        swapped = jnp.where(first_half, pltpu.roll(t, LANES - HEAD_DIM // 2, 1), pltpu.roll(t, HEAD_DIM // 2, 1))
        rot = t * cos + swapped * sin
        if j < ATTN_WIDTH // LANES:
            q_ref[:, j * LANES:(j + 1) * LANES] = (rot * (HEAD_DIM ** -0.5)).astype(BF16)
        else:
            jj = j - ATTN_WIDTH // LANES
            k_ref[:, jj * LANES:(jj + 1) * LANES] = rot
    v_ref[...] = a[:, ATTN_WIDTH + KV_WIDTH:]


def _attn_proj(x, g, w_attn, cos_t, sin_t, tm):
    t_tok = x.shape[0]
    n_pos_blocks = cos_t.shape[0] // tm
    return pl.pallas_call(
        _attn_proj_kernel,
        grid=(t_tok // tm,),
        in_specs=[
            pl.BlockSpec((tm, D_MODEL), lambda i: (i, 0)),
            pl.BlockSpec((1, D_MODEL), lambda i: (0, 0)),
            pl.BlockSpec((D_MODEL, ATTN_WIDTH + 2 * KV_WIDTH), lambda i: (0, 0)),
            pl.BlockSpec((tm, LANES), lambda i: (i % n_pos_blocks, 0)),
            pl.BlockSpec((tm, LANES), lambda i: (i % n_pos_blocks, 0)),
        ],
        out_specs=[
            pl.BlockSpec((tm, ATTN_WIDTH), lambda i: (i, 0)),
            pl.BlockSpec((tm, KV_WIDTH), lambda i: (i, 0)),
            pl.BlockSpec((tm, KV_WIDTH), lambda i: (i, 0)),
        ],
        out_shape=[
            jax.ShapeDtypeStruct((t_tok, ATTN_WIDTH), BF16),
            jax.ShapeDtypeStruct((t_tok, KV_WIDTH), F32),
            jax.ShapeDtypeStruct((t_tok, KV_WIDTH), F32),
        ],
        compiler_params=_cparams(("parallel",)),
        name="attn_proj",
    )(x, g, w_attn, cos_t, sin_t)


def _rwkv_proj_kernel(x_ref, g_ref, w_ref, p_ref, xn_ref):
    @pl.when(pl.program_id(1) == 0)
    def _():
        xn_ref[...] = _rms(x_ref[...], g_ref[...]).astype(BF16)

    p_ref[...] = _mm(xn_ref[...], w_ref[...])


def _rwkv_proj(x, g, w_rwkv, tm, tn):
    t_tok = x.shape[0]
    return pl.pallas_call(
        _rwkv_proj_kernel,
        grid=(t_tok // tm, RWKV_PROJ_PAD // tn),
        in_specs=[
            pl.BlockSpec((tm, D_MODEL), lambda i, n: (i, 0)),
            pl.BlockSpec((1, D_MODEL), lambda i, n: (0, 0)),
            pl.BlockSpec((D_MODEL, tn), lambda i, n: (0, n)),
        ],
        out_specs=pl.BlockSpec((tm, tn), lambda i, n: (i, n)),
        out_shape=jax.ShapeDtypeStruct((t_tok, RWKV_PROJ_PAD), F32),
        scratch_shapes=[pltpu.VMEM((tm, D_MODEL), BF16)],
        compiler_params=_cparams(("parallel", "arbitrary")),
        name="rwkv_proj",
    )(x, g, w_rwkv)


def _softmax_pv(s, mask, sink, vt):
    s = jnp.where(mask, s, NEG_BIG)
    m = jnp.maximum(jnp.max(s, axis=-1, keepdims=True), sink)
    p = jnp.exp(s - m)
    denom = jnp.sum(p, axis=-1, keepdims=True) + jnp.exp(sink - m)
    return _mm(p.astype(BF16), vt) * (1.0 / denom)


def _attn_prompt_kernel(q_ref, kp_ref, kc_ref, vp_ref, vc_ref, sink_ref, o_ref):
    i = pl.program_id(1)
    blk = q_ref.shape[0]
    kk = jnp.concatenate([kp_ref[...], kc_ref[...]], axis=0).astype(BF16)
    vv = jnp.concatenate([vp_ref[...], vc_ref[...]], axis=0).astype(BF16)
    a = lax.broadcasted_iota(jnp.int32, (blk, 2 * blk), 0)
    c = lax.broadcasted_iota(jnp.int32, (blk, 2 * blk), 1)
    mask = (c > a) & (c <= a + blk) & ((c >= blk) | (i > 0))
    lane_lo = lax.broadcasted_iota(jnp.int32, (blk, LANES), 1) < HEAD_DIM
    for pair in range(KV_WIDTH // LANES):
        kt = kk[:, pair * LANES:(pair + 1) * LANES]
        vt = vv[:, pair * LANES:(pair + 1) * LANES]
        for j in range(4):
            tile = pair * 4 + j
            qt = q_ref[:, tile * LANES:(tile + 1) * LANES]
            halves = []
            for e in range(2):
                qm = jnp.where(lane_lo if e == 0 else jnp.logical_not(lane_lo), qt, jnp.zeros_like(qt))
                s = _mm(qm, kt, _NT)
                halves.append(_softmax_pv(s, mask, sink_ref[8 * pair + 4 * e + j], vt))
            o_ref[:, tile * LANES:(tile + 1) * LANES] = jnp.where(lane_lo, halves[0], halves[1]).astype(BF16)


def _attn_prompt(q, k, v, sinks, batch, seq):
    blk = WINDOW
    nb = seq // blk
    cur = lambda b, i: (b * nb + i, 0)
    prev = lambda b, i: (b * nb + jnp.maximum(i - 1, 0), 0)
    return pl.pallas_call(
        _attn_prompt_kernel,
        grid=(batch, nb),
        in_specs=[
            pl.BlockSpec((blk, ATTN_WIDTH), cur),
            pl.BlockSpec((blk, KV_WIDTH), prev),
            pl.BlockSpec((blk, KV_WIDTH), cur),
            pl.BlockSpec((blk, KV_WIDTH), prev),
            pl.BlockSpec((blk, KV_WIDTH), cur),
            pl.BlockSpec(memory_space=pltpu.SMEM),
        ],
        out_specs=pl.BlockSpec((blk, ATTN_WIDTH), cur),
        out_shape=jax.ShapeDtypeStruct((batch * seq, ATTN_WIDTH), BF16),
        compiler_params=_cparams(("parallel", "arbitrary")),
        name="attn_prompt",
    )(q, k, k, v, v, sinks)


def _attn_sample_kernel(q_ref, kn_ref, vn_ref, ck_ref, cv_ref, sink_ref, o_ref, nk_ref, nv_ref):
    bb, t_len, _ = q_ref.shape
    wb = ck_ref.shape[1]
    n_keys = wb + t_len
    rows = 8 * t_len
    a = lax.broadcasted_iota(jnp.int32, (rows, n_keys), 0) % t_len
    c = lax.broadcasted_iota(jnp.int32, (rows, n_keys), 1)
    mask = ((c < wb) & (c > a + (wb - WINDOW))) | ((c >= wb) & (c - wb <= a))
    lane_lo = lax.broadcasted_iota(jnp.int32, (t_len, LANES), 1) < HEAD_DIM

    def body(b, carry):
        ck = ck_ref[b]
        cv = cv_ref[b]
        kn = kn_ref[b]
        vn = vn_ref[b]
        nk_ref[b, 0:wb - t_len, :] = ck[t_len:, :]
        nk_ref[b, wb - t_len:wb, :] = kn
        nv_ref[b, 0:wb - t_len, :] = cv[t_len:, :]
        nv_ref[b, wb - t_len:wb, :] = vn
        k_all = jnp.concatenate([ck, kn], axis=0).astype(BF16)
        v_all = jnp.concatenate([cv, vn], axis=0).astype(BF16)
        qb = q_ref[b]
        for pair in range(KV_WIDTH // LANES):
            kt = k_all[:, pair * LANES:(pair + 1) * LANES]
            vt = v_all[:, pair * LANES:(pair + 1) * LANES]
            stack = []
            for e in range(2):
                for j in range(4):
                    tile = pair * 4 + j
                    qt = qb[:, tile * LANES:(tile + 1) * LANES]
                    stack.append(jnp.where(lane_lo if e == 0 else jnp.logical_not(lane_lo), qt, jnp.zeros_like(qt)))
            qs = jnp.concatenate(stack, axis=0)
            s = _mm(qs, kt, _NT)
            o = _softmax_pv(s, mask, sink_ref[pair], vt)
            for j in range(4):
                tile = pair * 4 + j
                lo = o[j * t_len:(j + 1) * t_len, :]
                hi = o[(4 + j) * t_len:(5 + j) * t_len, :]
                o_ref[b, :, tile * LANES:(tile + 1) * LANES] = jnp.where(lane_lo, lo, hi).astype(BF16)
        return carry

    lax.fori_loop(0, bb, body, 0)


def _attn_sample(q3, k3, v3, ck, cv, sink_rows, bb):
    dec_b, t_len, _ = q3.shape
    wb = ck.shape[1]
    blk3 = lambda w: pl.BlockSpec((bb, t_len, w), lambda i: (i, 0, 0))
    cache = pl.BlockSpec((bb, wb, KV_WIDTH), lambda i: (i, 0, 0))
    return pl.pallas_call(
        _attn_sample_kernel,
        grid=(dec_b // bb,),
        in_specs=[blk3(ATTN_WIDTH), blk3(KV_WIDTH), blk3(KV_WIDTH), cache, cache,
                  pl.BlockSpec((2, 8 * t_len, 1), lambda i: (0, 0, 0))],
        out_specs=[blk3(ATTN_WIDTH), cache, cache],
        out_shape=[
            jax.ShapeDtypeStruct((dec_b, t_len, ATTN_WIDTH), BF16),
            jax.ShapeDtypeStruct(ck.shape, F32),
            jax.ShapeDtypeStruct(cv.shape, F32),
        ],
        compiler_params=_cparams(("parallel",)),
        name="attn_sample",
    )(q3, k3, v3, ck, cv, sink_rows)


GROUP = 4 * HEAD_DIM


def _bf(x):
    return x.astype(BF16)


def _seg_sum4(x, seg_mat):
    outs = []
    for t in range(x.shape[1] // GROUP):
        outs.append(_mm_exact_b(x[:, t * GROUP:(t + 1) * GROUP], seg_mat))
    return jnp.concatenate(outs, axis=1)


def _stack_heads(x, head_masks):
    zero = jnp.zeros_like(x)
    return jnp.concatenate([jnp.where(m, x, zero) for m in head_masks], axis=0)


def _sum_blocks(x, r):
    return (x[0:r] + x[r:2 * r]) + (x[2 * r:3 * r] + x[3 * r:4 * r])


def _place_heads(blocks):
    rows = []
    for h, blk in enumerate(blocks):
        rows.append(jnp.concatenate([blk if j == h else jnp.zeros_like(blk) for j in range(len(blocks))], axis=1))
    return jnp.concatenate(rows, axis=0)


def _rwkv4_kernel(p_ref, s_in_ref, sh_ref, mu_ref, w0_ref, wd_ref, a0_ref, wa_ref, wg_ref, kk_ref, ka_ref,
                  rk_ref, lnw_ref, lnb_ref, seg_ref, o_ref, s_out_ref, sh_out_ref, state_scr, prev_scr,
                  *, nseq, tlen):
    ci = pl.program_id(1)
    rows = nseq * tlen
    n_groups = RWKV_WIDTH // GROUP
    heads = GROUP // HEAD_DIM
    log_t = tlen.bit_length() - 1
    log_r = rows.bit_length() - 1
    log_h = HEAD_DIM.bit_length() - 1

    p = p_ref[...]
    rowi = lax.broadcasted_iota(jnp.int32, (rows, 1), 0)
    rolled = pltpu.roll(p, 1, 0)
    if nseq == 1:
        @pl.when(ci == 0)
        def _():
            prev_scr[...] = sh_ref[0]
            for g in range(n_groups):
                state_scr[g] = _place_heads([s_in_ref[0, heads * g + h] for h in range(heads)])

        p_prev = jnp.where(rowi == 0, prev_scr[...], rolled)
        prev_scr[...] = p[rows - 1:rows, :]
    else:
        p_prev = rolled
        for s in range(nseq):
            p_prev = jnp.where(rowi == s * tlen, sh_ref[s], p_prev)
            sh_out_ref[s] = p[(s + 1) * tlen - 1:(s + 1) * tlen, :]

    xs = p + (p_prev - p) * mu_ref[...]
    r = xs[:, 0:RWKV_WIDTH]
    k = xs[:, RWKV_WIDTH:2 * RWKV_WIDTH]
    v = xs[:, 2 * RWKV_WIDTH:3 * RWKV_WIDTH]
    xw = xs[:, XW_OFF:XW_OFF + LANES]
    xa = xs[:, XA_OFF:XA_OFF + LANES]
    xg = xs[:, XG_OFF:XG_OFF + XG_PAD]
    z = w0_ref[...] + _mm(_bf(jnp.tanh(xw)), wd_ref[...])
    w_log = -jax.nn.softplus(-z) - 0.5
    lw = -jnp.exp(w_log)
    a = jax.nn.sigmoid(a0_ref[...] + _mm(_bf(xa), wa_ref[...]))
    gate = _mm(_bf(jax.nn.sigmoid(xg)), wg_ref[...])
    seg = seg_ref[...]
    kk = k * kk_ref[...]
    kap = kk / jnp.maximum(jnp.sqrt(_seg_sum4(kk * kk, seg)), 1e-12)
    k2 = k * (1.0 + (a - 1.0) * ka_ref[...])
    b = kap * a
    bonus = _seg_sum4(r * k2 * rk_ref[...], seg) * v

    ri = lax.broadcasted_iota(jnp.int32, (rows, rows), 0)
    cj = lax.broadcasted_iota(jnp.int32, (rows, rows), 1)
    tri = jnp.where(((ri >> log_t) == (cj >> log_t)) & (ri >= cj), 1.0, 0.0).astype(BF16)
    cum = _mm_exact_a(tri, lw)
    eg = jnp.exp(cum)
    kap_t = kap * jnp.exp(cum - lw)
    r_t = r * eg
    einv = jnp.exp(-cum)
    b_t = b * einv
    k_t = k2 * einv

    bri = lax.broadcasted_iota(jnp.int32, (heads * rows, heads * rows), 0)
    bcj = lax.broadcasted_iota(jnp.int32, (heads * rows, heads * rows), 1)
    same = ((bri >> log_r) == (bcj >> log_r)) & ((bri >> log_t) == (bcj >> log_t))
    strict_bd = same & (bri > bcj)
    incl_bd = same & (bri >= bcj)
    eye_bd = jnp.where(bri == bcj, 1.0, 0.0).astype(F32)
    zero_bd = jnp.zeros((heads * rows, heads * rows), F32)
    lane = lax.broadcasted_iota(jnp.int32, (rows, GROUP), 1)
    head_masks = [(lane >> log_h) == h for h in range(heads)]
    sr = lax.broadcasted_iota(jnp.int32, (GROUP, GROUP), 0)
    sc = lax.broadcasted_iota(jnp.int32, (GROUP, GROUP), 1)
    state_bd = (sr >> log_h) == (sc >> log_h)
    zero_st = jnp.zeros((GROUP, GROUP), F32)

    ys = []
    for g in range(n_groups):
        sl = slice(g * GROUP, (g + 1) * GROUP)
        kap_g, r_g, b_g, k_g, v_g = kap_t[:, sl], r_t[:, sl], b_t[:, sl], k_t[:, sl], v[:, sl]
        lkr = jnp.concatenate([_stack_heads(_bf(kap_g), head_masks), _stack_heads(_bf(r_g), head_masks)], axis=0)
        gb = _mm(lkr, _stack_heads(_bf(b_g), head_masks), _NT)
        gk = _mm(lkr, _stack_heads(_bf(k_g), head_masks), _NT)
        hr = heads * rows
        n = -jnp.where(strict_bd, gb[:hr], zero_bd)
        a_k = _bf(jnp.where(strict_bd, gk[:hr], zero_bd))
        r_b = _bf(jnp.where(incl_bd, gb[hr:], zero_bd))
        r_k = _bf(jnp.where(incl_bd, gk[hr:], zero_bd))
        t_inv = eye_bd + n
        for _ in range(max(log_t - 1, 0)):
            nb = _bf(n)
            n = _mm(nb, nb)
            t_inv = t_inv + _mm(_bf(t_inv), _bf(n))
        v_stack = _stack_heads(_bf(v_g), head_masks)

        if nseq == 1:
            s0 = state_scr[g]
            sk = _mm(_bf(jnp.concatenate([kap_g, r_g], axis=0)), _bf(s0), _NT)
            k_s, r_s = sk[:rows], sk[rows:]
        else:
            s0_list, ks_list, rs_list = [], [], []
            for s in range(nseq):
                rs = slice(s * tlen, (s + 1) * tlen)
                s0 = _place_heads([s_in_ref[s, heads * g + h] for h in range(heads)])
                sk = _mm(_bf(jnp.concatenate([kap_g[rs], r_g[rs]], axis=0)), _bf(s0), _NT)
                s0_list.append(s0)
                ks_list.append(sk[:tlen])
                rs_list.append(sk[tlen:])
            k_s = jnp.concatenate(ks_list, axis=0)
            r_s = jnp.concatenate(rs_list, axis=0)

        w = _sum_blocks(_mm(a_k, v_stack), rows)
        rhs = -(k_s + w)
        u_stack = _mm(_bf(t_inv), _stack_heads(_bf(rhs), head_masks))
        u = _sum_blocks(u_stack, rows)
        y_stack = _mm(r_b, _bf(u_stack)) + _mm(r_k, v_stack)
        ys.append(r_s + _sum_blocks(y_stack, rows))

        if nseq == 1:
            d = _mm(_bf(jnp.concatenate([u, v_g], axis=0)), _bf(jnp.concatenate([b_g, k_g], axis=0)), _TN)
            state_scr[g] = (s0 + jnp.where(state_bd, d, zero_st)) * eg[rows - 1:rows, sl]
        else:
            for s in range(nseq):
                rs = slice(s * tlen, (s + 1) * tlen)
                d = _mm(_bf(jnp.concatenate([u[rs], v_g[rs]], axis=0)),
                        _bf(jnp.concatenate([b_g[rs], k_g[rs]], axis=0)), _TN)
                s_new = (s0_list[s] + jnp.where(state_bd, d, zero_st)) * eg[(s + 1) * tlen - 1:(s + 1) * tlen, sl]
                for h in range(heads):
                    hs = slice(h * HEAD_DIM, (h + 1) * HEAD_DIM)
                    s_out_ref[s, heads * g + h] = s_new[hs, hs]

    y = jnp.concatenate(ys, axis=1)
    inv_n = 1.0 / HEAD_DIM
    mean = _seg_sum4(y, seg) * inv_n
    yc = y - mean
    var = _seg_sum4(yc * yc, seg) * inv_n
    yn = yc * lax.rsqrt(var + LNX_EPS) * lnw_ref[...] + lnb_ref[...]
    o_ref[...] = ((yn + bonus) * gate).astype(o_ref.dtype)

    if nseq == 1:
        @pl.when(ci == pl.num_programs(1) - 1)
        def _():
            sh_out_ref[0] = prev_scr[...]
            for g in range(n_groups):
                st = state_scr[g]
                for h in range(heads):
                    hs = slice(h * HEAD_DIM, (h + 1) * HEAD_DIM)
                    s_out_ref[0, heads * g + h] = st[hs, hs]


def _rwkv_mix4(p2, s_in, shift3, prm, nseq, tlen):
    batch = s_in.shape[0]
    t_len = p2.shape[0] // batch
    n_chunks = t_len // tlen
    rows = nseq * tlen
    n_heads = RWKV_WIDTH // HEAD_DIM
    const2 = lambda shape: pl.BlockSpec(shape, lambda b, c: (0, 0))
    vec = const2((1, RWKV_WIDTH))
    state_spec = pl.BlockSpec((nseq, n_heads, HEAD_DIM, HEAD_DIM), lambda b, c: (b, 0, 0, 0))
    shift_spec = pl.BlockSpec((nseq, 1, RWKV_PROJ_PAD), lambda b, c: (b, 0, 0))
    return pl.pallas_call(
        functools.partial(_rwkv4_kernel, nseq=nseq, tlen=tlen),
        grid=(batch // nseq, n_chunks),
        in_specs=[
            pl.BlockSpec((rows, RWKV_PROJ_PAD), lambda b, c: (b * n_chunks + c, 0)),
            state_spec,
            shift_spec,
            const2((1, RWKV_PROJ_PAD)),
            vec,
            const2((LANES, RWKV_WIDTH)),
            vec,
            const2((LANES, RWKV_WIDTH)),
            const2((XG_PAD, RWKV_WIDTH)),
            vec, vec, vec, vec, vec,
            const2((GROUP, GROUP)),
        ],
        out_specs=[
            pl.BlockSpec((rows, RWKV_WIDTH), lambda b, c: (b * n_chunks + c, 0)),
            state_spec,
            shift_spec,
        ],
        out_shape=[
            jax.ShapeDtypeStruct((batch * t_len, RWKV_WIDTH), BF16),
            jax.ShapeDtypeStruct(s_in.shape, F32),
            jax.ShapeDtypeStruct(shift3.shape, F32),
        ],
        scratch_shapes=[pltpu.VMEM((RWKV_WIDTH // GROUP, GROUP, GROUP), F32), pltpu.VMEM((1, RWKV_PROJ_PAD), F32)],
        compiler_params=_cparams(("parallel", "arbitrary")),
        name="rwkv_mix_t%d" % tlen,
    )(p2, s_in, shift3, prm["mu"], prm["w0"], prm["wd"], prm["a0"], prm["wa"], prm["wg"], prm["k_k"],
      prm["k_a"], prm["r_k"], prm["lnx_w"], prm["lnx_b"], prm["seg"])


def _out_router_kernel(xp_ref, xs_ref, oap_ref, oas_ref, orp_ref, ors_ref, wa_ref, wr_ref, g_ref, rw_ref, rb_ref,
                       h_ref, hn_ref, idx_ref, gate_ref, *, n_p_tiles):
    body = functools.partial(_out_router_body, wa_ref=wa_ref, wr_ref=wr_ref, g_ref=g_ref, rw_ref=rw_ref,
                             rb_ref=rb_ref, h_ref=h_ref, hn_ref=hn_ref, idx_ref=idx_ref, gate_ref=gate_ref)
    is_prompt = pl.program_id(0) < n_p_tiles
    pl.when(is_prompt)(functools.partial(body, xp_ref, oap_ref, orp_ref))
    pl.when(jnp.logical_not(is_prompt))(functools.partial(body, xs_ref, oas_ref, ors_ref))


def _out_router_body(x_ref, oa_ref, or_ref, *, wa_ref, wr_ref, g_ref, rw_ref, rb_ref, h_ref, hn_ref, idx_ref,
                     gate_ref):
    h = x_ref[...] + _mm(oa_ref[...], wa_ref[...]) + _mm(or_ref[...], wr_ref[...])
    h_ref[...] = h
    hn = _rms(h, g_ref[...])
    hn_ref[...] = hn
    hh, hl = _split2(hn)
    logits = _mm(hh, rw_ref[0]) + (_mm(hh, rw_ref[1]) + _mm(hl, rw_ref[0])) + rb_ref[...]
    lane = lax.broadcasted_iota(jnp.int32, logits.shape, 1)
    vals = []
    idxs = []
    cur = logits
    for _ in range(TOP_K):
        m = jnp.max(cur, axis=-1, keepdims=True)
        sel = jnp.min(jnp.where(cur == m, lane, LANES), axis=-1, keepdims=True)
        vals.append(m)
        idxs.append(sel)
        cur = jnp.where(lane == sel, -jnp.inf, cur)
    es = [jnp.exp(vj - vals[0]) for vj in vals]
    tot = es[0] + es[1] + es[2] + es[3]
    idx_out = jnp.zeros(logits.shape, jnp.int32)
    gate_out = jnp.zeros(logits.shape, F32)
    for j in range(TOP_K):
        idx_out = jnp.where(lane == j, idxs[j], idx_out)
        gate_out = jnp.where(lane == j, es[j] / tot, gate_out)
    idx_ref[...] = idx_out
    gate_ref[...] = gate_out


def _out_router(x_p, x_s, oa_p, oa_s, or_p, or_s, w_oa, w_or, g, rw_split, rb_pad, tm):
    n_p_tiles = x_p.shape[0] // tm
    t_tok = x_p.shape[0] + x_s.shape[0]
    row = lambda w: pl.BlockSpec((tm, w), lambda i: (i, 0))
    row_p = lambda w: pl.BlockSpec((tm, w), lambda i: (jnp.minimum(i, n_p_tiles - 1), 0))
    row_s = lambda w: pl.BlockSpec((tm, w), lambda i: (jnp.maximum(i - n_p_tiles, 0), 0))
    full = lambda shape: pl.BlockSpec(shape, lambda i: tuple(0 for _ in shape))
    return pl.pallas_call(
        functools.partial(_out_router_kernel, n_p_tiles=n_p_tiles),
        grid=(t_tok // tm,),
        in_specs=[row_p(D_MODEL), row_s(D_MODEL), row_p(ATTN_WIDTH), row_s(ATTN_WIDTH), row_p(RWKV_WIDTH),
                  row_s(RWKV_WIDTH), full((ATTN_WIDTH, D_MODEL)), full((RWKV_WIDTH, D_MODEL)), full((1, D_MODEL)),
                  full((2, D_MODEL, LANES)), full((1, LANES))],
        out_specs=[row(D_MODEL), row(D_MODEL), row(LANES), row(LANES)],
        out_shape=[
            jax.ShapeDtypeStruct((t_tok, D_MODEL), F32),
            jax.ShapeDtypeStruct((t_tok, D_MODEL), F32),
            jax.ShapeDtypeStruct((t_tok, LANES), jnp.int32),
            jax.ShapeDtypeStruct((t_tok, LANES), F32),
        ],
        compiler_params=_cparams(("arbitrary",)),
        name="out_router",
    )(x_p, x_s, oa_p, oa_s, or_p, or_s, w_oa, w_or, g, rw_split, rb_pad)


GATHER_UNROLL = 8
MOE_SUB = 256


def _gather_kernel(nv_ref, tok_ref, hn_ref, o_ref, buf, sem):
    rows = buf.shape[0]

    @pl.when(nv_ref[pl.program_id(0)] > 0)
    def _():
        def issue(r8, carry):
            for u in range(GATHER_UNROLL):
                r = r8 * GATHER_UNROLL + u
                pltpu.make_async_copy(hn_ref.at[pl.ds(tok_ref[r], 1)], buf.at[pl.ds(r, 1)], sem).start()
            return carry

        lax.fori_loop(0, rows // GATHER_UNROLL, issue, 0)
        pltpu.make_async_copy(hn_ref.at[pl.ds(0, rows)], buf, sem).wait()
        o_ref[...] = buf[...].astype(o_ref.dtype)

    @pl.when(nv_ref[pl.program_id(0)] == 0)
    def _():
        o_ref[...] = jnp.zeros_like(o_ref)


def _gather_rows(sub_valid, row_tok, hn, rows_per_step):
    n_rows = row_tok.shape[0]
    grid_spec = pltpu.PrefetchScalarGridSpec(
        num_scalar_prefetch=1,
        grid=(n_rows // rows_per_step,),
        in_specs=[
            pl.BlockSpec((rows_per_step,), lambda i, nv: (i,), memory_space=pltpu.SMEM),
            pl.BlockSpec(memory_space=pl.ANY),
        ],
        out_specs=pl.BlockSpec((rows_per_step, D_MODEL), lambda i, nv: (i, 0)),
        scratch_shapes=[pltpu.VMEM((rows_per_step, D_MODEL), F32), pltpu.SemaphoreType.DMA(())],
    )
    return pl.pallas_call(
        _gather_kernel,
        grid_spec=grid_spec,
        out_shape=jax.ShapeDtypeStruct((n_rows, D_MODEL), BF16),
        compiler_params=_cparams(("arbitrary",)),
        name="moe_gather",
    )(sub_valid, row_tok, hn)


def _expert_kernel(be_ref, nv_ref, nr_ref, x_ref, wu_ref, bu_ref, wd_ref, bd_ref, sel_ref, o_ref):
    i = pl.program_id(0)
    f = pl.program_id(1)
    n_sub = x_ref.shape[0] // MOE_SUB
    used_sub = (nv_ref[i] + (MOE_SUB - 1)) // MOE_SUB

    def body(m_rows):
        rows = slice(0, m_rows)
        z = _mm(x_ref[rows, :], wu_ref[0].astype(BF16)) + bu_ref[0]
        zn = pltpu.roll(z, z.shape[1] - 1, 1)
        glu = jnp.minimum(z, SWIGLU_LIMIT)
        lin = jnp.clip(zn, -SWIGLU_LIMIT, SWIGLU_LIMIT)
        act = (glu * jax.nn.sigmoid(SWIGLU_ALPHA * glu) * (lin + 1.0)).astype(BF16)
        actc = _mm(act, sel_ref[...]).astype(BF16)
        contrib = _mm(actc, wd_ref[0].astype(BF16))

        @pl.when(f == 0)
        def _():
            o_ref[rows, :] = contrib + bd_ref[0]
            if m_rows < o_ref.shape[0]:
                o_ref[m_rows:, :] = jnp.zeros((o_ref.shape[0] - m_rows, o_ref.shape[1]), o_ref.dtype)

        @pl.when(f > 0)
        def _():
            o_ref[rows, :] += contrib

    for k in range(1, n_sub + 1):
        pl.when(used_sub == k)(functools.partial(body, k * MOE_SUB))

    @pl.when((used_sub == 0) & (f == 0))
    def _():
        o_ref[...] = jnp.zeros_like(o_ref)


def _experts(blk_expert, blk_valid, n_real, x_sorted, w_up, b_up3, w_down, b_down3, sel, bm, tf):
    n_rows = x_sorted.shape[0]
    n_f = D_FF // tf

    def real(i, nr):
        return jnp.minimum(i, nr[0] - 1)

    def f_eff(i, f, nr):
        return jnp.where(i < nr[0], f, n_f - 1)

    grid_spec = pltpu.PrefetchScalarGridSpec(
        num_scalar_prefetch=3,
        grid=(n_rows // bm, n_f),
        in_specs=[
            pl.BlockSpec((bm, D_MODEL), lambda i, f, be, nv, nr: (real(i, nr), 0)),
            pl.BlockSpec((1, D_MODEL, 2 * tf), lambda i, f, be, nv, nr: (be[i], 0, f_eff(i, f, nr))),
            pl.BlockSpec((1, 1, 2 * tf), lambda i, f, be, nv, nr: (be[i], 0, f_eff(i, f, nr))),
            pl.BlockSpec((1, tf, D_MODEL), lambda i, f, be, nv, nr: (be[i], f_eff(i, f, nr), 0)),
            pl.BlockSpec((1, 1, D_MODEL), lambda i, f, be, nv, nr: (be[i], 0, 0)),
            pl.BlockSpec((2 * tf, tf), lambda i, f, be, nv, nr: (0, 0)),
        ],
        out_specs=pl.BlockSpec((bm, D_MODEL), lambda i, f, be, nv, nr: (i, 0)),
    )
    return pl.pallas_call(
        _expert_kernel,
        grid_spec=grid_spec,
        out_shape=jax.ShapeDtypeStruct((n_rows, D_MODEL), F32),
        compiler_params=_cparams(("arbitrary", "arbitrary")),
        name="moe_experts",
    )(blk_expert, blk_valid, n_real, x_sorted, w_up, b_up3, w_down, b_down3, sel)


def _combine_kernel(dest_ref, rows_ref, h_ref, gate_ref, g_ref, op_ref, os_ref, buf, sem, *, n_p_tiles):
    tm = h_ref.shape[0]

    def issue(t2, carry):
        for u in range(2):
            t = t2 * 2 + u
            for j in range(TOP_K):
                pltpu.make_async_copy(rows_ref.at[pl.ds(dest_ref[t * TOP_K + j], 1)],
                                      buf.at[pl.ds(j * tm + t, 1)], sem).start()
        return carry

    lax.fori_loop(0, tm // 2, issue, 0)
    pltpu.make_async_copy(rows_ref.at[pl.ds(0, TOP_K * tm)], buf, sem).wait()
    gates = gate_ref[...]
    y = h_ref[...]
    for j in range(TOP_K):
        y = y + buf[j * tm:(j + 1) * tm, :] * gates[:, j:j + 1]
    out = _rms(y, g_ref[...])
    is_prompt = pl.program_id(0) < n_p_tiles

    @pl.when(is_prompt)
    def _():
        op_ref[...] = out

    @pl.when(jnp.logical_not(is_prompt))
    def _():
        os_ref[...] = out


def _combine(dest, out_rows, h, gates, g, n_p, tm):
    t_tok = h.shape[0]
    n_p_tiles = n_p // tm
    return pl.pallas_call(
        functools.partial(_combine_kernel, n_p_tiles=n_p_tiles),
        grid=(t_tok // tm,),
        in_specs=[
            pl.BlockSpec((tm * TOP_K,), lambda i: (i,), memory_space=pltpu.SMEM),
            pl.BlockSpec(memory_space=pl.ANY),
            pl.BlockSpec((tm, D_MODEL), lambda i: (i, 0)),
            pl.BlockSpec((tm, LANES), lambda i: (i, 0)),
            pl.BlockSpec((1, D_MODEL), lambda i: (0, 0)),
        ],
        out_specs=[pl.BlockSpec((tm, D_MODEL), lambda i: (jnp.minimum(i, n_p_tiles - 1), 0)),
                   pl.BlockSpec((tm, D_MODEL), lambda i: (jnp.maximum(i - n_p_tiles, 0), 0))],
        out_shape=[jax.ShapeDtypeStruct((n_p, D_MODEL), F32),
                   jax.ShapeDtypeStruct((t_tok - n_p, D_MODEL), F32)],
        scratch_shapes=[pltpu.VMEM((TOP_K * tm, D_MODEL), F32), pltpu.SemaphoreType.DMA(())],
        compiler_params=_cparams(("arbitrary",)),
        name="moe_combine",
    )(dest, out_rows, h, gates, g)


def _q_head_order():
    order = []
    for pair in range(2):
        for j in range(4):
            order += [8 * pair + j, 8 * pair + 4 + j]
    return order


def _q_col_perm():
    cols = []
    for h in _q_head_order():
        cols += list(range(h * HEAD_DIM, (h + 1) * HEAD_DIM))
    return jnp.asarray(cols, jnp.int32)


def _pad_rwkv_cols(m):
    def padw(a, w):
        return jnp.pad(a, [(0, 0)] * (a.ndim - 1) + [(0, w - a.shape[-1])])
    return jnp.concatenate([
        m[..., :XW_OFF],
        padw(m[..., XW_OFF:XW_OFF + DECAY_LORA], LANES),
        padw(m[..., XW_OFF + DECAY_LORA:XW_OFF + DECAY_LORA + ICLR_LORA], LANES),
        padw(m[..., XW_OFF + DECAY_LORA + ICLR_LORA:], XG_PAD),
    ], axis=-1)


def _unpad_rwkv_cols(m):
    return jnp.concatenate([
        m[..., :XW_OFF + DECAY_LORA],
        m[..., XA_OFF:XA_OFF + ICLR_LORA],
        m[..., XG_OFF:XG_OFF + GATE_LORA],
    ], axis=-1)


def _pad_rows(m, rows):
    return jnp.pad(m, ((0, rows - m.shape[0]), (0, 0)))


def _rope_tables(pos):
    half = HEAD_DIM // 2
    inv_freq = ROPE_THETA ** (-jnp.arange(half, dtype=F32) / half)
    ang = pos.astype(F32)[:, None] * inv_freq[None, :]
    cos = jnp.cos(ang)
    sin = jnp.sin(ang)
    cos_t = jnp.tile(jnp.concatenate([cos, cos], axis=-1), (1, LANES // HEAD_DIM))
    sin_t = jnp.tile(jnp.concatenate([-sin, sin], axis=-1), (1, LANES // HEAD_DIM))
    return cos_t, sin_t


def _route(top_idx, bm):
    n_tok = top_idx.shape[0]
    e_flat = top_idx.reshape(-1)
    onehot = (e_flat[:, None] == jnp.arange(N_EXPERTS, dtype=jnp.int32)[None, :]).astype(jnp.int32)
    csum = jnp.cumsum(onehot, axis=0)
    rank = jnp.sum(csum * onehot, axis=1) - 1
    counts = csum[-1]
    padded = ((counts + bm - 1) // bm) * bm
    p_end = jnp.cumsum(padded)
    p_start = p_end - padded
    dest = (p_start[e_flat] + rank).astype(jnp.int32)
    n_assign = n_tok * TOP_K
    n_blocks = -(-n_assign // bm) + N_EXPERTS
    n_rows = n_blocks * bm
    tok_flat = jnp.repeat(jnp.arange(n_tok, dtype=jnp.int32), TOP_K)
    row_tok = jnp.zeros((n_rows,), jnp.int32).at[dest].set(tok_flat)
    blk_start = jnp.arange(n_blocks, dtype=jnp.int32) * bm
    blk_expert = jnp.minimum(jnp.searchsorted(p_end, blk_start, side="right"), N_EXPERTS - 1).astype(jnp.int32)
    blk_valid = jnp.clip(counts[blk_expert] - (blk_start - p_start[blk_expert]), 0, bm).astype(jnp.int32)
    sub_start = jnp.arange(n_rows // MOE_SUB, dtype=jnp.int32) * MOE_SUB
    sub_expert = blk_expert[sub_start // bm]
    sub_valid = jnp.clip(counts[sub_expert] - (sub_start - p_start[sub_expert]), 0, MOE_SUB).astype(jnp.int32)
    n_real = (p_end[-1:] // bm).astype(jnp.int32)
    return dest, row_tok, blk_expert, blk_valid, sub_valid, n_real


def _pick(n, prefs):
    for p in prefs:
        if n % p == 0:
            return p
    return n


def kernel(x_prompt, x_sample, cache_k, cache_v, state_wkv, state_shift, attn_norm_g, w_in, attn_sinks, mu_shift, decay_w0, decay_lora_up, iclr_a0, iclr_lora_up, gate_lora_up, k_k, k_a, r_k, lnx_w, lnx_b, w_out, ffn_norm_g, router_w, router_b, w_up, b_up, w_down, b_down, final_norm_g):
    depth = w_in.shape[0]
    assert depth == 1
    batch, seq, _ = x_prompt.shape
    dec_b, dec_t, _ = x_sample.shape
    n_p = batch * seq
    n_s = dec_b * dec_t
    l = 0

    qperm = _q_col_perm()
    w_in_l = w_in[l]
    w_attn = jnp.concatenate([w_in_l[:, :ATTN_WIDTH][:, qperm], w_in_l[:, ATTN_WIDTH:ATTN_WIDTH + 2 * KV_WIDTH]],
                             axis=1).astype(BF16)
    w_rwkv = _pad_rwkv_cols(w_in_l[:, ATTN_WIDTH + 2 * KV_WIDTH:]).astype(BF16)
    w_oa = w_out[l][:ATTN_WIDTH][qperm].astype(BF16)
    w_or = w_out[l][ATTN_WIDTH:].astype(BF16)
    g_attn = attn_norm_g[l][None, :]
    g_ffn = ffn_norm_g[l][None, :]
    sinks_true = attn_sinks[l].astype(F32)
    seg = (jnp.arange(GROUP)[:, None] // HEAD_DIM == jnp.arange(GROUP)[None, :] // HEAD_DIM).astype(BF16)
    prm = dict(
        mu=_pad_rwkv_cols(mu_shift[l][None, :]),
        w0=decay_w0[l][None, :], wd=_pad_rows(decay_lora_up[l], LANES).astype(BF16),
        a0=iclr_a0[l][None, :], wa=_pad_rows(iclr_lora_up[l], LANES).astype(BF16),
        wg=_pad_rows(gate_lora_up[l], XG_PAD).astype(BF16),
        k_k=k_k[l][None, :], k_a=k_a[l][None, :], r_k=r_k[l].reshape(1, RWKV_WIDTH),
        lnx_w=lnx_w[l][None, :], lnx_b=lnx_b[l][None, :], seg=seg)
    rw_pad = jnp.pad(router_w[l], ((0, 0), (0, LANES - N_EXPERTS)))
    rw_hi = rw_pad.astype(BF16)
    rw_lo = (rw_pad - rw_hi.astype(F32)).astype(BF16)
    rw_split = jnp.stack([rw_hi, rw_lo])
    rb_pad = jnp.concatenate([router_b[l], jnp.full((LANES - N_EXPERTS,), NEG_BIG, F32)])[None, :]
    b_up3 = b_up[l][:, None, :]
    b_dn = b_down[l][:, None, :]
    tf = 256
    sel = (jnp.arange(2 * tf)[:, None] == 2 * jnp.arange(tf)[None, :]).astype(BF16)

    tm_p = _pick(seq, (512, 256, 128))
    cos_p, sin_p = _rope_tables(jnp.arange(seq, dtype=jnp.int32))
    q_p, k_p, v_p = _attn_proj(x_prompt.reshape(n_p, D_MODEL), g_attn, w_attn, cos_p, sin_p, tm_p)
    tm_s = _pick(n_s, (512, 256, 128, 64, 32, 16, 8))
    pos_s = PAST_LEN + (jnp.arange(n_s, dtype=jnp.int32) % dec_t)
    cos_s, sin_s = _rope_tables(pos_s)
    q_s, k_s, v_s = _attn_proj(x_sample.reshape(n_s, D_MODEL), g_attn, w_attn, cos_s, sin_s, tm_s)
    tn = 512
    pr_p = _rwkv_proj(x_prompt.reshape(n_p, D_MODEL), g_attn, w_rwkv, _pick(n_p, (1024, 512, 256, 128)), tn)
    pr_s = _rwkv_proj(x_sample.reshape(n_s, D_MODEL), g_attn, w_rwkv, tm_s, tn)

    oa_p = _attn_prompt(q_p, k_p, v_p, sinks_true, batch, seq)
    wb = cache_k.shape[2]
    sink_rows = jnp.stack([
        jnp.repeat(jnp.stack([sinks_true[8 * pair + 4 * e + j] for e in range(2) for j in range(4)]), dec_t)
        for pair in range(2)])[:, :, None]
    oa_s, nk_s, nv_s = _attn_sample(
        q_s.reshape(dec_b, dec_t, ATTN_WIDTH), k_s.reshape(dec_b, dec_t, KV_WIDTH),
        v_s.reshape(dec_b, dec_t, KV_WIDTH), cache_k[l].reshape(dec_b, wb, KV_WIDTH),
        cache_v[l].reshape(dec_b, wb, KV_WIDTH), sink_rows, _pick(dec_b, (8, 4, 2, 1)))

    zero_state = jnp.zeros((batch, RWKV_WIDTH // HEAD_DIM, HEAD_DIM, HEAD_DIM), F32)
    zero_shift = jnp.zeros((batch, 1, RWKV_PROJ_PAD), F32)
    or_p, st_p, sh_p = _rwkv_mix4(pr_p, zero_state, zero_shift, prm, 1, _pick(seq, (64, 32, 16, 8)))
    seq_per_step = _pick(dec_b, (64 // dec_t, 1)) if 64 % dec_t == 0 else 1
    or_s, st_s, sh_s = _rwkv_mix4(pr_s, state_wkv[l], _pad_rwkv_cols(state_shift[l])[:, None, :], prm,
                                  seq_per_step, dec_t)

    n_all = n_p + n_s
    tm_o = _pick(math.gcd(n_p, n_s), (512, 256, 128, 64, 32, 16, 8))
    h, hn, idx_pad, gate_pad = _out_router(
        x_prompt.reshape(n_p, D_MODEL), x_sample.reshape(n_s, D_MODEL), oa_p, oa_s.reshape(n_s, ATTN_WIDTH),
        or_p, or_s, w_oa, w_or, g_ffn, rw_split, rb_pad, tm_o)

    bm = 1024 if n_all * TOP_K >= 1024 * N_EXPERTS else MOE_SUB
    dest, row_tok, blk_expert, blk_valid, sub_valid, n_real = _route(idx_pad[:, :TOP_K], bm)
    x_sorted = _gather_rows(sub_valid, row_tok, hn, MOE_SUB)
    out_rows = _experts(blk_expert, blk_valid, n_real, x_sorted, w_up[l], b_up3, w_down[l], b_dn, sel, bm, tf)
    y_p, y_s = _combine(dest, out_rows, h, gate_pad, final_norm_g[None, :], n_p,
                        _pick(math.gcd(n_p, n_s), (128, 64, 32, 16, 8)))

    y_prompt = y_p.reshape(batch, seq, D_MODEL)
    y_sample = y_s.reshape(dec_b, dec_t, D_MODEL)
    kp = k_p.reshape(batch, seq, 4, HEAD_DIM)[:, -WINDOW:][None]
    vp = v_p.reshape(batch, seq, 4, HEAD_DIM)[:, -WINDOW:][None]
    wp = st_p[None]
    sp = _unpad_rwkv_cols(sh_p[:, 0, :])[None]
    ks = nk_s.reshape(dec_b, wb, 4, HEAD_DIM)[None]
    vs = nv_s.reshape(dec_b, wb, 4, HEAD_DIM)[None]
    ws = st_s[None]
    ss = _unpad_rwkv_cols(sh_s[:, 0, :])[None]
    return (y_prompt, y_sample, kp, vp, wp, sp, ks, vs, ws, ss)
```

```python
import functools
import math

import jax
import jax.numpy as jnp
from jax import lax
from jax.experimental import pallas as pl
from jax.experimental.pallas import tpu as pltpu

F32 = jnp.float32
BF16 = jnp.bfloat16

D_MODEL = 2048
HEAD_DIM = 64
LANES = 128
ATTN_WIDTH = 1024
KV_WIDTH = 256
ATTN_HEADS = 16
RWKV_WIDTH = 1024
DECAY_LORA = 64
ICLR_LORA = 64
GATE_LORA = 160
RWKV_PROJ = 3 * RWKV_WIDTH + DECAY_LORA + ICLR_LORA + GATE_LORA
XW_OFF = 3 * RWKV_WIDTH
XA_OFF = XW_OFF + LANES
XG_OFF = XA_OFF + LANES
XG_PAD = 2 * LANES
RWKV_PROJ_PAD = XG_OFF + XG_PAD
WINDOW = 128
ROPE_THETA = 10000.0
PAST_LEN = 8192
N_EXPERTS = 32
TOP_K = 4
D_FF = 2048
SWIGLU_ALPHA = 1.702
SWIGLU_LIMIT = 7.0
NORM_EPS = 1e-5
LNX_EPS = 64e-5
NEG_BIG = -1e30
VMEM_LIMIT = 56 * 1024 * 1024


def _cparams(sem):
    return pltpu.CompilerParams(dimension_semantics=sem, vmem_limit_bytes=VMEM_LIMIT)


_NN = (((1,), (0,)), ((), ()))
_NT = (((1,), (1,)), ((), ()))
_TN = (((0,), (0,)), ((), ()))


def _mm(a, b, dims=_NN):
    return lax.dot_general(a, b, dims, preferred_element_type=F32)


def _split2(x):
    hi = x.astype(BF16)
    lo = (x - hi.astype(F32)).astype(BF16)
    return hi, lo


def _split3(x):
    hi = x.astype(BF16)
    r1 = x - hi.astype(F32)
    mid = r1.astype(BF16)
    lo = (r1 - mid.astype(F32)).astype(BF16)
    return hi, mid, lo


def _mm_exact_b(a, b_bf16, dims=_NN):
    h, m, l = _split3(a)
    return _mm(h, b_bf16, dims) + (_mm(m, b_bf16, dims) + _mm(l, b_bf16, dims))


def _mm_exact_a(a_bf16, b, dims=_NN):
    h, m, l = _split3(b)
    return _mm(a_bf16, h, dims) + (_mm(a_bf16, m, dims) + _mm(a_bf16, l, dims))


def _rms(x, g):
    ms = jnp.mean(x * x, axis=-1, keepdims=True)
    return (x * lax.rsqrt(ms + NORM_EPS)) * g


def _attn_proj_kernel(x_ref, g_ref, w_ref, cos_ref, sin_ref, q_ref, k_ref, v_ref):
    xn = _rms(x_ref[...], g_ref[...]).astype(BF16)
    a = _mm(xn, w_ref[...])
    cos = cos_ref[...]
    sin = sin_ref[...]
    lane = lax.broadcasted_iota(jnp.int32, cos.shape, 1)
    first_half = (lane % HEAD_DIM) < (HEAD_DIM // 2)
    n_rot = (ATTN_WIDTH + KV_WIDTH) // LANES
    for j in range(n_rot):
        t = a[:, j * LANES:(j + 1) * LANES]
        swapped = jnp.where(first_half, pltpu.roll(t, LANES - HEAD_DIM // 2, 1), pltpu.roll(t, HEAD_DIM // 2, 1))
        rot = t * cos + swapped * sin
        if j < ATTN_WIDTH // LANES:
            q_ref[:, j * LANES:(j + 1) * LANES] = (rot * (HEAD_DIM ** -0.5)).astype(BF16)
        else:
            jj = j - ATTN_WIDTH // LANES
            k_ref[:, jj * LANES:(jj + 1) * LANES] = rot
    v_ref[...] = a[:, ATTN_WIDTH + KV_WIDTH:]


def _attn_proj(x, g, w_attn, cos_t, sin_t, tm):
    t_tok = x.shape[0]
    n_pos_blocks = cos_t.shape[0] // tm
    return pl.pallas_call(
        _attn_proj_kernel,
        grid=(t_tok // tm,),
        in_specs=[
            pl.BlockSpec((tm, D_MODEL), lambda i: (i, 0)),
            pl.BlockSpec((1, D_MODEL), lambda i: (0, 0)),
            pl.BlockSpec((D_MODEL, ATTN_WIDTH + 2 * KV_WIDTH), lambda i: (0, 0)),
            pl.BlockSpec((tm, LANES), lambda i: (i % n_pos_blocks, 0)),
            pl.BlockSpec((tm, LANES), lambda i: (i % n_pos_blocks, 0)),
        ],
        out_specs=[
            pl.BlockSpec((tm, ATTN_WIDTH), lambda i: (i, 0)),
            pl.BlockSpec((tm, KV_WIDTH), lambda i: (i, 0)),
            pl.BlockSpec((tm, KV_WIDTH), lambda i: (i, 0)),
        ],
        out_shape=[
            jax.ShapeDtypeStruct((t_tok, ATTN_WIDTH), BF16),
            jax.ShapeDtypeStruct((t_tok, KV_WIDTH), F32),
            jax.ShapeDtypeStruct((t_tok, KV_WIDTH), F32),
        ],
        compiler_params=_cparams(("parallel",)),
        name="attn_proj",
    )(x, g, w_attn, cos_t, sin_t)


def _rwkv_proj_kernel(x_ref, g_ref, w_ref, p_ref, xn_ref):
    @pl.when(pl.program_id(1) == 0)
    def _():
        xn_ref[...] = _rms(x_ref[...], g_ref[...]).astype(BF16)

    p_ref[...] = _mm(xn_ref[...], w_ref[...])


def _rwkv_proj(x, g, w_rwkv, tm, tn):
    t_tok = x.shape[0]
    return pl.pallas_call(
        _rwkv_proj_kernel,
        grid=(t_tok // tm, RWKV_PROJ_PAD // tn),
        in_specs=[
            pl.BlockSpec((tm, D_MODEL), lambda i, n: (i, 0)),
            pl.BlockSpec((1, D_MODEL), lambda i, n: (0, 0)),
            pl.BlockSpec((D_MODEL, tn), lambda i, n: (0, n)),
        ],
        out_specs=pl.BlockSpec((tm, tn), lambda i, n: (i, n)),
        out_shape=jax.ShapeDtypeStruct((t_tok, RWKV_PROJ_PAD), F32),
        scratch_shapes=[pltpu.VMEM((tm, D_MODEL), BF16)],
        compiler_params=_cparams(("parallel", "arbitrary")),
        name="rwkv_proj",
    )(x, g, w_rwkv)


def _softmax_pv(s, mask, sink, vt):
    s = jnp.where(mask, s, NEG_BIG)
    m = jnp.maximum(jnp.max(s, axis=-1, keepdims=True), sink)
    p = jnp.exp(s - m)
    denom = jnp.sum(p, axis=-1, keepdims=True) + jnp.exp(sink - m)
    return _mm(p.astype(BF16), vt) * (1.0 / denom)


def _attn_prompt_kernel(q_ref, kp_ref, kc_ref, vp_ref, vc_ref, sink_ref, o_ref):
    i = pl.program_id(1)
    blk = q_ref.shape[0]
    kk = jnp.concatenate([kp_ref[...], kc_ref[...]], axis=0).astype(BF16)
    vv = jnp.concatenate([vp_ref[...], vc_ref[...]], axis=0).astype(BF16)
    a = lax.broadcasted_iota(jnp.int32, (blk, 2 * blk), 0)
    c = lax.broadcasted_iota(jnp.int32, (blk, 2 * blk), 1)
    mask = (c > a) & (c <= a + blk) & ((c >= blk) | (i > 0))
    lane_lo = lax.broadcasted_iota(jnp.int32, (blk, LANES), 1) < HEAD_DIM
    for pair in range(KV_WIDTH // LANES):
        kt = kk[:, pair * LANES:(pair + 1) * LANES]
        vt = vv[:, pair * LANES:(pair + 1) * LANES]
        for j in range(4):
            tile = pair * 4 + j
            qt = q_ref[:, tile * LANES:(tile + 1) * LANES]
            halves = []
            for e in range(2):
                qm = jnp.where(lane_lo if e == 0 else jnp.logical_not(lane_lo), qt, jnp.zeros_like(qt))
                s = _mm(qm, kt, _NT)
                halves.append(_softmax_pv(s, mask, sink_ref[8 * pair + 4 * e + j], vt))
            o_ref[:, tile * LANES:(tile + 1) * LANES] = jnp.where(lane_lo, halves[0], halves[1]).astype(BF16)


def _attn_prompt(q, k, v, sinks, batch, seq):
    blk = WINDOW
    nb = seq // blk
    cur = lambda b, i: (b * nb + i, 0)
    prev = lambda b, i: (b * nb + jnp.maximum(i - 1, 0), 0)
    return pl.pallas_call(
        _attn_prompt_kernel,
        grid=(batch, nb),
        in_specs=[
            pl.BlockSpec((blk, ATTN_WIDTH), cur),
            pl.BlockSpec((blk, KV_WIDTH), prev),
            pl.BlockSpec((blk, KV_WIDTH), cur),
            pl.BlockSpec((blk, KV_WIDTH), prev),
            pl.BlockSpec((blk, KV_WIDTH), cur),
            pl.BlockSpec(memory_space=pltpu.SMEM),
        ],
        out_specs=pl.BlockSpec((blk, ATTN_WIDTH), cur),
        out_shape=jax.ShapeDtypeStruct((batch * seq, ATTN_WIDTH), BF16),
        compiler_params=_cparams(("parallel", "arbitrary")),
        name="attn_prompt",
    )(q, k, k, v, v, sinks)


def _attn_sample_kernel(q_ref, kn_ref, vn_ref, ck_ref, cv_ref, sink_ref, o_ref, nk_ref, nv_ref):
    bb, t_len, _ = q_ref.shape
    wb = ck_ref.shape[1]
    n_keys = wb + t_len
    rows = 8 * t_len
    a = lax.broadcasted_iota(jnp.int32, (rows, n_keys), 0) % t_len
    c = lax.broadcasted_iota(jnp.int32, (rows, n_keys), 1)
    mask = ((c < wb) & (c > a + (wb - WINDOW))) | ((c >= wb) & (c - wb <= a))
    lane_lo = lax.broadcasted_iota(jnp.int32, (t_len, LANES), 1) < HEAD_DIM

    def body(b, carry):
        ck = ck_ref[b]
        cv = cv_ref[b]
        kn = kn_ref[b]
        vn = vn_ref[b]
        nk_ref[b, 0:wb - t_len, :] = ck[t_len:, :]
        nk_ref[b, wb - t_len:wb, :] = kn
        nv_ref[b, 0:wb - t_len, :] = cv[t_len:, :]
        nv_ref[b, wb - t_len:wb, :] = vn
        k_all = jnp.concatenate([ck, kn], axis=0).astype(BF16)
        v_all = jnp.concatenate([cv, vn], axis=0).astype(BF16)
        qb = q_ref[b]
        for pair in range(KV_WIDTH // LANES):
            kt = k_all[:, pair * LANES:(pair + 1) * LANES]
            vt = v_all[:, pair * LANES:(pair + 1) * LANES]
            stack = []
            for e in range(2):
                for j in range(4):
                    tile = pair * 4 + j
                    qt = qb[:, tile * LANES:(tile + 1) * LANES]
                    stack.append(jnp.where(lane_lo if e == 0 else jnp.logical_not(lane_lo), qt, jnp.zeros_like(qt)))
            qs = jnp.concatenate(stack, axis=0)
            s = _mm(qs, kt, _NT)
            o = _softmax_pv(s, mask, sink_ref[pair], vt)
            for j in range(4):
                tile = pair * 4 + j
                lo = o[j * t_len:(j + 1) * t_len, :]
                hi = o[(4 + j) * t_len:(5 + j) * t_len, :]
                o_ref[b, :, tile * LANES:(tile + 1) * LANES] = jnp.where(lane_lo, lo, hi).astype(BF16)
        return carry

    lax.fori_loop(0, bb, body, 0)


def _attn_sample(q3, k3, v3, ck, cv, sink_rows, bb):
    dec_b, t_len, _ = q3.shape
    wb = ck.shape[1]
    blk3 = lambda w: pl.BlockSpec((bb, t_len, w), lambda i: (i, 0, 0))
    cache = pl.BlockSpec((bb, wb, KV_WIDTH), lambda i: (i, 0, 0))
    return pl.pallas_call(
        _attn_sample_kernel,
        grid=(dec_b // bb,),
        in_specs=[blk3(ATTN_WIDTH), blk3(KV_WIDTH), blk3(KV_WIDTH), cache, cache,
                  pl.BlockSpec((2, 8 * t_len, 1), lambda i: (0, 0, 0))],
        out_specs=[blk3(ATTN_WIDTH), cache, cache],
        out_shape=[
            jax.ShapeDtypeStruct((dec_b, t_len, ATTN_WIDTH), BF16),
            jax.ShapeDtypeStruct(ck.shape, F32),
            jax.ShapeDtypeStruct(cv.shape, F32),
        ],
        compiler_params=_cparams(("parallel",)),
        name="attn_sample",
    )(q3, k3, v3, ck, cv, sink_rows)


GROUP = 4 * HEAD_DIM


def _bf(x):
    return x.astype(BF16)


def _seg_sum4(x, seg_mat):
    outs = []
    for t in range(x.shape[1] // GROUP):
        outs.append(_mm_exact_b(x[:, t * GROUP:(t + 1) * GROUP], seg_mat))
    return jnp.concatenate(outs, axis=1)


def _stack_heads(x, head_masks):
    zero = jnp.zeros_like(x)
    return jnp.concatenate([jnp.where(m, x, zero) for m in head_masks], axis=0)


def _sum_blocks(x, r):
    return (x[0:r] + x[r:2 * r]) + (x[2 * r:3 * r] + x[3 * r:4 * r])


def _place_heads(blocks):
    rows = []
    for h, blk in enumerate(blocks):
        rows.append(jnp.concatenate([blk if j == h else jnp.zeros_like(blk) for j in range(len(blocks))], axis=1))
    return jnp.concatenate(rows, axis=0)


def _rwkv4_kernel(p_ref, s_in_ref, sh_ref, mu_ref, w0_ref, wd_ref, a0_ref, wa_ref, wg_ref, kk_ref, ka_ref,
                  rk_ref, lnw_ref, lnb_ref, seg_ref, o_ref, s_out_ref, sh_out_ref, state_scr, prev_scr,
                  *, nseq, tlen, npar):
    rows = nseq * tlen
    heads = GROUP // HEAD_DIM
    log_t = tlen.bit_length() - 1
    log_r = rows.bit_length() - 1
    log_h = HEAD_DIM.bit_length() - 1

    ri = lax.broadcasted_iota(jnp.int32, (rows, rows), 0)
    cj = lax.broadcasted_iota(jnp.int32, (rows, rows), 1)
    bri = lax.broadcasted_iota(jnp.int32, (heads * rows, heads * rows), 0)
    bcj = lax.broadcasted_iota(jnp.int32, (heads * rows, heads * rows), 1)
    same = ((bri >> log_r) == (bcj >> log_r)) & ((bri >> log_t) == (bcj >> log_t))
    lane = lax.broadcasted_iota(jnp.int32, (rows, GROUP), 1)
    sr = lax.broadcasted_iota(jnp.int32, (GROUP, GROUP), 0)
    sc = lax.broadcasted_iota(jnp.int32, (GROUP, GROUP), 1)
    masks = dict(
        tri=jnp.where(((ri >> log_t) == (cj >> log_t)) & (ri >= cj), 1.0, 0.0).astype(BF16),
        strict_bd=same & (bri > bcj),
        incl_bd=same & (bri >= bcj),
        eye_bd=jnp.where(bri == bcj, 1.0, 0.0).astype(F32),
        head_masks=[(lane >> log_h) == h for h in range(heads)],
        state_bd=(sr >> log_h) == (sc >> log_h),
    )
    refs = (p_ref, s_in_ref, sh_ref, mu_ref, w0_ref, wd_ref, a0_ref, wa_ref, wg_ref, kk_ref, ka_ref, rk_ref,
            lnw_ref, lnb_ref, seg_ref, o_ref, s_out_ref, sh_out_ref, state_scr, prev_scr)
    for j in range(npar):
        _rwkv4_block(j, refs, masks, nseq=nseq, tlen=tlen)


def _rwkv4_block(j, refs, masks, *, nseq, tlen):
    (p_ref, s_in_ref, sh_ref, mu_ref, w0_ref, wd_ref, a0_ref, wa_ref, wg_ref, kk_ref, ka_ref, rk_ref,
     lnw_ref, lnb_ref, seg_ref, o_ref, s_out_ref, sh_out_ref, state_scr, prev_scr) = refs
    ci = pl.program_id(1)
    rows = nseq * tlen
    n_groups = RWKV_WIDTH // GROUP
    heads = GROUP // HEAD_DIM
    log_t = tlen.bit_length() - 1
    carry = nseq == 1

    p = p_ref[j] if carry else p_ref[...]
    rowi = lax.broadcasted_iota(jnp.int32, (rows, 1), 0)
    rolled = pltpu.roll(p, 1, 0)
    if carry:
        @pl.when(ci == 0)
        def _():
            prev_scr[j] = sh_ref[j]
            for g in range(n_groups):
                state_scr[j * n_groups + g] = _place_heads([s_in_ref[j, heads * g + h] for h in range(heads)])

        p_prev = jnp.where(rowi == 0, prev_scr[j], rolled)
        prev_scr[j] = p[rows - 1:rows, :]
    else:
        p_prev = rolled
        for s in range(nseq):
            p_prev = jnp.where(rowi == s * tlen, sh_ref[s], p_prev)
            sh_out_ref[s] = p[(s + 1) * tlen - 1:(s + 1) * tlen, :]

    xs = p + (p_prev - p) * mu_ref[...]
    r = xs[:, 0:RWKV_WIDTH]
    k = xs[:, RWKV_WIDTH:2 * RWKV_WIDTH]
    v = xs[:, 2 * RWKV_WIDTH:3 * RWKV_WIDTH]
    xw = xs[:, XW_OFF:XW_OFF + LANES]
    xa = xs[:, XA_OFF:XA_OFF + LANES]
    xg = xs[:, XG_OFF:XG_OFF + XG_PAD]
    z = w0_ref[...] + _mm(_bf(jnp.tanh(xw)), wd_ref[...])
    w_log = -jax.nn.softplus(-z) - 0.5
    lw = -jnp.exp(w_log)
    a = jax.nn.sigmoid(a0_ref[...] + _mm(_bf(xa), wa_ref[...]))
    gate = _mm(_bf(jax.nn.sigmoid(xg)), wg_ref[...])
    seg = seg_ref[...]
    kk = k * kk_ref[...]
    kap = kk / jnp.maximum(jnp.sqrt(_seg_sum4(kk * kk, seg)), 1e-12)
    k2 = k * (1.0 + (a - 1.0) * ka_ref[...])
    b = kap * a
    bonus = _seg_sum4(r * k2 * rk_ref[...], seg) * v

    cum = _mm_exact_a(masks["tri"], lw)
    eg = jnp.exp(cum)
    kap_t = kap * jnp.exp(cum - lw)
    r_t = r * eg
    einv = jnp.exp(-cum)
    b_t = b * einv
    k_t = k2 * einv

    strict_bd, incl_bd, eye_bd = masks["strict_bd"], masks["incl_bd"], masks["eye_bd"]
    head_masks, state_bd = masks["head_masks"], masks["state_bd"]
    zero_bd = jnp.zeros((heads * rows, heads * rows), F32)
    zero_st = jnp.zeros((GROUP, GROUP), F32)

    ys = []
    for g in range(n_groups):
        sl = slice(g * GROUP, (g + 1) * GROUP)
        kap_g, r_g, b_g, k_g, v_g = kap_t[:, sl], r_t[:, sl], b_t[:, sl], k_t[:, sl], v[:, sl]
        lkr = jnp.concatenate([_stack_heads(_bf(kap_g), head_masks), _stack_heads(_bf(r_g), head_masks)], axis=0)
        gb = _mm(lkr, _stack_heads(_bf(b_g), head_masks), _NT)
        gk = _mm(lkr, _stack_heads(_bf(k_g), head_masks), _NT)
        hr = heads * rows
        n = -jnp.where(strict_bd, gb[:hr], zero_bd)
        a_k = _bf(jnp.where(strict_bd, gk[:hr], zero_bd))
        r_b = _bf(jnp.where(incl_bd, gb[hr:], zero_bd))
        r_k = _bf(jnp.where(incl_bd, gk[hr:], zero_bd))
        t_inv = eye_bd + n
        for _ in range(max(log_t - 1, 0)):
            nb = _bf(n)
            n = _mm(nb, nb)
            t_inv = t_inv + _mm(_bf(t_inv), _bf(n))
        v_stack = _stack_heads(_bf(v_g), head_masks)

        if carry:
            s0 = state_scr[j * n_groups + g]
            sk = _mm(_bf(jnp.concatenate([kap_g, r_g], axis=0)), _bf(s0), _NT)
            k_s, r_s = sk[:rows], sk[rows:]
        else:
            s0_list, ks_list, rs_list = [], [], []
            for s in range(nseq):
                rs = slice(s * tlen, (s + 1) * tlen)
                s0 = _place_heads([s_in_ref[s, heads * g + h] for h in range(heads)])
                sk = _mm(_bf(jnp.concatenate([kap_g[rs], r_g[rs]], axis=0)), _bf(s0), _NT)
                s0_list.append(s0)
                ks_list.append(sk[:tlen])
                rs_list.append(sk[tlen:])
            k_s = jnp.concatenate(ks_list, axis=0)
            r_s = jnp.concatenate(rs_list, axis=0)

        w = _sum_blocks(_mm(a_k, v_stack), rows)
        rhs = -(k_s + w)
        u_stack = _mm(_bf(t_inv), _stack_heads(_bf(rhs), head_masks))
        u = _sum_blocks(u_stack, rows)
        y_stack = _mm(r_b, _bf(u_stack)) + _mm(r_k, v_stack)
        ys.append(r_s + _sum_blocks(y_stack, rows))

        if carry:
            d = _mm(_bf(jnp.concatenate([u, v_g], axis=0)), _bf(jnp.concatenate([b_g, k_g], axis=0)), _TN)
            state_scr[j * n_groups + g] = (s0 + jnp.where(state_bd, d, zero_st)) * eg[rows - 1:rows, sl]
        else:
            for s in range(nseq):
                rs = slice(s * tlen, (s + 1) * tlen)
                d = _mm(_bf(jnp.concatenate([u[rs], v_g[rs]], axis=0)),
                        _bf(jnp.concatenate([b_g[rs], k_g[rs]], axis=0)), _TN)
                s_new = (s0_list[s] + jnp.where(state_bd, d, zero_st)) * eg[(s + 1) * tlen - 1:(s + 1) * tlen, sl]
                for h in range(heads):
                    hs = slice(h * HEAD_DIM, (h + 1) * HEAD_DIM)
                    s_out_ref[s, heads * g + h] = s_new[hs, hs]

    y = jnp.concatenate(ys, axis=1)
    inv_n = 1.0 / HEAD_DIM
    mean = _seg_sum4(y, seg) * inv_n
    yc = y - mean
    var = _seg_sum4(yc * yc, seg) * inv_n
    yn = yc * lax.rsqrt(var + LNX_EPS) * lnw_ref[...] + lnb_ref[...]
    out = ((yn + bonus) * gate).astype(o_ref.dtype)
    if carry:
        o_ref[j] = out

        @pl.when(ci == pl.num_programs(1) - 1)
        def _():
            sh_out_ref[j] = prev_scr[j]
            for g in range(n_groups):
                st = state_scr[j * n_groups + g]
                for h in range(heads):
                    hs = slice(h * HEAD_DIM, (h + 1) * HEAD_DIM)
                    s_out_ref[j, heads * g + h] = st[hs, hs]
    else:
        o_ref[...] = out


def _rwkv_mix4(p2, s_in, shift3, prm, nseq, tlen, npar=1):
    batch = s_in.shape[0]
    t_len = p2.shape[0] // batch
    n_chunks = t_len // tlen
    rows = nseq * tlen
    n_heads = RWKV_WIDTH // HEAD_DIM
    per_step = nseq * npar
    const2 = lambda shape: pl.BlockSpec(shape, lambda b, c: (0, 0))
    vec = const2((1, RWKV_WIDTH))
    state_spec = pl.BlockSpec((per_step, n_heads, HEAD_DIM, HEAD_DIM), lambda b, c: (b, 0, 0, 0))
    shift_spec = pl.BlockSpec((per_step, 1, RWKV_PROJ_PAD), lambda b, c: (b, 0, 0))
    if nseq == 1:
        p_in = p2.reshape(batch, t_len, RWKV_PROJ_PAD)
        p_spec = pl.BlockSpec((npar, tlen, RWKV_PROJ_PAD), lambda b, c: (b, c, 0))
        o_spec = pl.BlockSpec((npar, tlen, RWKV_WIDTH), lambda b, c: (b, c, 0))
        o_shape = jax.ShapeDtypeStruct((batch, t_len, RWKV_WIDTH), BF16)
    else:
        p_in = p2
        p_spec = pl.BlockSpec((rows, RWKV_PROJ_PAD), lambda b, c: (b * n_chunks + c, 0))
        o_spec = pl.BlockSpec((rows, RWKV_WIDTH), lambda b, c: (b * n_chunks + c, 0))
        o_shape = jax.ShapeDtypeStruct((batch * t_len, RWKV_WIDTH), BF16)
    o, s_out, sh_out = pl.pallas_call(
        functools.partial(_rwkv4_kernel, nseq=nseq, tlen=tlen, npar=npar),
        grid=(batch // per_step, n_chunks),
        in_specs=[
            p_spec,
            state_spec,
            shift_spec,
            const2((1, RWKV_PROJ_PAD)),
            vec,
            const2((LANES, RWKV_WIDTH)),
            vec,
            const2((LANES, RWKV_WIDTH)),
            const2((XG_PAD, RWKV_WIDTH)),
            vec, vec, vec, vec, vec,
            const2((GROUP, GROUP)),
        ],
        out_specs=[o_spec, state_spec, shift_spec],
        out_shape=[
            o_shape,
            jax.ShapeDtypeStruct(s_in.shape, F32),
            jax.ShapeDtypeStruct(shift3.shape, F32),
        ],
        scratch_shapes=[pltpu.VMEM((npar * (RWKV_WIDTH // GROUP), GROUP, GROUP), F32),
                        pltpu.VMEM((npar, 1, RWKV_PROJ_PAD), F32)],
        compiler_params=_cparams(("parallel", "arbitrary")),
        name="rwkv_mix_t%d" % tlen,
    )(p_in, s_in, shift3, prm["mu"], prm["w0"], prm["wd"], prm["a0"], prm["wa"], prm["wg"], prm["k_k"],
      prm["k_a"], prm["r_k"], prm["lnx_w"], prm["lnx_b"], prm["seg"])
    return o.reshape(batch * t_len, RWKV_WIDTH), s_out, sh_out


def _out_router_kernel(xp_ref, xs_ref, oap_ref, oas_ref, orp_ref, ors_ref, wa_ref, wr_ref, g_ref, rw_ref, rb_ref,
                       h_ref, hn_ref, idx_ref, gate_ref, *, n_p_tiles):
    body = functools.partial(_out_router_body, wa_ref=wa_ref, wr_ref=wr_ref, g_ref=g_ref, rw_ref=rw_ref,
                             rb_ref=rb_ref, h_ref=h_ref, hn_ref=hn_ref, idx_ref=idx_ref, gate_ref=gate_ref)
    is_prompt = pl.program_id(0) < n_p_tiles
    pl.when(is_prompt)(functools.partial(body, xp_ref, oap_ref, orp_ref))
    pl.when(jnp.logical_not(is_prompt))(functools.partial(body, xs_ref, oas_ref, ors_ref))


def _out_router_body(x_ref, oa_ref, or_ref, *, wa_ref, wr_ref, g_ref, rw_ref, rb_ref, h_ref, hn_ref, idx_ref,
                     gate_ref):
    h = x_ref[...] + _mm(oa_ref[...], wa_ref[...]) + _mm(or_ref[...], wr_ref[...])
    h_ref[...] = h
    hn = _rms(h, g_ref[...])
    hn_ref[...] = hn
    hh, hl = _split2(hn)
    logits = _mm(hh, rw_ref[0]) + (_mm(hh, rw_ref[1]) + _mm(hl, rw_ref[0])) + rb_ref[...]
    lane = lax.broadcasted_iota(jnp.int32, logits.shape, 1)
    vals = []
    idxs = []
    cur = logits
    for _ in range(TOP_K):
        m = jnp.max(cur, axis=-1, keepdims=True)
        sel = jnp.min(jnp.where(cur == m, lane, LANES), axis=-1, keepdims=True)
        vals.append(m)
        idxs.append(sel)
        cur = jnp.where(lane == sel, -jnp.inf, cur)
    es = [jnp.exp(vj - vals[0]) for vj in vals]
    tot = es[0] + es[1] + es[2] + es[3]
    idx_out = jnp.zeros(logits.shape, jnp.int32)
    gate_out = jnp.zeros(logits.shape, F32)
    for j in range(TOP_K):
        idx_out = jnp.where(lane == j, idxs[j], idx_out)
        gate_out = jnp.where(lane == j, es[j] / tot, gate_out)
    idx_ref[...] = idx_out
    gate_ref[...] = gate_out


def _out_router(x_p, x_s, oa_p, oa_s, or_p, or_s, w_oa, w_or, g, rw_split, rb_pad, tm):
    n_p_tiles = x_p.shape[0] // tm
    t_tok = x_p.shape[0] + x_s.shape[0]
    row = lambda w: pl.BlockSpec((tm, w), lambda i: (i, 0))
    row_p = lambda w: pl.BlockSpec((tm, w), lambda i: (jnp.minimum(i, n_p_tiles - 1), 0))
    row_s = lambda w: pl.BlockSpec((tm, w), lambda i: (jnp.maximum(i - n_p_tiles, 0), 0))
    full = lambda shape: pl.BlockSpec(shape, lambda i: tuple(0 for _ in shape))
    return pl.pallas_call(
        functools.partial(_out_router_kernel, n_p_tiles=n_p_tiles),
        grid=(t_tok // tm,),
        in_specs=[row_p(D_MODEL), row_s(D_MODEL), row_p(ATTN_WIDTH), row_s(ATTN_WIDTH), row_p(RWKV_WIDTH),
                  row_s(RWKV_WIDTH), full((ATTN_WIDTH, D_MODEL)), full((RWKV_WIDTH, D_MODEL)), full((1, D_MODEL)),
                  full((2, D_MODEL, LANES)), full((1, LANES))],
        out_specs=[row(D_MODEL), row(D_MODEL), row(LANES), row(LANES)],
        out_shape=[
            jax.ShapeDtypeStruct((t_tok, D_MODEL), F32),
            jax.ShapeDtypeStruct((t_tok, D_MODEL), F32),
            jax.ShapeDtypeStruct((t_tok, LANES), jnp.int32),
            jax.ShapeDtypeStruct((t_tok, LANES), F32),
        ],
        compiler_params=_cparams(("arbitrary",)),
        name="out_router",
    )(x_p, x_s, oa_p, oa_s, or_p, or_s, w_oa, w_or, g, rw_split, rb_pad)


GATHER_UNROLL = 8
MOE_SUB = 256


def _gather_kernel(nv_ref, tok_ref, tok_next_ref, hn_ref, o_ref, buf, sem):
    rows = buf.shape[1]
    i = pl.program_id(0)
    last = pl.num_programs(0) - 1
    slot = i % 2

    def issue(tok, dst_slot):
        def body(r8, carry):
            for u in range(GATHER_UNROLL):
                r = r8 * GATHER_UNROLL + u
                pltpu.make_async_copy(hn_ref.at[pl.ds(tok[r], 1)], buf.at[dst_slot, pl.ds(r, 1)],
                                      sem.at[dst_slot]).start()
            return carry

        lax.fori_loop(0, rows // GATHER_UNROLL, body, 0)

    @pl.when((i == 0) & (nv_ref[0] > 0))
    def _():
        issue(tok_ref, 0)

    @pl.when((i < last) & (nv_ref[jnp.minimum(i + 1, last)] > 0))
    def _():
        issue(tok_next_ref, 1 - slot)

    @pl.when(nv_ref[i] > 0)
    def _():
        pltpu.make_async_copy(hn_ref.at[pl.ds(0, rows)], buf.at[slot], sem.at[slot]).wait()
        o_ref[...] = buf[slot].astype(o_ref.dtype)

    @pl.when(nv_ref[i] == 0)
    def _():
        o_ref[...] = jnp.zeros_like(o_ref)


def _gather_rows(sub_valid, row_tok, hn, rows_per_step):
    n_rows = row_tok.shape[0]
    n_steps = n_rows // rows_per_step
    grid_spec = pltpu.PrefetchScalarGridSpec(
        num_scalar_prefetch=1,
        grid=(n_steps,),
        in_specs=[
            pl.BlockSpec((rows_per_step,), lambda i, nv: (i,), memory_space=pltpu.SMEM),
            pl.BlockSpec((rows_per_step,), lambda i, nv: (jnp.minimum(i + 1, n_steps - 1),),
                         memory_space=pltpu.SMEM),
            pl.BlockSpec(memory_space=pl.ANY),
        ],
        out_specs=pl.BlockSpec((rows_per_step, D_MODEL), lambda i, nv: (i, 0)),
        scratch_shapes=[pltpu.VMEM((2, rows_per_step, D_MODEL), F32), pltpu.SemaphoreType.DMA((2,))],
    )
    return pl.pallas_call(
        _gather_kernel,
        grid_spec=grid_spec,
        out_shape=jax.ShapeDtypeStruct((n_rows, D_MODEL), BF16),
        compiler_params=_cparams(("arbitrary",)),
        name="moe_gather",
    )(sub_valid, row_tok, row_tok, hn)


def _expert_kernel(be_ref, nv_ref, nr_ref, x_ref, wu_ref, bu_ref, wd_ref, bd_ref, sel_ref, o_ref):
    i = pl.program_id(0)
    f = pl.program_id(1)
    n_sub = x_ref.shape[0] // MOE_SUB
    used_sub = (nv_ref[i] + (MOE_SUB - 1)) // MOE_SUB

    def body(m_rows):
        rows = slice(0, m_rows)
        z = _mm(x_ref[rows, :], wu_ref[0].astype(BF16)) + bu_ref[0]
        zn = pltpu.roll(z, z.shape[1] - 1, 1)
        glu = jnp.minimum(z, SWIGLU_LIMIT)
        lin = jnp.clip(zn, -SWIGLU_LIMIT, SWIGLU_LIMIT)
        act = (glu * jax.nn.sigmoid(SWIGLU_ALPHA * glu) * (lin + 1.0)).astype(BF16)
        actc = _mm(act, sel_ref[...]).astype(BF16)
        wd = wd_ref[0].astype(BF16)

        @pl.when(f == 0)
        def _():
            o_ref[rows, :] = _mm(actc, wd) + bd_ref[0]
            if m_rows < o_ref.shape[0]:
                o_ref[m_rows:, :] = jnp.zeros((o_ref.shape[0] - m_rows, o_ref.shape[1]), o_ref.dtype)

        @pl.when(f > 0)
        def _():
            o_ref[rows, :] += _mm(actc, wd)

    for k in range(1, n_sub + 1):
        pl.when(used_sub == k)(functools.partial(body, k * MOE_SUB))

    @pl.when((used_sub == 0) & (f == 0))
    def _():
        o_ref[...] = jnp.zeros_like(o_ref)


def _experts(blk_expert, blk_valid, n_real, x_sorted, w_up, b_up3, w_down, b_down3, sel, bm, tf):
    n_rows = x_sorted.shape[0]
    n_f = D_FF // tf

    def real(i, nr):
        return jnp.minimum(i, nr[0] - 1)

    def f_eff(i, f, nr):
        return jnp.where(i < nr[0], f, n_f - 1)

    grid_spec = pltpu.PrefetchScalarGridSpec(
        num_scalar_prefetch=3,
        grid=(n_rows // bm, n_f),
        in_specs=[
            pl.BlockSpec((bm, D_MODEL), lambda i, f, be, nv, nr: (real(i, nr), 0)),
            pl.BlockSpec((1, D_MODEL, 2 * tf), lambda i, f, be, nv, nr: (be[i], 0, f_eff(i, f, nr))),
            pl.BlockSpec((1, 1, 2 * tf), lambda i, f, be, nv, nr: (be[i], 0, f_eff(i, f, nr))),
            pl.BlockSpec((1, tf, D_MODEL), lambda i, f, be, nv, nr: (be[i], f_eff(i, f, nr), 0)),
            pl.BlockSpec((1, 1, D_MODEL), lambda i, f, be, nv, nr: (be[i], 0, 0)),
            pl.BlockSpec((2 * tf, tf), lambda i, f, be, nv, nr: (0, 0)),
        ],
        out_specs=pl.BlockSpec((bm, D_MODEL), lambda i, f, be, nv, nr: (i, 0)),
    )
    return pl.pallas_call(
        _expert_kernel,
        grid_spec=grid_spec,
        out_shape=jax.ShapeDtypeStruct((n_rows, D_MODEL), F32),
        compiler_params=_cparams(("arbitrary", "arbitrary")),
        name="moe_experts",
    )(blk_expert, blk_valid, n_real, x_sorted, w_up, b_up3, w_down, b_down3, sel)


def _combine_kernel(dest_ref, dest_next_ref, rows_ref, h_ref, gate_ref, g_ref, op_ref, os_ref, buf, sem,
                    *, n_p_tiles):
    tm = h_ref.shape[0]
    i = pl.program_id(0)
    last = pl.num_programs(0) - 1
    slot = i % 2

    def issue(dest, dst_slot):
        def body(t2, carry):
            for u in range(2):
                t = t2 * 2 + u
                for j in range(TOP_K):
                    pltpu.make_async_copy(rows_ref.at[pl.ds(dest[t * TOP_K + j], 1)],
                                          buf.at[dst_slot, pl.ds(j * tm + t, 1)], sem.at[dst_slot]).start()
            return carry

        lax.fori_loop(0, tm // 2, body, 0)

    @pl.when(i == 0)
    def _():
        issue(dest_ref, 0)

    @pl.when(i < last)
    def _():
        issue(dest_next_ref, 1 - slot)

    pltpu.make_async_copy(rows_ref.at[pl.ds(0, TOP_K * tm)], buf.at[slot], sem.at[slot]).wait()
    gates = gate_ref[...]
    y = h_ref[...]
    for j in range(TOP_K):
        y = y + buf[slot, j * tm:(j + 1) * tm, :] * gates[:, j:j + 1]
    out = _rms(y, g_ref[...])
    is_prompt = pl.program_id(0) < n_p_tiles

    @pl.when(is_prompt)
    def _():
        op_ref[...] = out

    @pl.when(jnp.logical_not(is_prompt))
    def _():
        os_ref[...] = out


def _combine(dest, out_rows, h, gates, g, n_p, tm):
    t_tok = h.shape[0]
    n_p_tiles = n_p // tm
    n_steps = t_tok // tm
    return pl.pallas_call(
        functools.partial(_combine_kernel, n_p_tiles=n_p_tiles),
        grid=(n_steps,),
        in_specs=[
            pl.BlockSpec((tm * TOP_K,), lambda i: (i,), memory_space=pltpu.SMEM),
            pl.BlockSpec((tm * TOP_K,), lambda i: (jnp.minimum(i + 1, n_steps - 1),), memory_space=pltpu.SMEM),
            pl.BlockSpec(memory_space=pl.ANY),
            pl.BlockSpec((tm, D_MODEL), lambda i: (i, 0)),
            pl.BlockSpec((tm, LANES), lambda i: (i, 0)),
            pl.BlockSpec((1, D_MODEL), lambda i: (0, 0)),
        ],
        out_specs=[pl.BlockSpec((tm, D_MODEL), lambda i: (jnp.minimum(i, n_p_tiles - 1), 0)),
                   pl.BlockSpec((tm, D_MODEL), lambda i: (jnp.maximum(i - n_p_tiles, 0), 0))],
        out_shape=[jax.ShapeDtypeStruct((n_p, D_MODEL), F32),
                   jax.ShapeDtypeStruct((t_tok - n_p, D_MODEL), F32)],
        scratch_shapes=[pltpu.VMEM((2, TOP_K * tm, D_MODEL), F32), pltpu.SemaphoreType.DMA((2,))],
        compiler_params=_cparams(("arbitrary",)),
        name="moe_combine",
    )(dest, dest, out_rows, h, gates, g)


def _q_head_order():
    order = []
    for pair in range(2):
        for j in range(4):
            order += [8 * pair + j, 8 * pair + 4 + j]
    return order


def _q_col_perm():
    cols = []
    for h in _q_head_order():
        cols += list(range(h * HEAD_DIM, (h + 1) * HEAD_DIM))
    return jnp.asarray(cols, jnp.int32)


def _pad_rwkv_cols(m):
    def padw(a, w):
        return jnp.pad(a, [(0, 0)] * (a.ndim - 1) + [(0, w - a.shape[-1])])
    return jnp.concatenate([
        m[..., :XW_OFF],
        padw(m[..., XW_OFF:XW_OFF + DECAY_LORA], LANES),
        padw(m[..., XW_OFF + DECAY_LORA:XW_OFF + DECAY_LORA + ICLR_LORA], LANES),
        padw(m[..., XW_OFF + DECAY_LORA + ICLR_LORA:], XG_PAD),
    ], axis=-1)


def _unpad_rwkv_cols(m):
    return jnp.concatenate([
        m[..., :XW_OFF + DECAY_LORA],
        m[..., XA_OFF:XA_OFF + ICLR_LORA],
        m[..., XG_OFF:XG_OFF + GATE_LORA],
    ], axis=-1)


def _pad_rows(m, rows):
    return jnp.pad(m, ((0, rows - m.shape[0]), (0, 0)))


def _rope_tables(pos):
    half = HEAD_DIM // 2
    inv_freq = ROPE_THETA ** (-jnp.arange(half, dtype=F32) / half)
    ang = pos.astype(F32)[:, None] * inv_freq[None, :]
    cos = jnp.cos(ang)
    sin = jnp.sin(ang)
    cos_t = jnp.tile(jnp.concatenate([cos, cos], axis=-1), (1, LANES // HEAD_DIM))
    sin_t = jnp.tile(jnp.concatenate([-sin, sin], axis=-1), (1, LANES // HEAD_DIM))
    return cos_t, sin_t


def _route(top_idx, bm):
    n_tok = top_idx.shape[0]
    e_flat = top_idx.reshape(-1)
    onehot = (e_flat[:, None] == jnp.arange(N_EXPERTS, dtype=jnp.int32)[None, :]).astype(jnp.int32)
    csum = jnp.cumsum(onehot, axis=0)
    rank = jnp.sum(csum * onehot, axis=1) - 1
    counts = csum[-1]
    padded = ((counts + bm - 1) // bm) * bm
    p_end = jnp.cumsum(padded)
    p_start = p_end - padded
    dest = (p_start[e_flat] + rank).astype(jnp.int32)
    n_assign = n_tok * TOP_K
    n_blocks = -(-n_assign // bm) + N_EXPERTS
    n_rows = n_blocks * bm
    tok_flat = jnp.repeat(jnp.arange(n_tok, dtype=jnp.int32), TOP_K)
    row_tok = jnp.zeros((n_rows,), jnp.int32).at[dest].set(tok_flat, unique_indices=True)
    blk_start = jnp.arange(n_blocks, dtype=jnp.int32) * bm
    blk_expert = jnp.minimum(jnp.searchsorted(p_end, blk_start, side="right"), N_EXPERTS - 1).astype(jnp.int32)
    blk_valid = jnp.clip(counts[blk_expert] - (blk_start - p_start[blk_expert]), 0, bm).astype(jnp.int32)
    sub_start = jnp.arange(n_rows // MOE_SUB, dtype=jnp.int32) * MOE_SUB
    sub_expert = blk_expert[sub_start // bm]
    sub_valid = jnp.clip(counts[sub_expert] - (sub_start - p_start[sub_expert]), 0, MOE_SUB).astype(jnp.int32)
    n_real = (p_end[-1:] // bm).astype(jnp.int32)
    return dest, row_tok, blk_expert, blk_valid, sub_valid, n_real


def _pick(n, prefs):
    for p in prefs:
        if n % p == 0:
            return p
    return n


def kernel(x_prompt, x_sample, cache_k, cache_v, state_wkv, state_shift, attn_norm_g, w_in, attn_sinks, mu_shift, decay_w0, decay_lora_up, iclr_a0, iclr_lora_up, gate_lora_up, k_k, k_a, r_k, lnx_w, lnx_b, w_out, ffn_norm_g, router_w, router_b, w_up, b_up, w_down, b_down, final_norm_g):
    depth = w_in.shape[0]
    assert depth == 1
    batch, seq, _ = x_prompt.shape
    dec_b, dec_t, _ = x_sample.shape
    n_p = batch * seq
    n_s = dec_b * dec_t
    l = 0

    qperm = _q_col_perm()
    w_in_l = w_in[l]
    w_attn = jnp.concatenate([w_in_l[:, :ATTN_WIDTH][:, qperm], w_in_l[:, ATTN_WIDTH:ATTN_WIDTH + 2 * KV_WIDTH]],
                             axis=1).astype(BF16)
    w_rwkv = _pad_rwkv_cols(w_in_l[:, ATTN_WIDTH + 2 * KV_WIDTH:]).astype(BF16)
    w_oa = w_out[l][:ATTN_WIDTH][qperm].astype(BF16)
    w_or = w_out[l][ATTN_WIDTH:].astype(BF16)
    g_attn = attn_norm_g[l][None, :]
    g_ffn = ffn_norm_g[l][None, :]
    sinks_true = attn_sinks[l].astype(F32)
    seg = (jnp.arange(GROUP)[:, None] // HEAD_DIM == jnp.arange(GROUP)[None, :] // HEAD_DIM).astype(BF16)
    prm = dict(
        mu=_pad_rwkv_cols(mu_shift[l][None, :]),
        w0=decay_w0[l][None, :], wd=_pad_rows(decay_lora_up[l], LANES).astype(BF16),
        a0=iclr_a0[l][None, :], wa=_pad_rows(iclr_lora_up[l], LANES).astype(BF16),
        wg=_pad_rows(gate_lora_up[l], XG_PAD).astype(BF16),
        k_k=k_k[l][None, :], k_a=k_a[l][None, :], r_k=r_k[l].reshape(1, RWKV_WIDTH),
        lnx_w=lnx_w[l][None, :], lnx_b=lnx_b[l][None, :], seg=seg)
    rw_pad = jnp.pad(router_w[l], ((0, 0), (0, LANES - N_EXPERTS)))
    rw_hi = rw_pad.astype(BF16)
    rw_lo = (rw_pad - rw_hi.astype(F32)).astype(BF16)
    rw_split = jnp.stack([rw_hi, rw_lo])
    rb_pad = jnp.concatenate([router_b[l], jnp.full((LANES - N_EXPERTS,), NEG_BIG, F32)])[None, :]
    b_up3 = b_up[l][:, None, :]
    b_dn = b_down[l][:, None, :]
    tf = 256
    sel = (jnp.arange(2 * tf)[:, None] == 2 * jnp.arange(tf)[None, :]).astype(BF16)

    tm_p = _pick(seq, (512, 256, 128))
    cos_p, sin_p = _rope_tables(jnp.arange(seq, dtype=jnp.int32))
    q_p, k_p, v_p = _attn_proj(x_prompt.reshape(n_p, D_MODEL), g_attn, w_attn, cos_p, sin_p, tm_p)
    tm_s = _pick(n_s, (512, 256, 128, 64, 32, 16, 8))
    pos_s = PAST_LEN + (jnp.arange(n_s, dtype=jnp.int32) % dec_t)
    cos_s, sin_s = _rope_tables(pos_s)
    q_s, k_s, v_s = _attn_proj(x_sample.reshape(n_s, D_MODEL), g_attn, w_attn, cos_s, sin_s, tm_s)
    tn = 512
    pr_p = _rwkv_proj(x_prompt.reshape(n_p, D_MODEL), g_attn, w_rwkv, _pick(n_p, (1024, 512, 256, 128)), tn)
    pr_s = _rwkv_proj(x_sample.reshape(n_s, D_MODEL), g_attn, w_rwkv, tm_s, tn)

    oa_p = _attn_prompt(q_p, k_p, v_p, sinks_true, batch, seq)
    wb = cache_k.shape[2]
    sink_rows = jnp.stack([
        jnp.repeat(jnp.stack([sinks_true[8 * pair + 4 * e + j] for e in range(2) for j in range(4)]), dec_t)
        for pair in range(2)])[:, :, None]
    oa_s, nk_s, nv_s = _attn_sample(
        q_s.reshape(dec_b, dec_t, ATTN_WIDTH), k_s.reshape(dec_b, dec_t, KV_WIDTH),
        v_s.reshape(dec_b, dec_t, KV_WIDTH), cache_k[l].reshape(dec_b, wb, KV_WIDTH),
        cache_v[l].reshape(dec_b, wb, KV_WIDTH), sink_rows, _pick(dec_b, (8, 4, 2, 1)))

    zero_state = jnp.zeros((batch, RWKV_WIDTH // HEAD_DIM, HEAD_DIM, HEAD_DIM), F32)
    zero_shift = jnp.zeros((batch, 1, RWKV_PROJ_PAD), F32)
    or_p, st_p, sh_p = _rwkv_mix4(pr_p, zero_state, zero_shift, prm, 1, _pick(seq, (64, 32, 16, 8)),
                                  npar=1)
    seq_per_step = _pick(dec_b, (64 // dec_t, 1)) if 64 % dec_t == 0 else 1
    or_s, st_s, sh_s = _rwkv_mix4(pr_s, state_wkv[l], _pad_rwkv_cols(state_shift[l])[:, None, :], prm,
                                  seq_per_step, dec_t)

    n_all = n_p + n_s
    tm_o = _pick(math.gcd(n_p, n_s), (512, 256, 128, 64, 32, 16, 8))
    h, hn, idx_pad, gate_pad = _out_router(
        x_prompt.reshape(n_p, D_MODEL), x_sample.reshape(n_s, D_MODEL), oa_p, oa_s.reshape(n_s, ATTN_WIDTH),
        or_p, or_s, w_oa, w_or, g_ffn, rw_split, rb_pad, tm_o)

    bm = 1024 if n_all * TOP_K >= 1024 * N_EXPERTS else MOE_SUB
    dest, row_tok, blk_expert, blk_valid, sub_valid, n_real = _route(idx_pad[:, :TOP_K], bm)
    x_sorted = _gather_rows(sub_valid, row_tok, hn, MOE_SUB)
    out_rows = _experts(blk_expert, blk_valid, n_real, x_sorted, w_up[l], b_up3, w_down[l], b_dn, sel, bm, tf)
    y_p, y_s = _combine(dest, out_rows, h, gate_pad, final_norm_g[None, :], n_p,
                        _pick(math.gcd(n_p, n_s), (128, 64, 32, 16, 8)))

    y_prompt = y_p.reshape(batch, seq, D_MODEL)
    y_sample = y_s.reshape(dec_b, dec_t, D_MODEL)
    kp = k_p.reshape(batch, seq, 4, HEAD_DIM)[:, -WINDOW:][None]
    vp = v_p.reshape(batch, seq, 4, HEAD_DIM)[:, -WINDOW:][None]
    wp = st_p[None]
    sp = _unpad_rwkv_cols(sh_p[:, 0, :])[None]
    ks = nk_s.reshape(dec_b, wb, 4, HEAD_DIM)[None]
    vs = nv_s.reshape(dec_b, wb, 4, HEAD_DIM)[None]
    ws = st_s[None]
    ss = _unpad_rwkv_cols(sh_s[:, 0, :])[None]
    return (y_prompt, y_sample, kp, vp, wp, sp, ks, vs, ws, ss)
```

```python
import functools
import math

import jax
import jax.numpy as jnp
from jax import lax
from jax.experimental import pallas as pl
from jax.experimental.pallas import tpu as pltpu

F32 = jnp.float32
BF16 = jnp.bfloat16

D_MODEL = 2048
HEAD_DIM = 64
LANES = 128
ATTN_WIDTH = 1024
KV_WIDTH = 256
ATTN_HEADS = 16
RWKV_WIDTH = 1024
DECAY_LORA = 64
ICLR_LORA = 64
GATE_LORA = 160
RWKV_PROJ = 3 * RWKV_WIDTH + DECAY_LORA + ICLR_LORA + GATE_LORA
XW_OFF = 3 * RWKV_WIDTH
XA_OFF = XW_OFF + LANES
XG_OFF = XA_OFF + LANES
XG_PAD = 2 * LANES
RWKV_PROJ_PAD = XG_OFF + XG_PAD
WINDOW = 128
ROPE_THETA = 10000.0
PAST_LEN = 8192
N_EXPERTS = 32
TOP_K = 4
D_FF = 2048
SWIGLU_ALPHA = 1.702
SWIGLU_LIMIT = 7.0
NORM_EPS = 1e-5
LNX_EPS = 64e-5
NEG_BIG = -1e30
VMEM_LIMIT = 56 * 1024 * 1024


def _cparams(sem):
    return pltpu.CompilerParams(dimension_semantics=sem, vmem_limit_bytes=VMEM_LIMIT)


_NN = (((1,), (0,)), ((), ()))
_NT = (((1,), (1,)), ((), ()))
_TN = (((0,), (0,)), ((), ()))


def _mm(a, b, dims=_NN):
    return lax.dot_general(a, b, dims, preferred_element_type=F32)


def _split2(x):
    hi = x.astype(BF16)
    lo = (x - hi.astype(F32)).astype(BF16)
    return hi, lo


def _split3(x):
    hi = x.astype(BF16)
    r1 = x - hi.astype(F32)
    mid = r1.astype(BF16)
    lo = (r1 - mid.astype(F32)).astype(BF16)
    return hi, mid, lo


def _mm_exact_b(a, b_bf16, dims=_NN):
    h, m, l = _split3(a)
    return _mm(h, b_bf16, dims) + (_mm(m, b_bf16, dims) + _mm(l, b_bf16, dims))


def _mm_exact_a(a_bf16, b, dims=_NN):
    h, m, l = _split3(b)
    return _mm(a_bf16, h, dims) + (_mm(a_bf16, m, dims) + _mm(a_bf16, l, dims))


def _rms(x, g):
    ms = jnp.mean(x * x, axis=-1, keepdims=True)
    return (x * lax.rsqrt(ms + NORM_EPS)) * g


def _attn_proj_kernel(x_ref, g_ref, w_ref, cos_ref, sin_ref, q_ref, k_ref, v_ref):
    xn = _rms(x_ref[...], g_ref[...]).astype(BF16)
    a = _mm(xn, w_ref[...])
    cos = cos_ref[...]
    sin = sin_ref[...]
    lane = lax.broadcasted_iota(jnp.int32, cos.shape, 1)
    first_half = (lane % HEAD_DIM) < (HEAD_DIM // 2)
    n_rot = (ATTN_WIDTH + KV_WIDTH) // LANES
    for j in range(n_rot):
        t = a[:, j * LANES:(j + 1) * LANES]
        swapped = jnp.where(first_half, pltpu.roll(t, LANES - HEAD_DIM // 2, 1), pltpu.roll(t, HEAD_DIM // 2, 1))
        rot = t * cos + swapped * sin
        if j < ATTN_WIDTH // LANES:
            q_ref[:, j * LANES:(j + 1) * LANES] = (rot * (HEAD_DIM ** -0.5)).astype(BF16)
        else:
            jj = j - ATTN_WIDTH // LANES
            k_ref[:, jj * LANES:(jj + 1) * LANES] = rot
    v_ref[...] = a[:, ATTN_WIDTH + KV_WIDTH:]


def _attn_proj(x, g, w_attn, cos_t, sin_t, tm):
    t_tok = x.shape[0]
    n_pos_blocks = cos_t.shape[0] // tm
    return pl.pallas_call(
        _attn_proj_kernel,
        grid=(t_tok // tm,),
        in_specs=[
            pl.BlockSpec((tm, D_MODEL), lambda i: (i, 0)),
            pl.BlockSpec((1, D_MODEL), lambda i: (0, 0)),
            pl.BlockSpec((D_MODEL, ATTN_WIDTH + 2 * KV_WIDTH), lambda i: (0, 0)),
            pl.BlockSpec((tm, LANES), lambda i: (i % n_pos_blocks, 0)),
            pl.BlockSpec((tm, LANES), lambda i: (i % n_pos_blocks, 0)),
        ],
        out_specs=[
            pl.BlockSpec((tm, ATTN_WIDTH), lambda i: (i, 0)),
            pl.BlockSpec((tm, KV_WIDTH), lambda i: (i, 0)),
            pl.BlockSpec((tm, KV_WIDTH), lambda i: (i, 0)),
        ],
        out_shape=[
            jax.ShapeDtypeStruct((t_tok, ATTN_WIDTH), BF16),
            jax.ShapeDtypeStruct((t_tok, KV_WIDTH), F32),
            jax.ShapeDtypeStruct((t_tok, KV_WIDTH), F32),
        ],
        compiler_params=_cparams(("parallel",)),
        name="attn_proj",
    )(x, g, w_attn, cos_t, sin_t)


def _rwkv_proj_kernel(x_ref, g_ref, w_ref, p_ref, xn_ref):
    @pl.when(pl.program_id(1) == 0)
    def _():
        xn_ref[...] = _rms(x_ref[...], g_ref[...]).astype(BF16)

    p_ref[...] = _mm(xn_ref[...], w_ref[...])


def _rwkv_proj(x, g, w_rwkv, tm, tn):
    t_tok = x.shape[0]
    return pl.pallas_call(
        _rwkv_proj_kernel,
        grid=(t_tok // tm, RWKV_PROJ_PAD // tn),
        in_specs=[
            pl.BlockSpec((tm, D_MODEL), lambda i, n: (i, 0)),
            pl.BlockSpec((1, D_MODEL), lambda i, n: (0, 0)),
            pl.BlockSpec((D_MODEL, tn), lambda i, n: (0, n)),
        ],
        out_specs=pl.BlockSpec((tm, tn), lambda i, n: (i, n)),
        out_shape=jax.ShapeDtypeStruct((t_tok, RWKV_PROJ_PAD), F32),
        scratch_shapes=[pltpu.VMEM((tm, D_MODEL), BF16)],
        compiler_params=_cparams(("parallel", "arbitrary")),
        name="rwkv_proj",
    )(x, g, w_rwkv)


def _softmax_pv(s, mask, sink, vt):
    s = jnp.where(mask, s, NEG_BIG)
    m = jnp.maximum(jnp.max(s, axis=-1, keepdims=True), sink)
    p = jnp.exp(s - m)
    denom = jnp.sum(p, axis=-1, keepdims=True) + jnp.exp(sink - m)
    return _mm(p.astype(BF16), vt) * (1.0 / denom)


def _attn_prompt_kernel(q_ref, kp_ref, kc_ref, vp_ref, vc_ref, sink_ref, o_ref):
    i = pl.program_id(1)
    blk = q_ref.shape[0]
    kk = jnp.concatenate([kp_ref[...], kc_ref[...]], axis=0).astype(BF16)
    vv = jnp.concatenate([vp_ref[...], vc_ref[...]], axis=0).astype(BF16)
    a = lax.broadcasted_iota(jnp.int32, (blk, 2 * blk), 0)
    c = lax.broadcasted_iota(jnp.int32, (blk, 2 * blk), 1)
    mask = (c > a) & (c <= a + blk) & ((c >= blk) | (i > 0))
    lane_lo = lax.broadcasted_iota(jnp.int32, (blk, LANES), 1) < HEAD_DIM
    for pair in range(KV_WIDTH // LANES):
        kt = kk[:, pair * LANES:(pair + 1) * LANES]
        vt = vv[:, pair * LANES:(pair + 1) * LANES]
        for j in range(4):
            tile = pair * 4 + j
            qt = q_ref[:, tile * LANES:(tile + 1) * LANES]
            halves = []
            for e in range(2):
                qm = jnp.where(lane_lo if e == 0 else jnp.logical_not(lane_lo), qt, jnp.zeros_like(qt))
                s = _mm(qm, kt, _NT)
                halves.append(_softmax_pv(s, mask, sink_ref[8 * pair + 4 * e + j], vt))
            o_ref[:, tile * LANES:(tile + 1) * LANES] = jnp.where(lane_lo, halves[0], halves[1]).astype(BF16)


def _attn_prompt(q, k, v, sinks, batch, seq):
    blk = WINDOW
    nb = seq // blk
    cur = lambda b, i: (b * nb + i, 0)
    prev = lambda b, i: (b * nb + jnp.maximum(i - 1, 0), 0)
    return pl.pallas_call(
        _attn_prompt_kernel,
        grid=(batch, nb),
        in_specs=[
            pl.BlockSpec((blk, ATTN_WIDTH), cur),
            pl.BlockSpec((blk, KV_WIDTH), prev),
            pl.BlockSpec((blk, KV_WIDTH), cur),
            pl.BlockSpec((blk, KV_WIDTH), prev),
            pl.BlockSpec((blk, KV_WIDTH), cur),
            pl.BlockSpec(memory_space=pltpu.SMEM),
        ],
        out_specs=pl.BlockSpec((blk, ATTN_WIDTH), cur),
        out_shape=jax.ShapeDtypeStruct((batch * seq, ATTN_WIDTH), BF16),
        compiler_params=_cparams(("parallel", "arbitrary")),
        name="attn_prompt",
    )(q, k, k, v, v, sinks)


def _attn_sample_kernel(q_ref, kn_ref, vn_ref, ck_ref, cv_ref, sink_ref, o_ref, nk_ref, nv_ref):
    bb, t_len, _ = q_ref.shape
    wb = ck_ref.shape[1]
    n_keys = wb + t_len
    rows = 8 * t_len
    a = lax.broadcasted_iota(jnp.int32, (rows, n_keys), 0) % t_len
    c = lax.broadcasted_iota(jnp.int32, (rows, n_keys), 1)
    mask = ((c < wb) & (c > a + (wb - WINDOW))) | ((c >= wb) & (c - wb <= a))
    lane_lo = lax.broadcasted_iota(jnp.int32, (t_len, LANES), 1) < HEAD_DIM

    def body(b, carry):
        ck = ck_ref[b]
        cv = cv_ref[b]
        kn = kn_ref[b]
        vn = vn_ref[b]
        nk_ref[b, 0:wb - t_len, :] = ck[t_len:, :]
        nk_ref[b, wb - t_len:wb, :] = kn
        nv_ref[b, 0:wb - t_len, :] = cv[t_len:, :]
        nv_ref[b, wb - t_len:wb, :] = vn
        k_all = jnp.concatenate([ck, kn], axis=0).astype(BF16)
        v_all = jnp.concatenate([cv, vn], axis=0).astype(BF16)
        qb = q_ref[b]
        for pair in range(KV_WIDTH // LANES):
            kt = k_all[:, pair * LANES:(pair + 1) * LANES]
            vt = v_all[:, pair * LANES:(pair + 1) * LANES]
            stack = []
            for e in range(2):
                for j in range(4):
                    tile = pair * 4 + j
                    qt = qb[:, tile * LANES:(tile + 1) * LANES]
                    stack.append(jnp.where(lane_lo if e == 0 else jnp.logical_not(lane_lo), qt, jnp.zeros_like(qt)))
            qs = jnp.concatenate(stack, axis=0)
            s = _mm(qs, kt, _NT)
            o = _softmax_pv(s, mask, sink_ref[pair], vt)
            for j in range(4):
                tile = pair * 4 + j
                lo = o[j * t_len:(j + 1) * t_len, :]
                hi = o[(4 + j) * t_len:(5 + j) * t_len, :]
                o_ref[b, :, tile * LANES:(tile + 1) * LANES] = jnp.where(lane_lo, lo, hi).astype(BF16)
        return carry

    lax.fori_loop(0, bb, body, 0)


def _attn_sample(q3, k3, v3, ck, cv, sink_rows, bb):
    dec_b, t_len, _ = q3.shape
    wb = ck.shape[1]
    blk3 = lambda w: pl.BlockSpec((bb, t_len, w), lambda i: (i, 0, 0))
    cache = pl.BlockSpec((bb, wb, KV_WIDTH), lambda i: (i, 0, 0))
    return pl.pallas_call(
        _attn_sample_kernel,
        grid=(dec_b // bb,),
        in_specs=[blk3(ATTN_WIDTH), blk3(KV_WIDTH), blk3(KV_WIDTH), cache, cache,
                  pl.BlockSpec((2, 8 * t_len, 1), lambda i: (0, 0, 0))],
        out_specs=[blk3(ATTN_WIDTH), cache, cache],
        out_shape=[
            jax.ShapeDtypeStruct((dec_b, t_len, ATTN_WIDTH), BF16),
            jax.ShapeDtypeStruct(ck.shape, F32),
            jax.ShapeDtypeStruct(cv.shape, F32),
        ],
        compiler_params=_cparams(("parallel",)),
        name="attn_sample",
    )(q3, k3, v3, ck, cv, sink_rows)


GROUP = 4 * HEAD_DIM


def _bf(x):
    return x.astype(BF16)


def _seg_sum4(x, seg_mat):
    outs = []
    for t in range(x.shape[1] // GROUP):
        outs.append(_mm_exact_b(x[:, t * GROUP:(t + 1) * GROUP], seg_mat))
    return jnp.concatenate(outs, axis=1)


def _stack_heads(x, head_masks):
    zero = jnp.zeros_like(x)
    return jnp.concatenate([jnp.where(m, x, zero) for m in head_masks], axis=0)


def _sum_blocks(x, r):
    return (x[0:r] + x[r:2 * r]) + (x[2 * r:3 * r] + x[3 * r:4 * r])


def _place_heads(blocks):
    rows = []
    for h, blk in enumerate(blocks):
        rows.append(jnp.concatenate([blk if j == h else jnp.zeros_like(blk) for j in range(len(blocks))], axis=1))
    return jnp.concatenate(rows, axis=0)


def _rwkv4_kernel(p_ref, s_in_ref, sh_ref, mu_ref, w0_ref, wd_ref, a0_ref, wa_ref, wg_ref, kk_ref, ka_ref,
                  rk_ref, lnw_ref, lnb_ref, seg_ref, o_ref, s_out_ref, sh_out_ref, state_scr, prev_scr,
                  *, nseq, tlen, npar):
    rows = nseq * tlen
    heads = GROUP // HEAD_DIM
    log_t = tlen.bit_length() - 1
    log_r = rows.bit_length() - 1
    log_h = HEAD_DIM.bit_length() - 1

    ri = lax.broadcasted_iota(jnp.int32, (rows, rows), 0)
    cj = lax.broadcasted_iota(jnp.int32, (rows, rows), 1)
    bri = lax.broadcasted_iota(jnp.int32, (heads * rows, heads * rows), 0)
    bcj = lax.broadcasted_iota(jnp.int32, (heads * rows, heads * rows), 1)
    same = ((bri >> log_r) == (bcj >> log_r)) & ((bri >> log_t) == (bcj >> log_t))
    lane = lax.broadcasted_iota(jnp.int32, (rows, GROUP), 1)
    sr = lax.broadcasted_iota(jnp.int32, (GROUP, GROUP), 0)
    sc = lax.broadcasted_iota(jnp.int32, (GROUP, GROUP), 1)
    masks = dict(
        tri=jnp.where(((ri >> log_t) == (cj >> log_t)) & (ri >= cj), 1.0, 0.0).astype(BF16),
        strict_bd=same & (bri > bcj),
        incl_bd=same & (bri >= bcj),
        eye_bd=jnp.where(bri == bcj, 1.0, 0.0).astype(F32),
        head_masks=[(lane >> log_h) == h for h in range(heads)],
        state_bd=(sr >> log_h) == (sc >> log_h),
    )
    refs = (p_ref, s_in_ref, sh_ref, mu_ref, w0_ref, wd_ref, a0_ref, wa_ref, wg_ref, kk_ref, ka_ref, rk_ref,
            lnw_ref, lnb_ref, seg_ref, o_ref, s_out_ref, sh_out_ref, state_scr, prev_scr)
    for j in range(npar):
        _rwkv4_block(j, refs, masks, nseq=nseq, tlen=tlen)


def _rwkv4_block(j, refs, masks, *, nseq, tlen):
    (p_ref, s_in_ref, sh_ref, mu_ref, w0_ref, wd_ref, a0_ref, wa_ref, wg_ref, kk_ref, ka_ref, rk_ref,
     lnw_ref, lnb_ref, seg_ref, o_ref, s_out_ref, sh_out_ref, state_scr, prev_scr) = refs
    ci = pl.program_id(1)
    rows = nseq * tlen
    n_groups = RWKV_WIDTH // GROUP
    heads = GROUP // HEAD_DIM
    log_t = tlen.bit_length() - 1
    carry = nseq == 1

    p = p_ref[j] if carry else p_ref[...]
    rowi = lax.broadcasted_iota(jnp.int32, (rows, 1), 0)
    rolled = pltpu.roll(p, 1, 0)
    if carry:
        @pl.when(ci == 0)
        def _():
            prev_scr[j] = sh_ref[j]
            for g in range(n_groups):
                state_scr[j * n_groups + g] = _place_heads([s_in_ref[j, heads * g + h] for h in range(heads)])

        p_prev = jnp.where(rowi == 0, prev_scr[j], rolled)
        prev_scr[j] = p[rows - 1:rows, :]
    else:
        p_prev = rolled
        for s in range(nseq):
            p_prev = jnp.where(rowi == s * tlen, sh_ref[s], p_prev)
            sh_out_ref[s] = p[(s + 1) * tlen - 1:(s + 1) * tlen, :]

    xs = p + (p_prev - p) * mu_ref[...]
    r = xs[:, 0:RWKV_WIDTH]
    k = xs[:, RWKV_WIDTH:2 * RWKV_WIDTH]
    v = xs[:, 2 * RWKV_WIDTH:3 * RWKV_WIDTH]
    xw = xs[:, XW_OFF:XW_OFF + LANES]
    xa = xs[:, XA_OFF:XA_OFF + LANES]
    xg = xs[:, XG_OFF:XG_OFF + XG_PAD]
    z = w0_ref[...] + _mm(_bf(jnp.tanh(xw)), wd_ref[...])
    w_log = -jax.nn.softplus(-z) - 0.5
    lw = -jnp.exp(w_log)
    a = jax.nn.sigmoid(a0_ref[...] + _mm(_bf(xa), wa_ref[...]))
    gate = _mm(_bf(jax.nn.sigmoid(xg)), wg_ref[...])
    seg = seg_ref[...]
    kk = k * kk_ref[...]
    kap = kk / jnp.maximum(jnp.sqrt(_seg_sum4(kk * kk, seg)), 1e-12)
    k2 = k * (1.0 + (a - 1.0) * ka_ref[...])
    b = kap * a
    bonus = _seg_sum4(r * k2 * rk_ref[...], seg) * v

    cum = _mm_exact_a(masks["tri"], lw)
    eg = jnp.exp(cum)
    kap_t = kap * jnp.exp(cum - lw)
    r_t = r * eg
    einv = jnp.exp(-cum)
    b_t = b * einv
    k_t = k2 * einv

    strict_bd, incl_bd, eye_bd = masks["strict_bd"], masks["incl_bd"], masks["eye_bd"]
    head_masks, state_bd = masks["head_masks"], masks["state_bd"]
    zero_bd = jnp.zeros((heads * rows, heads * rows), F32)
    zero_st = jnp.zeros((GROUP, GROUP), F32)

    ys = []
    for g in range(n_groups):
        sl = slice(g * GROUP, (g + 1) * GROUP)
        kap_g, r_g, b_g, k_g, v_g = kap_t[:, sl], r_t[:, sl], b_t[:, sl], k_t[:, sl], v[:, sl]
        lkr = jnp.concatenate([_stack_heads(_bf(kap_g), head_masks), _stack_heads(_bf(r_g), head_masks)], axis=0)
        gb = _mm(lkr, _stack_heads(_bf(b_g), head_masks), _NT)
        gk = _mm(lkr, _stack_heads(_bf(k_g), head_masks), _NT)
        hr = heads * rows
        n = -jnp.where(strict_bd, gb[:hr], zero_bd)
        a_k = _bf(jnp.where(strict_bd, gk[:hr], zero_bd))
        r_b = _bf(jnp.where(incl_bd, gb[hr:], zero_bd))
        r_k = _bf(jnp.where(incl_bd, gk[hr:], zero_bd))
        t_inv = eye_bd + n
        for _ in range(max(log_t - 1, 0)):
            nb = _bf(n)
            n = _mm(nb, nb)
            t_inv = t_inv + _mm(_bf(t_inv), _bf(n))
        v_stack = _stack_heads(_bf(v_g), head_masks)

        if carry:
            s0 = state_scr[j * n_groups + g]
            sk = _mm(_bf(jnp.concatenate([kap_g, r_g], axis=0)), _bf(s0), _NT)
            k_s, r_s = sk[:rows], sk[rows:]
        else:
            s0_list, ks_list, rs_list = [], [], []
            for s in range(nseq):
                rs = slice(s * tlen, (s + 1) * tlen)
                s0 = _place_heads([s_in_ref[s, heads * g + h] for h in range(heads)])
                sk = _mm(_bf(jnp.concatenate([kap_g[rs], r_g[rs]], axis=0)), _bf(s0), _NT)
                s0_list.append(s0)
                ks_list.append(sk[:tlen])
                rs_list.append(sk[tlen:])
            k_s = jnp.concatenate(ks_list, axis=0)
            r_s = jnp.concatenate(rs_list, axis=0)

        w = _sum_blocks(_mm(a_k, v_stack), rows)
        rhs = -(k_s + w)
        u_stack = _mm(_bf(t_inv), _stack_heads(_bf(rhs), head_masks))
        u = _sum_blocks(u_stack, rows)
        y_stack = _mm(r_b, _bf(u_stack)) + _mm(r_k, v_stack)
        ys.append(r_s + _sum_blocks(y_stack, rows))

        if carry:
            d = _mm(_bf(jnp.concatenate([u, v_g], axis=0)), _bf(jnp.concatenate([b_g, k_g], axis=0)), _TN)
            state_scr[j * n_groups + g] = (s0 + jnp.where(state_bd, d, zero_st)) * eg[rows - 1:rows, sl]
        else:
            for s in range(nseq):
                rs = slice(s * tlen, (s + 1) * tlen)
                d = _mm(_bf(jnp.concatenate([u[rs], v_g[rs]], axis=0)),
                        _bf(jnp.concatenate([b_g[rs], k_g[rs]], axis=0)), _TN)
                s_new = (s0_list[s] + jnp.where(state_bd, d, zero_st)) * eg[(s + 1) * tlen - 1:(s + 1) * tlen, sl]
                for h in range(heads):
                    hs = slice(h * HEAD_DIM, (h + 1) * HEAD_DIM)
                    s_out_ref[s, heads * g + h] = s_new[hs, hs]

    y = jnp.concatenate(ys, axis=1)
    inv_n = 1.0 / HEAD_DIM
    mean = _seg_sum4(y, seg) * inv_n
    yc = y - mean
    var = _seg_sum4(yc * yc, seg) * inv_n
    yn = yc * lax.rsqrt(var + LNX_EPS) * lnw_ref[...] + lnb_ref[...]
    out = ((yn + bonus) * gate).astype(o_ref.dtype)
    if carry:
        o_ref[j] = out

        @pl.when(ci == pl.num_programs(1) - 1)
        def _():
            sh_out_ref[j] = prev_scr[j]
            for g in range(n_groups):
                st = state_scr[j * n_groups + g]
                for h in range(heads):
                    hs = slice(h * HEAD_DIM, (h + 1) * HEAD_DIM)
                    s_out_ref[j, heads * g + h] = st[hs, hs]
    else:
        o_ref[...] = out


def _rwkv_mix4(p2, s_in, shift3, prm, nseq, tlen, npar=1):
    batch = s_in.shape[0]
    t_len = p2.shape[0] // batch
    n_chunks = t_len // tlen
    rows = nseq * tlen
    n_heads = RWKV_WIDTH // HEAD_DIM
    per_step = nseq * npar
    const2 = lambda shape: pl.BlockSpec(shape, lambda b, c: (0, 0))
    vec = const2((1, RWKV_WIDTH))
    state_spec = pl.BlockSpec((per_step, n_heads, HEAD_DIM, HEAD_DIM), lambda b, c: (b, 0, 0, 0))
    shift_spec = pl.BlockSpec((per_step, 1, RWKV_PROJ_PAD), lambda b, c: (b, 0, 0))
    if nseq == 1:
        p_in = p2.reshape(batch, t_len, RWKV_PROJ_PAD)
        p_spec = pl.BlockSpec((npar, tlen, RWKV_PROJ_PAD), lambda b, c: (b, c, 0))
        o_spec = pl.BlockSpec((npar, tlen, RWKV_WIDTH), lambda b, c: (b, c, 0))
        o_shape = jax.ShapeDtypeStruct((batch, t_len, RWKV_WIDTH), BF16)
    else:
        p_in = p2
        p_spec = pl.BlockSpec((rows, RWKV_PROJ_PAD), lambda b, c: (b * n_chunks + c, 0))
        o_spec = pl.BlockSpec((rows, RWKV_WIDTH), lambda b, c: (b * n_chunks + c, 0))
        o_shape = jax.ShapeDtypeStruct((batch * t_len, RWKV_WIDTH), BF16)
    o, s_out, sh_out = pl.pallas_call(
        functools.partial(_rwkv4_kernel, nseq=nseq, tlen=tlen, npar=npar),
        grid=(batch // per_step, n_chunks),
        in_specs=[
            p_spec,
            state_spec,
            shift_spec,
            const2((1, RWKV_PROJ_PAD)),
            vec,
            const2((LANES, RWKV_WIDTH)),
            vec,
            const2((LANES, RWKV_WIDTH)),
            const2((XG_PAD, RWKV_WIDTH)),
            vec, vec, vec, vec, vec,
            const2((GROUP, GROUP)),
        ],
        out_specs=[o_spec, state_spec, shift_spec],
        out_shape=[
            o_shape,
            jax.ShapeDtypeStruct(s_in.shape, F32),
            jax.ShapeDtypeStruct(shift3.shape, F32),
        ],
        scratch_shapes=[pltpu.VMEM((npar * (RWKV_WIDTH // GROUP), GROUP, GROUP), F32),
                        pltpu.VMEM((npar, 1, RWKV_PROJ_PAD), F32)],
        compiler_params=_cparams(("parallel", "arbitrary")),
        name="rwkv_mix_t%d" % tlen,
    )(p_in, s_in, shift3, prm["mu"], prm["w0"], prm["wd"], prm["a0"], prm["wa"], prm["wg"], prm["k_k"],
      prm["k_a"], prm["r_k"], prm["lnx_w"], prm["lnx_b"], prm["seg"])
    return o.reshape(batch * t_len, RWKV_WIDTH), s_out, sh_out


def _out_router_kernel(xp_ref, xs_ref, oap_ref, oas_ref, orp_ref, ors_ref, wa_ref, wr_ref, g_ref, rw_ref, rb_ref,
                       h_ref, hn_ref, idx_ref, gate_ref, *, n_p_tiles):
    body = functools.partial(_out_router_body, wa_ref=wa_ref, wr_ref=wr_ref, g_ref=g_ref, rw_ref=rw_ref,
                             rb_ref=rb_ref, h_ref=h_ref, hn_ref=hn_ref, idx_ref=idx_ref, gate_ref=gate_ref)
    is_prompt = pl.program_id(0) < n_p_tiles
    pl.when(is_prompt)(functools.partial(body, xp_ref, oap_ref, orp_ref))
    pl.when(jnp.logical_not(is_prompt))(functools.partial(body, xs_ref, oas_ref, ors_ref))


def _out_router_body(x_ref, oa_ref, or_ref, *, wa_ref, wr_ref, g_ref, rw_ref, rb_ref, h_ref, hn_ref, idx_ref,
                     gate_ref):
    h = x_ref[...] + _mm(oa_ref[...], wa_ref[...]) + _mm(or_ref[...], wr_ref[...])
    h_ref[...] = h
    hn = _rms(h, g_ref[...])
    hn_ref[...] = hn
    hh, hl = _split2(hn)
    logits = _mm(hh, rw_ref[0]) + (_mm(hh, rw_ref[1]) + _mm(hl, rw_ref[0])) + rb_ref[...]
    lane = lax.broadcasted_iota(jnp.int32, logits.shape, 1)
    vals = []
    idxs = []
    cur = logits
    for _ in range(TOP_K):
        m = jnp.max(cur, axis=-1, keepdims=True)
        sel = jnp.min(jnp.where(cur == m, lane, LANES), axis=-1, keepdims=True)
        vals.append(m)
        idxs.append(sel)
        cur = jnp.where(lane == sel, -jnp.inf, cur)
    es = [jnp.exp(vj - vals[0]) for vj in vals]
    tot = es[0] + es[1] + es[2] + es[3]
    idx_out = jnp.zeros(logits.shape, jnp.int32)
    gate_out = jnp.zeros(logits.shape, F32)
    for j in range(TOP_K):
        idx_out = jnp.where(lane == j, idxs[j], idx_out)
        gate_out = jnp.where(lane == j, es[j] / tot, gate_out)
    idx_ref[...] = idx_out
    gate_ref[...] = gate_out


def _out_router(x_p, x_s, oa_p, oa_s, or_p, or_s, w_oa, w_or, g, rw_split, rb_pad, tm):
    n_p_tiles = x_p.shape[0] // tm
    t_tok = x_p.shape[0] + x_s.shape[0]
    row = lambda w: pl.BlockSpec((tm, w), lambda i: (i, 0))
    row_p = lambda w: pl.BlockSpec((tm, w), lambda i: (jnp.minimum(i, n_p_tiles - 1), 0))
    row_s = lambda w: pl.BlockSpec((tm, w), lambda i: (jnp.maximum(i - n_p_tiles, 0), 0))
    full = lambda shape: pl.BlockSpec(shape, lambda i: tuple(0 for _ in shape))
    return pl.pallas_call(
        functools.partial(_out_router_kernel, n_p_tiles=n_p_tiles),
        grid=(t_tok // tm,),
        in_specs=[row_p(D_MODEL), row_s(D_MODEL), row_p(ATTN_WIDTH), row_s(ATTN_WIDTH), row_p(RWKV_WIDTH),
                  row_s(RWKV_WIDTH), full((ATTN_WIDTH, D_MODEL)), full((RWKV_WIDTH, D_MODEL)), full((1, D_MODEL)),
                  full((2, D_MODEL, LANES)), full((1, LANES))],
        out_specs=[row(D_MODEL), row(D_MODEL), row(LANES), row(LANES)],
        out_shape=[
            jax.ShapeDtypeStruct((t_tok, D_MODEL), F32),
            jax.ShapeDtypeStruct((t_tok, D_MODEL), F32),
            jax.ShapeDtypeStruct((t_tok, LANES), jnp.int32),
            jax.ShapeDtypeStruct((t_tok, LANES), F32),
        ],
        compiler_params=_cparams(("arbitrary",)),
        name="out_router",
    )(x_p, x_s, oa_p, oa_s, or_p, or_s, w_oa, w_or, g, rw_split, rb_pad)


GATHER_UNROLL = 8
MOE_SUB = 256


def _gather_kernel(nv_ref, tok_ref, tok_next_ref, hn_ref, o_ref, buf, sem):
    rows = buf.shape[1]
    i = pl.program_id(0)
    last = pl.num_programs(0) - 1
    slot = i % 2

    def issue(tok, dst_slot):
        def body(r8, carry):
            for u in range(GATHER_UNROLL):
                r = r8 * GATHER_UNROLL + u
                pltpu.make_async_copy(hn_ref.at[pl.ds(tok[r], 1)], buf.at[dst_slot, pl.ds(r, 1)],
                                      sem.at[dst_slot]).start()
            return carry

        lax.fori_loop(0, rows // GATHER_UNROLL, body, 0)

    @pl.when((i == 0) & (nv_ref[0] > 0))
    def _():
        issue(tok_ref, 0)

    @pl.when((i < last) & (nv_ref[jnp.minimum(i + 1, last)] > 0))
    def _():
        issue(tok_next_ref, 1 - slot)

    @pl.when(nv_ref[i] > 0)
    def _():
        pltpu.make_async_copy(hn_ref.at[pl.ds(0, rows)], buf.at[slot], sem.at[slot]).wait()
        o_ref[...] = buf[slot].astype(o_ref.dtype)

    @pl.when(nv_ref[i] == 0)
    def _():
        o_ref[...] = jnp.zeros_like(o_ref)


def _gather_rows(sub_valid, row_tok, hn, rows_per_step):
    n_rows = row_tok.shape[0]
    n_steps = n_rows // rows_per_step
    grid_spec = pltpu.PrefetchScalarGridSpec(
        num_scalar_prefetch=1,
        grid=(n_steps,),
        in_specs=[
            pl.BlockSpec((rows_per_step,), lambda i, nv: (i,), memory_space=pltpu.SMEM),
            pl.BlockSpec((rows_per_step,), lambda i, nv: (jnp.minimum(i + 1, n_steps - 1),),
                         memory_space=pltpu.SMEM),
            pl.BlockSpec(memory_space=pl.ANY),
        ],
        out_specs=pl.BlockSpec((rows_per_step, D_MODEL), lambda i, nv: (i, 0)),
        scratch_shapes=[pltpu.VMEM((2, rows_per_step, D_MODEL), F32), pltpu.SemaphoreType.DMA((2,))],
    )
    return pl.pallas_call(
        _gather_kernel,
        grid_spec=grid_spec,
        out_shape=jax.ShapeDtypeStruct((n_rows, D_MODEL), BF16),
        compiler_params=_cparams(("arbitrary",)),
        name="moe_gather",
    )(sub_valid, row_tok, row_tok, hn)


def _expert_kernel(be_ref, nv_ref, nr_ref, x_ref, wu_ref, bu_ref, wd_ref, bd_ref, sel_ref, o_ref, act_scr):
    i = pl.program_id(0)
    s = pl.program_id(1)
    n_f = pl.num_programs(1) - 1
    n_sub = x_ref.shape[0] // MOE_SUB
    used_sub = (nv_ref[i] + (MOE_SUB - 1)) // MOE_SUB

    def stage_a(rows):
        z = _mm(x_ref[rows, :], wu_ref[0].astype(BF16)) + bu_ref[0]
        zn = pltpu.roll(z, z.shape[1] - 1, 1)
        glu = jnp.minimum(z, SWIGLU_LIMIT)
        lin = jnp.clip(zn, -SWIGLU_LIMIT, SWIGLU_LIMIT)
        act = (glu * jax.nn.sigmoid(SWIGLU_ALPHA * glu) * (lin + 1.0)).astype(BF16)
        return _mm(act, sel_ref[...]).astype(BF16)

    def first(m_rows):
        rows = slice(0, m_rows)
        o_ref[rows, :] = jnp.broadcast_to(bd_ref[0], (m_rows, o_ref.shape[1]))
        if m_rows < o_ref.shape[0]:
            o_ref[m_rows:, :] = jnp.zeros((o_ref.shape[0] - m_rows, o_ref.shape[1]), o_ref.dtype)
        act_scr[rows, :] = stage_a(rows)

    def middle(m_rows):
        rows = slice(0, m_rows)
        contrib = _mm(act_scr[rows, :], wd_ref[0].astype(BF16))
        new_act = stage_a(rows)
        o_ref[rows, :] += contrib
        act_scr[rows, :] = new_act

    def final(m_rows):
        rows = slice(0, m_rows)
        o_ref[rows, :] += _mm(act_scr[rows, :], wd_ref[0].astype(BF16))

    for k in range(1, n_sub + 1):
        live = used_sub == k
        pl.when(live & (s == 0))(functools.partial(first, k * MOE_SUB))
        pl.when(live & (s > 0) & (s < n_f))(functools.partial(middle, k * MOE_SUB))
        pl.when(live & (s == n_f))(functools.partial(final, k * MOE_SUB))

    @pl.when((used_sub == 0) & (s == 0))
    def _():
        o_ref[...] = jnp.zeros_like(o_ref)


def _experts(blk_expert, blk_valid, n_real, x_sorted, w_up, b_up3, w_down, b_down3, sel, bm, tf):
    n_rows = x_sorted.shape[0]
    n_f = D_FF // tf

    def real(i, nr):
        return jnp.minimum(i, nr[0] - 1)

    def up_tile(i, s, nr):
        return jnp.where(i < nr[0], jnp.minimum(s, n_f - 1), n_f - 1)

    def down_tile(i, s, nr):
        return jnp.where(i < nr[0], jnp.maximum(s - 1, 0), n_f - 1)

    grid_spec = pltpu.PrefetchScalarGridSpec(
        num_scalar_prefetch=3,
        grid=(n_rows // bm, n_f + 1),
        in_specs=[
            pl.BlockSpec((bm, D_MODEL), lambda i, s, be, nv, nr: (real(i, nr), 0)),
            pl.BlockSpec((1, D_MODEL, 2 * tf), lambda i, s, be, nv, nr: (be[i], 0, up_tile(i, s, nr))),
            pl.BlockSpec((1, 1, 2 * tf), lambda i, s, be, nv, nr: (be[i], 0, up_tile(i, s, nr))),
            pl.BlockSpec((1, tf, D_MODEL), lambda i, s, be, nv, nr: (be[i], down_tile(i, s, nr), 0)),
            pl.BlockSpec((1, 1, D_MODEL), lambda i, s, be, nv, nr: (be[i], 0, 0)),
            pl.BlockSpec((2 * tf, tf), lambda i, s, be, nv, nr: (0, 0)),
        ],
        out_specs=pl.BlockSpec((bm, D_MODEL), lambda i, s, be, nv, nr: (i, 0)),
        scratch_shapes=[pltpu.VMEM((bm, tf), BF16)],
    )
    return pl.pallas_call(
        _expert_kernel,
        grid_spec=grid_spec,
        out_shape=jax.ShapeDtypeStruct((n_rows, D_MODEL), F32),
        compiler_params=_cparams(("arbitrary", "arbitrary")),
        name="moe_experts",
    )(blk_expert, blk_valid, n_real, x_sorted, w_up, b_up3, w_down, b_down3, sel)


def _combine_kernel(dest_ref, dest_next_ref, rows_ref, h_ref, gate_ref, g_ref, op_ref, os_ref, buf, sem,
                    *, n_p_tiles):
    tm = h_ref.shape[0]
    i = pl.program_id(0)
    last = pl.num_programs(0) - 1
    slot = i % 2

    def issue(dest, dst_slot):
        def body(t2, carry):
            for u in range(2):
                t = t2 * 2 + u
                for j in range(TOP_K):
                    pltpu.make_async_copy(rows_ref.at[pl.ds(dest[t * TOP_K + j], 1)],
                                          buf.at[dst_slot, pl.ds(j * tm + t, 1)], sem.at[dst_slot]).start()
            return carry

        lax.fori_loop(0, tm // 2, body, 0)

    @pl.when(i == 0)
    def _():
        issue(dest_ref, 0)

    @pl.when(i < last)
    def _():
        issue(dest_next_ref, 1 - slot)

    pltpu.make_async_copy(rows_ref.at[pl.ds(0, TOP_K * tm)], buf.at[slot], sem.at[slot]).wait()
    gates = gate_ref[...]
    y = h_ref[...]
    for j in range(TOP_K):
        y = y + buf[slot, j * tm:(j + 1) * tm, :] * gates[:, j:j + 1]
    out = _rms(y, g_ref[...])
    is_prompt = pl.program_id(0) < n_p_tiles

    @pl.when(is_prompt)
    def _():
        op_ref[...] = out

    @pl.when(jnp.logical_not(is_prompt))
    def _():
        os_ref[...] = out


def _combine(dest, out_rows, h, gates, g, n_p, tm):
    t_tok = h.shape[0]
    n_p_tiles = n_p // tm
    n_steps = t_tok // tm
    return pl.pallas_call(
        functools.partial(_combine_kernel, n_p_tiles=n_p_tiles),
        grid=(n_steps,),
        in_specs=[
            pl.BlockSpec((tm * TOP_K,), lambda i: (i,), memory_space=pltpu.SMEM),
            pl.BlockSpec((tm * TOP_K,), lambda i: (jnp.minimum(i + 1, n_steps - 1),), memory_space=pltpu.SMEM),
            pl.BlockSpec(memory_space=pl.ANY),
            pl.BlockSpec((tm, D_MODEL), lambda i: (i, 0)),
            pl.BlockSpec((tm, LANES), lambda i: (i, 0)),
            pl.BlockSpec((1, D_MODEL), lambda i: (0, 0)),
        ],
        out_specs=[pl.BlockSpec((tm, D_MODEL), lambda i: (jnp.minimum(i, n_p_tiles - 1), 0)),
                   pl.BlockSpec((tm, D_MODEL), lambda i: (jnp.maximum(i - n_p_tiles, 0), 0))],
        out_shape=[jax.ShapeDtypeStruct((n_p, D_MODEL), F32),
                   jax.ShapeDtypeStruct((t_tok - n_p, D_MODEL), F32)],
        scratch_shapes=[pltpu.VMEM((2, TOP_K * tm, D_MODEL), F32), pltpu.SemaphoreType.DMA((2,))],
        compiler_params=_cparams(("arbitrary",)),
        name="moe_combine",
    )(dest, dest, out_rows, h, gates, g)


def _q_head_order():
    order = []
    for pair in range(2):
        for j in range(4):
            order += [8 * pair + j, 8 * pair + 4 + j]
    return order


def _q_col_perm():
    cols = []
    for h in _q_head_order():
        cols += list(range(h * HEAD_DIM, (h + 1) * HEAD_DIM))
    return jnp.asarray(cols, jnp.int32)


def _pad_rwkv_cols(m):
    def padw(a, w):
        return jnp.pad(a, [(0, 0)] * (a.ndim - 1) + [(0, w - a.shape[-1])])
    return jnp.concatenate([
        m[..., :XW_OFF],
        padw(m[..., XW_OFF:XW_OFF + DECAY_LORA], LANES),
        padw(m[..., XW_OFF + DECAY_LORA:XW_OFF + DECAY_LORA + ICLR_LORA], LANES),
        padw(m[..., XW_OFF + DECAY_LORA + ICLR_LORA:], XG_PAD),
    ], axis=-1)


def _unpad_rwkv_cols(m):
    return jnp.concatenate([
        m[..., :XW_OFF + DECAY_LORA],
        m[..., XA_OFF:XA_OFF + ICLR_LORA],
        m[..., XG_OFF:XG_OFF + GATE_LORA],
    ], axis=-1)


def _pad_rows(m, rows):
    return jnp.pad(m, ((0, rows - m.shape[0]), (0, 0)))


def _rope_tables(pos):
    half = HEAD_DIM // 2
    inv_freq = ROPE_THETA ** (-jnp.arange(half, dtype=F32) / half)
    ang = pos.astype(F32)[:, None] * inv_freq[None, :]
    cos = jnp.cos(ang)
    sin = jnp.sin(ang)
    cos_t = jnp.tile(jnp.concatenate([cos, cos], axis=-1), (1, LANES // HEAD_DIM))
    sin_t = jnp.tile(jnp.concatenate([-sin, sin], axis=-1), (1, LANES // HEAD_DIM))
    return cos_t, sin_t


def _route(top_idx, bm):
    n_tok = top_idx.shape[0]
    e_flat = top_idx.reshape(-1)
    onehot = (e_flat[:, None] == jnp.arange(N_EXPERTS, dtype=jnp.int32)[None, :]).astype(jnp.int32)
    csum = jnp.cumsum(onehot, axis=0)
    rank = jnp.sum(csum * onehot, axis=1) - 1
    counts = csum[-1]
    padded = ((counts + bm - 1) // bm) * bm
    p_end = jnp.cumsum(padded)
    p_start = p_end - padded
    dest = (p_start[e_flat] + rank).astype(jnp.int32)
    n_assign = n_tok * TOP_K
    n_blocks = -(-n_assign // bm) + N_EXPERTS
    n_rows = n_blocks * bm
    tok_flat = jnp.repeat(jnp.arange(n_tok, dtype=jnp.int32), TOP_K)
    row_tok = jnp.zeros((n_rows,), jnp.int32).at[dest].set(tok_flat, unique_indices=True)
    blk_start = jnp.arange(n_blocks, dtype=jnp.int32) * bm
    blk_expert = jnp.minimum(jnp.searchsorted(p_end, blk_start, side="right"), N_EXPERTS - 1).astype(jnp.int32)
    blk_valid = jnp.clip(counts[blk_expert] - (blk_start - p_start[blk_expert]), 0, bm).astype(jnp.int32)
    sub_start = jnp.arange(n_rows // MOE_SUB, dtype=jnp.int32) * MOE_SUB
    sub_expert = blk_expert[sub_start // bm]
    sub_valid = jnp.clip(counts[sub_expert] - (sub_start - p_start[sub_expert]), 0, MOE_SUB).astype(jnp.int32)
    n_real = (p_end[-1:] // bm).astype(jnp.int32)
    return dest, row_tok, blk_expert, blk_valid, sub_valid, n_real


def _pick(n, prefs):
    for p in prefs:
        if n % p == 0:
            return p
    return n


def kernel(x_prompt, x_sample, cache_k, cache_v, state_wkv, state_shift, attn_norm_g, w_in, attn_sinks, mu_shift, decay_w0, decay_lora_up, iclr_a0, iclr_lora_up, gate_lora_up, k_k, k_a, r_k, lnx_w, lnx_b, w_out, ffn_norm_g, router_w, router_b, w_up, b_up, w_down, b_down, final_norm_g):
    depth = w_in.shape[0]
    assert depth == 1
    batch, seq, _ = x_prompt.shape
    dec_b, dec_t, _ = x_sample.shape
    n_p = batch * seq
    n_s = dec_b * dec_t
    l = 0

    qperm = _q_col_perm()
    w_in_l = w_in[l]
    w_attn = jnp.concatenate([w_in_l[:, :ATTN_WIDTH][:, qperm], w_in_l[:, ATTN_WIDTH:ATTN_WIDTH + 2 * KV_WIDTH]],
                             axis=1).astype(BF16)
    w_rwkv = _pad_rwkv_cols(w_in_l[:, ATTN_WIDTH + 2 * KV_WIDTH:]).astype(BF16)
    w_oa = w_out[l][:ATTN_WIDTH][qperm].astype(BF16)
    w_or = w_out[l][ATTN_WIDTH:].astype(BF16)
    g_attn = attn_norm_g[l][None, :]
    g_ffn = ffn_norm_g[l][None, :]
    sinks_true = attn_sinks[l].astype(F32)
    seg = (jnp.arange(GROUP)[:, None] // HEAD_DIM == jnp.arange(GROUP)[None, :] // HEAD_DIM).astype(BF16)
    prm = dict(
        mu=_pad_rwkv_cols(mu_shift[l][None, :]),
        w0=decay_w0[l][None, :], wd=_pad_rows(decay_lora_up[l], LANES).astype(BF16),
        a0=iclr_a0[l][None, :], wa=_pad_rows(iclr_lora_up[l], LANES).astype(BF16),
        wg=_pad_rows(gate_lora_up[l], XG_PAD).astype(BF16),
        k_k=k_k[l][None, :], k_a=k_a[l][None, :], r_k=r_k[l].reshape(1, RWKV_WIDTH),
        lnx_w=lnx_w[l][None, :], lnx_b=lnx_b[l][None, :], seg=seg)
    rw_pad = jnp.pad(router_w[l], ((0, 0), (0, LANES - N_EXPERTS)))
    rw_hi = rw_pad.astype(BF16)
    rw_lo = (rw_pad - rw_hi.astype(F32)).astype(BF16)
    rw_split = jnp.stack([rw_hi, rw_lo])
    rb_pad = jnp.concatenate([router_b[l], jnp.full((LANES - N_EXPERTS,), NEG_BIG, F32)])[None, :]
    b_up3 = b_up[l][:, None, :]
    b_dn = b_down[l][:, None, :]
    tf = 256
    sel = (jnp.arange(2 * tf)[:, None] == 2 * jnp.arange(tf)[None, :]).astype(BF16)

    tm_p = _pick(seq, (512, 256, 128))
    cos_p, sin_p = _rope_tables(jnp.arange(seq, dtype=jnp.int32))
    q_p, k_p, v_p = _attn_proj(x_prompt.reshape(n_p, D_MODEL), g_attn, w_attn, cos_p, sin_p, tm_p)
    tm_s = _pick(n_s, (512, 256, 128, 64, 32, 16, 8))
    pos_s = PAST_LEN + (jnp.arange(n_s, dtype=jnp.int32) % dec_t)
    cos_s, sin_s = _rope_tables(pos_s)
    q_s, k_s, v_s = _attn_proj(x_sample.reshape(n_s, D_MODEL), g_attn, w_attn, cos_s, sin_s, tm_s)
    tn = 512
    pr_p = _rwkv_proj(x_prompt.reshape(n_p, D_MODEL), g_attn, w_rwkv, _pick(n_p, (1024, 512, 256, 128)), tn)
    pr_s = _rwkv_proj(x_sample.reshape(n_s, D_MODEL), g_attn, w_rwkv, tm_s, tn)

    oa_p = _attn_prompt(q_p, k_p, v_p, sinks_true, batch, seq)
    wb = cache_k.shape[2]
    sink_rows = jnp.stack([
        jnp.repeat(jnp.stack([sinks_true[8 * pair + 4 * e + j] for e in range(2) for j in range(4)]), dec_t)
        for pair in range(2)])[:, :, None]
    oa_s, nk_s, nv_s = _attn_sample(
        q_s.reshape(dec_b, dec_t, ATTN_WIDTH), k_s.reshape(dec_b, dec_t, KV_WIDTH),
        v_s.reshape(dec_b, dec_t, KV_WIDTH), cache_k[l].reshape(dec_b, wb, KV_WIDTH),
        cache_v[l].reshape(dec_b, wb, KV_WIDTH), sink_rows, _pick(dec_b, (8, 4, 2, 1)))

    zero_state = jnp.zeros((batch, RWKV_WIDTH // HEAD_DIM, HEAD_DIM, HEAD_DIM), F32)
    zero_shift = jnp.zeros((batch, 1, RWKV_PROJ_PAD), F32)
    or_p, st_p, sh_p = _rwkv_mix4(pr_p, zero_state, zero_shift, prm, 1, _pick(seq, (64, 32, 16, 8)),
                                  npar=1)
    seq_per_step = _pick(dec_b, (64 // dec_t, 1)) if 64 % dec_t == 0 else 1
    or_s, st_s, sh_s = _rwkv_mix4(pr_s, state_wkv[l], _pad_rwkv_cols(state_shift[l])[:, None, :], prm,
                                  seq_per_step, dec_t)

    n_all = n_p + n_s
    tm_o = _pick(math.gcd(n_p, n_s), (512, 256, 128, 64, 32, 16, 8))
    h, hn, idx_pad, gate_pad = _out_router(
        x_prompt.reshape(n_p, D_MODEL), x_sample.reshape(n_s, D_MODEL), oa_p, oa_s.reshape(n_s, ATTN_WIDTH),
        or_p, or_s, w_oa, w_or, g_ffn, rw_split, rb_pad, tm_o)

    bm = 1024 if n_all * TOP_K >= 1024 * N_EXPERTS else MOE_SUB
    dest, row_tok, blk_expert, blk_valid, sub_valid, n_real = _route(idx_pad[:, :TOP_K], bm)
    x_sorted = _gather_rows(sub_valid, row_tok, hn, MOE_SUB)
    out_rows = _experts(blk_expert, blk_valid, n_real, x_sorted, w_up[l], b_up3, w_down[l], b_dn, sel, bm, tf)
    y_p, y_s = _combine(dest, out_rows, h, gate_pad, final_norm_g[None, :], n_p,
                        _pick(math.gcd(n_p, n_s), (128, 64, 32, 16, 8)))

    y_prompt = y_p.reshape(batch, seq, D_MODEL)
    y_sample = y_s.reshape(dec_b, dec_t, D_MODEL)
    kp = k_p.reshape(batch, seq, 4, HEAD_DIM)[:, -WINDOW:][None]
    vp = v_p.reshape(batch, seq, 4, HEAD_DIM)[:, -WINDOW:][None]
    wp = st_p[None]
    sp = _unpad_rwkv_cols(sh_p[:, 0, :])[None]
    ks = nk_s.reshape(dec_b, wb, 4, HEAD_DIM)[None]
    vs = nv_s.reshape(dec_b, wb, 4, HEAD_DIM)[None]
    ws = st_s[None]
    ss = _unpad_rwkv_cols(sh_s[:, 0, :])[None]
    return (y_prompt, y_sample, kp, vp, wp, sp, ks, vs, ws, ss)
```

```python
import functools
import math

import jax
import jax.numpy as jnp
from jax import lax
from jax.experimental import pallas as pl
from jax.experimental.pallas import tpu as pltpu

F32 = jnp.float32
BF16 = jnp.bfloat16

D_MODEL = 2048
HEAD_DIM = 64
LANES = 128
ATTN_WIDTH = 1024
KV_WIDTH = 256
ATTN_HEADS = 16
RWKV_WIDTH = 1024
DECAY_LORA = 64
ICLR_LORA = 64
GATE_LORA = 160
RWKV_PROJ = 3 * RWKV_WIDTH + DECAY_LORA + ICLR_LORA + GATE_LORA
XW_OFF = 3 * RWKV_WIDTH
XA_OFF = XW_OFF + LANES
XG_OFF = XA_OFF + LANES
XG_PAD = 2 * LANES
RWKV_PROJ_PAD = XG_OFF + XG_PAD
WINDOW = 128
ROPE_THETA = 10000.0
PAST_LEN = 8192
N_EXPERTS = 32
TOP_K = 4
D_FF = 2048
SWIGLU_ALPHA = 1.702
SWIGLU_LIMIT = 7.0
NORM_EPS = 1e-5
LNX_EPS = 64e-5
NEG_BIG = -1e30
VMEM_LIMIT = 56 * 1024 * 1024


def _cparams(sem):
    return pltpu.CompilerParams(dimension_semantics=sem, vmem_limit_bytes=VMEM_LIMIT)


_NN = (((1,), (0,)), ((), ()))
_NT = (((1,), (1,)), ((), ()))
_TN = (((0,), (0,)), ((), ()))


def _mm(a, b, dims=_NN):
    return lax.dot_general(a, b, dims, preferred_element_type=F32)


def _split2(x):
    hi = x.astype(BF16)
    lo = (x - hi.astype(F32)).astype(BF16)
    return hi, lo


def _split3(x):
    hi = x.astype(BF16)
    r1 = x - hi.astype(F32)
    mid = r1.astype(BF16)
    lo = (r1 - mid.astype(F32)).astype(BF16)
    return hi, mid, lo


def _mm_exact_b(a, b_bf16, dims=_NN):
    h, m, l = _split3(a)
    return _mm(h, b_bf16, dims) + (_mm(m, b_bf16, dims) + _mm(l, b_bf16, dims))


def _mm_exact_a(a_bf16, b, dims=_NN):
    h, m, l = _split3(b)
    return _mm(a_bf16, h, dims) + (_mm(a_bf16, m, dims) + _mm(a_bf16, l, dims))


def _rms(x, g):
    ms = jnp.mean(x * x, axis=-1, keepdims=True)
    return (x * lax.rsqrt(ms + NORM_EPS)) * g


def _attn_proj_kernel(x_ref, g_ref, w_ref, cos_ref, sin_ref, q_ref, k_ref, v_ref):
    xn = _rms(x_ref[...], g_ref[...]).astype(BF16)
    a = _mm(xn, w_ref[...])
    cos = cos_ref[...]
    sin = sin_ref[...]
    lane = lax.broadcasted_iota(jnp.int32, cos.shape, 1)
    first_half = (lane % HEAD_DIM) < (HEAD_DIM // 2)
    n_rot = (ATTN_WIDTH + KV_WIDTH) // LANES
    for j in range(n_rot):
        t = a[:, j * LANES:(j + 1) * LANES]
        swapped = jnp.where(first_half, pltpu.roll(t, LANES - HEAD_DIM // 2, 1), pltpu.roll(t, HEAD_DIM // 2, 1))
        rot = t * cos + swapped * sin
        if j < ATTN_WIDTH // LANES:
            q_ref[:, j * LANES:(j + 1) * LANES] = (rot * (HEAD_DIM ** -0.5)).astype(BF16)
        else:
            jj = j - ATTN_WIDTH // LANES
            k_ref[:, jj * LANES:(jj + 1) * LANES] = rot
    v_ref[...] = a[:, ATTN_WIDTH + KV_WIDTH:]


def _attn_proj(x, g, w_attn, cos_t, sin_t, tm):
    t_tok = x.shape[0]
    n_pos_blocks = cos_t.shape[0] // tm
    return pl.pallas_call(
        _attn_proj_kernel,
        grid=(t_tok // tm,),
        in_specs=[
            pl.BlockSpec((tm, D_MODEL), lambda i: (i, 0)),
            pl.BlockSpec((1, D_MODEL), lambda i: (0, 0)),
            pl.BlockSpec((D_MODEL, ATTN_WIDTH + 2 * KV_WIDTH), lambda i: (0, 0)),
            pl.BlockSpec((tm, LANES), lambda i: (i % n_pos_blocks, 0)),
            pl.BlockSpec((tm, LANES), lambda i: (i % n_pos_blocks, 0)),
        ],
        out_specs=[
            pl.BlockSpec((tm, ATTN_WIDTH), lambda i: (i, 0)),
            pl.BlockSpec((tm, KV_WIDTH), lambda i: (i, 0)),
            pl.BlockSpec((tm, KV_WIDTH), lambda i: (i, 0)),
        ],
        out_shape=[
            jax.ShapeDtypeStruct((t_tok, ATTN_WIDTH), BF16),
            jax.ShapeDtypeStruct((t_tok, KV_WIDTH), F32),
            jax.ShapeDtypeStruct((t_tok, KV_WIDTH), F32),
        ],
        compiler_params=_cparams(("parallel",)),
        name="attn_proj",
    )(x, g, w_attn, cos_t, sin_t)


def _rwkv_proj_kernel(x_ref, g_ref, w_ref, p_ref, xn_ref):
    @pl.when(pl.program_id(1) == 0)
    def _():
        xn_ref[...] = _rms(x_ref[...], g_ref[...]).astype(BF16)

    p_ref[...] = _mm(xn_ref[...], w_ref[...])


def _rwkv_proj(x, g, w_rwkv, tm, tn):
    t_tok = x.shape[0]
    return pl.pallas_call(
        _rwkv_proj_kernel,
        grid=(t_tok // tm, RWKV_PROJ_PAD // tn),
        in_specs=[
            pl.BlockSpec((tm, D_MODEL), lambda i, n: (i, 0)),
            pl.BlockSpec((1, D_MODEL), lambda i, n: (0, 0)),
            pl.BlockSpec((D_MODEL, tn), lambda i, n: (0, n)),
        ],
        out_specs=pl.BlockSpec((tm, tn), lambda i, n: (i, n)),
        out_shape=jax.ShapeDtypeStruct((t_tok, RWKV_PROJ_PAD), F32),
        scratch_shapes=[pltpu.VMEM((tm, D_MODEL), BF16)],
        compiler_params=_cparams(("parallel", "arbitrary")),
        name="rwkv_proj",
    )(x, g, w_rwkv)


def _softmax_pv(s, mask, sink, vt):
    s = jnp.where(mask, s, NEG_BIG)
    m = jnp.maximum(jnp.max(s, axis=-1, keepdims=True), sink)
    p = jnp.exp(s - m)
    denom = jnp.sum(p, axis=-1, keepdims=True) + jnp.exp(sink - m)
    return _mm(p.astype(BF16), vt) * (1.0 / denom)


def _attn_prompt_kernel(q_ref, kp_ref, kc_ref, vp_ref, vc_ref, sink_ref, o_ref):
    i = pl.program_id(1)
    blk = q_ref.shape[0]
    kk = jnp.concatenate([kp_ref[...], kc_ref[...]], axis=0).astype(BF16)
    vv = jnp.concatenate([vp_ref[...], vc_ref[...]], axis=0).astype(BF16)
    a = lax.broadcasted_iota(jnp.int32, (blk, 2 * blk), 0)
    c = lax.broadcasted_iota(jnp.int32, (blk, 2 * blk), 1)
    mask = (c > a) & (c <= a + blk) & ((c >= blk) | (i > 0))
    lane_lo = lax.broadcasted_iota(jnp.int32, (blk, LANES), 1) < HEAD_DIM
    n_pairs = KV_WIDTH // LANES
    kts = [kk[:, pair * LANES:(pair + 1) * LANES] for pair in range(n_pairs)]
    vts = [vv[:, pair * LANES:(pair + 1) * LANES] for pair in range(n_pairs)]
    heads = [(pair, j, e) for pair in range(n_pairs) for j in range(4) for e in range(2)]
    scores = []
    for pair, j, e in heads:
        tile = pair * 4 + j
        qt = q_ref[:, tile * LANES:(tile + 1) * LANES]
        qm = jnp.where(lane_lo if e == 0 else jnp.logical_not(lane_lo), qt, jnp.zeros_like(qt))
        scores.append(_mm(qm, kts[pair], _NT))
    probs, scales = [], []
    for s, (pair, j, e) in zip(scores, heads):
        sink = sink_ref[8 * pair + 4 * e + j]
        s = jnp.where(mask, s, NEG_BIG)
        m = jnp.maximum(jnp.max(s, axis=-1, keepdims=True), sink)
        p = jnp.exp(s - m)
        probs.append(p.astype(BF16))
        scales.append(1.0 / (jnp.sum(p, axis=-1, keepdims=True) + jnp.exp(sink - m)))
    outs = [_mm(p, vts[pair]) * sc for p, sc, (pair, j, e) in zip(probs, scales, heads)]
    for tile in range(ATTN_WIDTH // LANES):
        o_ref[:, tile * LANES:(tile + 1) * LANES] = jnp.where(lane_lo, outs[2 * tile], outs[2 * tile + 1]).astype(BF16)


def _attn_prompt(q, k, v, sinks, batch, seq):
    blk = WINDOW
    nb = seq // blk
    cur = lambda b, i: (b * nb + i, 0)
    prev = lambda b, i: (b * nb + jnp.maximum(i - 1, 0), 0)
    return pl.pallas_call(
        _attn_prompt_kernel,
        grid=(batch, nb),
        in_specs=[
            pl.BlockSpec((blk, ATTN_WIDTH), cur),
            pl.BlockSpec((blk, KV_WIDTH), prev),
            pl.BlockSpec((blk, KV_WIDTH), cur),
            pl.BlockSpec((blk, KV_WIDTH), prev),
            pl.BlockSpec((blk, KV_WIDTH), cur),
            pl.BlockSpec(memory_space=pltpu.SMEM),
        ],
        out_specs=pl.BlockSpec((blk, ATTN_WIDTH), cur),
        out_shape=jax.ShapeDtypeStruct((batch * seq, ATTN_WIDTH), BF16),
        compiler_params=_cparams(("parallel", "arbitrary")),
        name="attn_prompt",
    )(q, k, k, v, v, sinks)


def _attn_sample_kernel(q_ref, kn_ref, vn_ref, ck_ref, cv_ref, sink_ref, o_ref, nk_ref, nv_ref):
    bb, t_len, _ = q_ref.shape
    wb = ck_ref.shape[1]
    n_keys = wb + t_len
    rows = 8 * t_len
    a = lax.broadcasted_iota(jnp.int32, (rows, n_keys), 0) % t_len
    c = lax.broadcasted_iota(jnp.int32, (rows, n_keys), 1)
    mask = ((c < wb) & (c > a + (wb - WINDOW))) | ((c >= wb) & (c - wb <= a))
    lane_lo = lax.broadcasted_iota(jnp.int32, (t_len, LANES), 1) < HEAD_DIM

    def body(b, carry):
        ck = ck_ref[b]
        cv = cv_ref[b]
        kn = kn_ref[b]
        vn = vn_ref[b]
        nk_ref[b, 0:wb - t_len, :] = ck[t_len:, :]
        nk_ref[b, wb - t_len:wb, :] = kn
        nv_ref[b, 0:wb - t_len, :] = cv[t_len:, :]
        nv_ref[b, wb - t_len:wb, :] = vn
        k_all = jnp.concatenate([ck, kn], axis=0).astype(BF16)
        v_all = jnp.concatenate([cv, vn], axis=0).astype(BF16)
        qb = q_ref[b]
        pairs = range(KV_WIDTH // LANES)
        scores = []
        for pair in pairs:
            stack = []
            for e in range(2):
                for j in range(4):
                    tile = pair * 4 + j
                    qt = qb[:, tile * LANES:(tile + 1) * LANES]
                    stack.append(jnp.where(lane_lo if e == 0 else jnp.logical_not(lane_lo), qt, jnp.zeros_like(qt)))
            qs = jnp.concatenate(stack, axis=0)
            scores.append(_mm(qs, k_all[:, pair * LANES:(pair + 1) * LANES], _NT))
        outs = [_softmax_pv(scores[pair], mask, sink_ref[pair], v_all[:, pair * LANES:(pair + 1) * LANES])
                for pair in pairs]
        for pair in pairs:
            o = outs[pair]
            for j in range(4):
                tile = pair * 4 + j
                lo = o[j * t_len:(j + 1) * t_len, :]
                hi = o[(4 + j) * t_len:(5 + j) * t_len, :]
                o_ref[b, :, tile * LANES:(tile + 1) * LANES] = jnp.where(lane_lo, lo, hi).astype(BF16)
        return carry

    lax.fori_loop(0, bb, body, 0)


def _attn_sample(q3, k3, v3, ck, cv, sink_rows, bb):
    dec_b, t_len, _ = q3.shape
    wb = ck.shape[1]
    blk3 = lambda w: pl.BlockSpec((bb, t_len, w), lambda i: (i, 0, 0))
    cache = pl.BlockSpec((bb, wb, KV_WIDTH), lambda i: (i, 0, 0))
    return pl.pallas_call(
        _attn_sample_kernel,
        grid=(dec_b // bb,),
        in_specs=[blk3(ATTN_WIDTH), blk3(KV_WIDTH), blk3(KV_WIDTH), cache, cache,
                  pl.BlockSpec((2, 8 * t_len, 1), lambda i: (0, 0, 0))],
        out_specs=[blk3(ATTN_WIDTH), cache, cache],
        out_shape=[
            jax.ShapeDtypeStruct((dec_b, t_len, ATTN_WIDTH), BF16),
            jax.ShapeDtypeStruct(ck.shape, F32),
            jax.ShapeDtypeStruct(cv.shape, F32),
        ],
        compiler_params=_cparams(("parallel",)),
        name="attn_sample",
    )(q3, k3, v3, ck, cv, sink_rows)


GROUP = 4 * HEAD_DIM


def _bf(x):
    return x.astype(BF16)


def _seg_sum4(x, seg_mat):
    outs = []
    for t in range(x.shape[1] // GROUP):
        outs.append(_mm_exact_b(x[:, t * GROUP:(t + 1) * GROUP], seg_mat))
    return jnp.concatenate(outs, axis=1)


def _stack_heads(x, head_masks):
    zero = jnp.zeros_like(x)
    return jnp.concatenate([jnp.where(m, x, zero) for m in head_masks], axis=0)


def _sum_blocks(x, r):
    return (x[0:r] + x[r:2 * r]) + (x[2 * r:3 * r] + x[3 * r:4 * r])


def _place_heads(blocks):
    rows = []
    for h, blk in enumerate(blocks):
        rows.append(jnp.concatenate([blk if j == h else jnp.zeros_like(blk) for j in range(len(blocks))], axis=1))
    return jnp.concatenate(rows, axis=0)


def _rwkv4_kernel(p_ref, s_in_ref, sh_ref, mu_ref, w0_ref, wd_ref, a0_ref, wa_ref, wg_ref, kk_ref, ka_ref,
                  rk_ref, lnw_ref, lnb_ref, seg_ref, o_ref, s_out_ref, sh_out_ref, state_scr, prev_scr,
                  *, nseq, tlen, npar):
    rows = nseq * tlen
    heads = GROUP // HEAD_DIM
    log_t = tlen.bit_length() - 1
    log_r = rows.bit_length() - 1
    log_h = HEAD_DIM.bit_length() - 1

    ri = lax.broadcasted_iota(jnp.int32, (rows, rows), 0)
    cj = lax.broadcasted_iota(jnp.int32, (rows, rows), 1)
    bri = lax.broadcasted_iota(jnp.int32, (heads * rows, heads * rows), 0)
    bcj = lax.broadcasted_iota(jnp.int32, (heads * rows, heads * rows), 1)
    same = ((bri >> log_r) == (bcj >> log_r)) & ((bri >> log_t) == (bcj >> log_t))
    lane = lax.broadcasted_iota(jnp.int32, (rows, GROUP), 1)
    sr = lax.broadcasted_iota(jnp.int32, (GROUP, GROUP), 0)
    sc = lax.broadcasted_iota(jnp.int32, (GROUP, GROUP), 1)
    masks = dict(
        tri=jnp.where(((ri >> log_t) == (cj >> log_t)) & (ri >= cj), 1.0, 0.0).astype(BF16),
        strict_bd=same & (bri > bcj),
        incl_bd=same & (bri >= bcj),
        eye_bd=jnp.where(bri == bcj, 1.0, 0.0).astype(F32),
        head_masks=[(lane >> log_h) == h for h in range(heads)],
        state_bd=(sr >> log_h) == (sc >> log_h),
    )
    refs = (p_ref, s_in_ref, sh_ref, mu_ref, w0_ref, wd_ref, a0_ref, wa_ref, wg_ref, kk_ref, ka_ref, rk_ref,
            lnw_ref, lnb_ref, seg_ref, o_ref, s_out_ref, sh_out_ref, state_scr, prev_scr)
    for j in range(npar):
        _rwkv4_block(j, refs, masks, nseq=nseq, tlen=tlen)


def _rwkv4_block(j, refs, masks, *, nseq, tlen):
    (p_ref, s_in_ref, sh_ref, mu_ref, w0_ref, wd_ref, a0_ref, wa_ref, wg_ref, kk_ref, ka_ref, rk_ref,
     lnw_ref, lnb_ref, seg_ref, o_ref, s_out_ref, sh_out_ref, state_scr, prev_scr) = refs
    ci = pl.program_id(1)
    rows = nseq * tlen
    n_groups = RWKV_WIDTH // GROUP
    heads = GROUP // HEAD_DIM
    log_t = tlen.bit_length() - 1
    carry = nseq == 1

    p = p_ref[j] if carry else p_ref[...]
    rowi = lax.broadcasted_iota(jnp.int32, (rows, 1), 0)
    rolled = pltpu.roll(p, 1, 0)
    if carry:
        @pl.when(ci == 0)
        def _():
            prev_scr[j] = sh_ref[j]
            for g in range(n_groups):
                state_scr[j * n_groups + g] = _place_heads([s_in_ref[j, heads * g + h] for h in range(heads)])

        p_prev = jnp.where(rowi == 0, prev_scr[j], rolled)
        prev_scr[j] = p[rows - 1:rows, :]
    else:
        p_prev = rolled
        for s in range(nseq):
            p_prev = jnp.where(rowi == s * tlen, sh_ref[s], p_prev)
            sh_out_ref[s] = p[(s + 1) * tlen - 1:(s + 1) * tlen, :]

    xs = p + (p_prev - p) * mu_ref[...]
    r = xs[:, 0:RWKV_WIDTH]
    k = xs[:, RWKV_WIDTH:2 * RWKV_WIDTH]
    v = xs[:, 2 * RWKV_WIDTH:3 * RWKV_WIDTH]
    xw = xs[:, XW_OFF:XW_OFF + LANES]
    xa = xs[:, XA_OFF:XA_OFF + LANES]
    xg = xs[:, XG_OFF:XG_OFF + XG_PAD]
    z = w0_ref[...] + _mm(_bf(jnp.tanh(xw)), wd_ref[...])
    w_log = -jax.nn.softplus(-z) - 0.5
    lw = -jnp.exp(w_log)
    a = jax.nn.sigmoid(a0_ref[...] + _mm(_bf(xa), wa_ref[...]))
    gate = _mm(_bf(jax.nn.sigmoid(xg)), wg_ref[...])
    seg = seg_ref[...]
    kk = k * kk_ref[...]
    kap = kk / jnp.maximum(jnp.sqrt(_seg_sum4(kk * kk, seg)), 1e-12)
    k2 = k * (1.0 + (a - 1.0) * ka_ref[...])
    b = kap * a
    bonus = _seg_sum4(r * k2 * rk_ref[...], seg) * v

    cum = _mm_exact_a(masks["tri"], lw)
    eg = jnp.exp(cum)
    kap_t = kap * jnp.exp(cum - lw)
    r_t = r * eg
    einv = jnp.exp(-cum)
    b_t = b * einv
    k_t = k2 * einv

    strict_bd, incl_bd, eye_bd = masks["strict_bd"], masks["incl_bd"], masks["eye_bd"]
    head_masks, state_bd = masks["head_masks"], masks["state_bd"]
    zero_bd = jnp.zeros((heads * rows, heads * rows), F32)
    zero_st = jnp.zeros((GROUP, GROUP), F32)

    groups = range(n_groups)
    hr = heads * rows
    sls = [slice(g * GROUP, (g + 1) * GROUP) for g in groups]
    kap_g = [kap_t[:, sl] for sl in sls]
    r_g = [r_t[:, sl] for sl in sls]
    b_g = [b_t[:, sl] for sl in sls]
    k_g = [k_t[:, sl] for sl in sls]
    v_g = [v[:, sl] for sl in sls]
    lkr = [jnp.concatenate([_stack_heads(_bf(kap_g[g]), head_masks), _stack_heads(_bf(r_g[g]), head_masks)], axis=0)
           for g in groups]
    gb = [_mm(lkr[g], _stack_heads(_bf(b_g[g]), head_masks), _NT) for g in groups]
    gk = [_mm(lkr[g], _stack_heads(_bf(k_g[g]), head_masks), _NT) for g in groups]
    n = [-jnp.where(strict_bd, gb[g][:hr], zero_bd) for g in groups]
    t_inv = [eye_bd + n[g] for g in groups]
    for _ in range(max(log_t - 1, 0)):
        nb = [_bf(n[g]) for g in groups]
        n = [_mm(nb[g], nb[g]) for g in groups]
        t_inv = [t_inv[g] + _mm(_bf(t_inv[g]), _bf(n[g])) for g in groups]
    a_k = [_bf(jnp.where(strict_bd, gk[g][:hr], zero_bd)) for g in groups]
    r_b = [_bf(jnp.where(incl_bd, gb[g][hr:], zero_bd)) for g in groups]
    r_k = [_bf(jnp.where(incl_bd, gk[g][hr:], zero_bd)) for g in groups]
    v_stack = [_stack_heads(_bf(v_g[g]), head_masks) for g in groups]

    if carry:
        s0 = [state_scr[j * n_groups + g] for g in groups]
        sk = [_mm(_bf(jnp.concatenate([kap_g[g], r_g[g]], axis=0)), _bf(s0[g]), _NT) for g in groups]
        k_s = [sk[g][:rows] for g in groups]
        r_s = [sk[g][rows:] for g in groups]
    else:
        s0, k_s, r_s = [], [], []
        for g in groups:
            s0_list, ks_list, rs_list = [], [], []
            for s in range(nseq):
                rs = slice(s * tlen, (s + 1) * tlen)
                st = _place_heads([s_in_ref[s, heads * g + h] for h in range(heads)])
                sk = _mm(_bf(jnp.concatenate([kap_g[g][rs], r_g[g][rs]], axis=0)), _bf(st), _NT)
                s0_list.append(st)
                ks_list.append(sk[:tlen])
                rs_list.append(sk[tlen:])
            s0.append(s0_list)
            k_s.append(jnp.concatenate(ks_list, axis=0))
            r_s.append(jnp.concatenate(rs_list, axis=0))

    w = [_sum_blocks(_mm(a_k[g], v_stack[g]), rows) for g in groups]
    rhs = [-(k_s[g] + w[g]) for g in groups]
    u_stack = [_mm(_bf(t_inv[g]), _stack_heads(_bf(rhs[g]), head_masks)) for g in groups]
    u = [_sum_blocks(u_stack[g], rows) for g in groups]
    y_stack = [_mm(r_b[g], _bf(u_stack[g])) + _mm(r_k[g], v_stack[g]) for g in groups]
    ys = [r_s[g] + _sum_blocks(y_stack[g], rows) for g in groups]

    for g in groups:
        if carry:
            d = _mm(_bf(jnp.concatenate([u[g], v_g[g]], axis=0)), _bf(jnp.concatenate([b_g[g], k_g[g]], axis=0)), _TN)
            state_scr[j * n_groups + g] = (s0[g] + jnp.where(state_bd, d, zero_st)) * eg[rows - 1:rows, sls[g]]
        else:
            for s in range(nseq):
                rs = slice(s * tlen, (s + 1) * tlen)
                d = _mm(_bf(jnp.concatenate([u[g][rs], v_g[g][rs]], axis=0)),
                        _bf(jnp.concatenate([b_g[g][rs], k_g[g][rs]], axis=0)), _TN)
                s_new = (s0[g][s] + jnp.where(state_bd, d, zero_st)) * eg[(s + 1) * tlen - 1:(s + 1) * tlen, sls[g]]
                for h in range(heads):
                    hs = slice(h * HEAD_DIM, (h + 1) * HEAD_DIM)
                    s_out_ref[s, heads * g + h] = s_new[hs, hs]

    y = jnp.concatenate(ys, axis=1)
    inv_n = 1.0 / HEAD_DIM
    mean = _seg_sum4(y, seg) * inv_n
    yc = y - mean
    var = _seg_sum4(yc * yc, seg) * inv_n
    yn = yc * lax.rsqrt(var + LNX_EPS) * lnw_ref[...] + lnb_ref[...]
    out = ((yn + bonus) * gate).astype(o_ref.dtype)
    if carry:
        o_ref[j] = out

        @pl.when(ci == pl.num_programs(1) - 1)
        def _():
            sh_out_ref[j] = prev_scr[j]
            for g in range(n_groups):
                st = state_scr[j * n_groups + g]
                for h in range(heads):
                    hs = slice(h * HEAD_DIM, (h + 1) * HEAD_DIM)
                    s_out_ref[j, heads * g + h] = st[hs, hs]
    else:
        o_ref[...] = out


def _rwkv_mix4(p2, s_in, shift3, prm, nseq, tlen, npar=1):
    batch = s_in.shape[0]
    t_len = p2.shape[0] // batch
    n_chunks = t_len // tlen
    rows = nseq * tlen
    n_heads = RWKV_WIDTH // HEAD_DIM
    per_step = nseq * npar
    const2 = lambda shape: pl.BlockSpec(shape, lambda b, c: (0, 0))
    vec = const2((1, RWKV_WIDTH))
    state_spec = pl.BlockSpec((per_step, n_heads, HEAD_DIM, HEAD_DIM), lambda b, c: (b, 0, 0, 0))
    shift_spec = pl.BlockSpec((per_step, 1, RWKV_PROJ_PAD), lambda b, c: (b, 0, 0))
    if nseq == 1:
        p_in = p2.reshape(batch, t_len, RWKV_PROJ_PAD)
        p_spec = pl.BlockSpec((npar, tlen, RWKV_PROJ_PAD), lambda b, c: (b, c, 0))
        o_spec = pl.BlockSpec((npar, tlen, RWKV_WIDTH), lambda b, c: (b, c, 0))
        o_shape = jax.ShapeDtypeStruct((batch, t_len, RWKV_WIDTH), BF16)
    else:
        p_in = p2
        p_spec = pl.BlockSpec((rows, RWKV_PROJ_PAD), lambda b, c: (b * n_chunks + c, 0))
        o_spec = pl.BlockSpec((rows, RWKV_WIDTH), lambda b, c: (b * n_chunks + c, 0))
        o_shape = jax.ShapeDtypeStruct((batch * t_len, RWKV_WIDTH), BF16)
    o, s_out, sh_out = pl.pallas_call(
        functools.partial(_rwkv4_kernel, nseq=nseq, tlen=tlen, npar=npar),
        grid=(batch // per_step, n_chunks),
        in_specs=[
            p_spec,
            state_spec,
            shift_spec,
            const2((1, RWKV_PROJ_PAD)),
            vec,
            const2((LANES, RWKV_WIDTH)),
            vec,
            const2((LANES, RWKV_WIDTH)),
            const2((XG_PAD, RWKV_WIDTH)),
            vec, vec, vec, vec, vec,
            const2((GROUP, GROUP)),
        ],
        out_specs=[o_spec, state_spec, shift_spec],
        out_shape=[
            o_shape,
            jax.ShapeDtypeStruct(s_in.shape, F32),
            jax.ShapeDtypeStruct(shift3.shape, F32),
        ],
        scratch_shapes=[pltpu.VMEM((npar * (RWKV_WIDTH // GROUP), GROUP, GROUP), F32),
                        pltpu.VMEM((npar, 1, RWKV_PROJ_PAD), F32)],
        compiler_params=_cparams(("parallel", "arbitrary")),
        name="rwkv_mix_t%d" % tlen,
    )(p_in, s_in, shift3, prm["mu"], prm["w0"], prm["wd"], prm["a0"], prm["wa"], prm["wg"], prm["k_k"],
      prm["k_a"], prm["r_k"], prm["lnx_w"], prm["lnx_b"], prm["seg"])
    return o.reshape(batch * t_len, RWKV_WIDTH), s_out, sh_out


def _out_router_kernel(xp_ref, xs_ref, oap_ref, oas_ref, orp_ref, ors_ref, wa_ref, wr_ref, g_ref, rw_ref, rb_ref,
                       h_ref, hn_ref, idx_ref, gate_ref, *, n_p_tiles):
    body = functools.partial(_out_router_body, wa_ref=wa_ref, wr_ref=wr_ref, g_ref=g_ref, rw_ref=rw_ref,
                             rb_ref=rb_ref, h_ref=h_ref, hn_ref=hn_ref, idx_ref=idx_ref, gate_ref=gate_ref)
    is_prompt = pl.program_id(0) < n_p_tiles
    pl.when(is_prompt)(functools.partial(body, xp_ref, oap_ref, orp_ref))
    pl.when(jnp.logical_not(is_prompt))(functools.partial(body, xs_ref, oas_ref, ors_ref))


def _out_router_body(x_ref, oa_ref, or_ref, *, wa_ref, wr_ref, g_ref, rw_ref, rb_ref, h_ref, hn_ref, idx_ref,
                     gate_ref):
    h = x_ref[...] + _mm(oa_ref[...], wa_ref[...]) + _mm(or_ref[...], wr_ref[...])
    h_ref[...] = h
    hn = _rms(h, g_ref[...])
    hh, hl = _split2(hn)
    bits = lax.bitcast_convert_type(hh.astype(F32), jnp.int32)
    half = D_MODEL // 2
    hn_ref[...] = ((bits[:, :half] >> 16) & jnp.int32(0xFFFF)) | (bits[:, half:] & jnp.int32(-65536))
    logits = _mm(hh, rw_ref[0]) + (_mm(hh, rw_ref[1]) + _mm(hl, rw_ref[0])) + rb_ref[...]
    lane = lax.broadcasted_iota(jnp.int32, logits.shape, 1)
    vals = []
    idxs = []
    cur = logits
    for _ in range(TOP_K):
        m = jnp.max(cur, axis=-1, keepdims=True)
        sel = jnp.min(jnp.where(cur == m, lane, LANES), axis=-1, keepdims=True)
        vals.append(m)
        idxs.append(sel)
        cur = jnp.where(lane == sel, -jnp.inf, cur)
    es = [jnp.exp(vj - vals[0]) for vj in vals]
    tot = es[0] + es[1] + es[2] + es[3]
    idx_out = jnp.zeros(logits.shape, jnp.int32)
    gate_out = jnp.zeros(logits.shape, F32)
    for j in range(TOP_K):
        idx_out = jnp.where(lane == j, idxs[j], idx_out)
        gate_out = jnp.where(lane == j, es[j] / tot, gate_out)
    idx_ref[...] = idx_out
    gate_ref[...] = gate_out


def _out_router(x_p, x_s, oa_p, oa_s, or_p, or_s, w_oa, w_or, g, rw_split, rb_pad, tm):
    n_p_tiles = x_p.shape[0] // tm
    t_tok = x_p.shape[0] + x_s.shape[0]
    row = lambda w: pl.BlockSpec((tm, w), lambda i: (i, 0))
    row_p = lambda w: pl.BlockSpec((tm, w), lambda i: (jnp.minimum(i, n_p_tiles - 1), 0))
    row_s = lambda w: pl.BlockSpec((tm, w), lambda i: (jnp.maximum(i - n_p_tiles, 0), 0))
    full = lambda shape: pl.BlockSpec(shape, lambda i: tuple(0 for _ in shape))
    return pl.pallas_call(
        functools.partial(_out_router_kernel, n_p_tiles=n_p_tiles),
        grid=(t_tok // tm,),
        in_specs=[row_p(D_MODEL), row_s(D_MODEL), row_p(ATTN_WIDTH), row_s(ATTN_WIDTH), row_p(RWKV_WIDTH),
                  row_s(RWKV_WIDTH), full((ATTN_WIDTH, D_MODEL)), full((RWKV_WIDTH, D_MODEL)), full((1, D_MODEL)),
                  full((2, D_MODEL, LANES)), full((1, LANES))],
        out_specs=[row(D_MODEL), row(D_MODEL // 2), row(LANES), row(LANES)],
        out_shape=[
            jax.ShapeDtypeStruct((t_tok, D_MODEL), F32),
            jax.ShapeDtypeStruct((t_tok, D_MODEL // 2), jnp.int32),
            jax.ShapeDtypeStruct((t_tok, LANES), jnp.int32),
            jax.ShapeDtypeStruct((t_tok, LANES), F32),
        ],
        compiler_params=_cparams(("arbitrary",)),
        name="out_router",
    )(x_p, x_s, oa_p, oa_s, or_p, or_s, w_oa, w_or, g, rw_split, rb_pad)


GATHER_UNROLL = 8
MOE_SUB = 256


def _gather_kernel(nv_ref, tok_ref, tok_next_ref, hn_ref, o_ref, buf, sem):
    rows = buf.shape[1]
    i = pl.program_id(0)
    last = pl.num_programs(0) - 1
    slot = i % 2

    def issue(tok, dst_slot):
        def body(r8, carry):
            for u in range(GATHER_UNROLL):
                r = r8 * GATHER_UNROLL + u
                pltpu.make_async_copy(hn_ref.at[pl.ds(tok[r], 1)], buf.at[dst_slot, pl.ds(r, 1)],
                                      sem.at[dst_slot]).start()
            return carry

        lax.fori_loop(0, rows // GATHER_UNROLL, body, 0)

    @pl.when((i == 0) & (nv_ref[0] > 0))
    def _():
        issue(tok_ref, 0)

    @pl.when((i < last) & (nv_ref[jnp.minimum(i + 1, last)] > 0))
    def _():
        issue(tok_next_ref, 1 - slot)

    @pl.when(nv_ref[i] > 0)
    def _():
        pltpu.make_async_copy(hn_ref.at[pl.ds(0, rows)], buf.at[slot], sem.at[slot]).wait()
        words = buf[slot]
        half = words.shape[1]
        o_ref[:, :half] = lax.bitcast_convert_type(words << 16, F32).astype(o_ref.dtype)
        o_ref[:, half:] = lax.bitcast_convert_type(words & jnp.int32(-65536), F32).astype(o_ref.dtype)

    @pl.when(nv_ref[i] == 0)
    def _():
        o_ref[...] = jnp.zeros_like(o_ref)


def _gather_rows(sub_valid, row_tok, hn, rows_per_step):
    n_rows = row_tok.shape[0]
    n_steps = n_rows // rows_per_step
    grid_spec = pltpu.PrefetchScalarGridSpec(
        num_scalar_prefetch=1,
        grid=(n_steps,),
        in_specs=[
            pl.BlockSpec((rows_per_step,), lambda i, nv: (i,), memory_space=pltpu.SMEM),
            pl.BlockSpec((rows_per_step,), lambda i, nv: (jnp.minimum(i + 1, n_steps - 1),),
                         memory_space=pltpu.SMEM),
            pl.BlockSpec(memory_space=pl.ANY),
        ],
        out_specs=pl.BlockSpec((rows_per_step, D_MODEL), lambda i, nv: (i, 0)),
        scratch_shapes=[pltpu.VMEM((2, rows_per_step, D_MODEL // 2), jnp.int32), pltpu.SemaphoreType.DMA((2,))],
    )
    return pl.pallas_call(
        _gather_kernel,
        grid_spec=grid_spec,
        out_shape=jax.ShapeDtypeStruct((n_rows, D_MODEL), BF16),
        compiler_params=_cparams(("arbitrary",)),
        name="moe_gather",
    )(sub_valid, row_tok, row_tok, hn)


def _expert_kernel(be_ref, nv_ref, nr_ref, x_ref, wu_ref, bu_ref, wd_ref, bd_ref, sel_ref, o_ref):
    i = pl.program_id(0)
    f = pl.program_id(1)
    n_sub = x_ref.shape[0] // MOE_SUB
    used_sub = (nv_ref[i] + (MOE_SUB - 1)) // MOE_SUB

    def body(m_rows):
        rows = slice(0, m_rows)
        z = _mm(x_ref[rows, :], wu_ref[0].astype(BF16)) + bu_ref[0]
        zn = pltpu.roll(z, z.shape[1] - 1, 1)
        glu = jnp.minimum(z, SWIGLU_LIMIT)
        lin = jnp.clip(zn, -SWIGLU_LIMIT, SWIGLU_LIMIT)
        act = (glu * jax.nn.sigmoid(SWIGLU_ALPHA * glu) * (lin + 1.0)).astype(BF16)
        actc = _mm(act, sel_ref[...]).astype(BF16)
        wd = wd_ref[0].astype(BF16)

        @pl.when(f == 0)
        def _():
            o_ref[rows, :] = _mm(actc, wd) + bd_ref[0]
            if m_rows < o_ref.shape[0]:
                o_ref[m_rows:, :] = jnp.zeros((o_ref.shape[0] - m_rows, o_ref.shape[1]), o_ref.dtype)

        @pl.when(f > 0)
        def _():
            o_ref[rows, :] += _mm(actc, wd)

    for k in range(1, n_sub + 1):
        pl.when(used_sub == k)(functools.partial(body, k * MOE_SUB))

    @pl.when((used_sub == 0) & (f == 0))
    def _():
        o_ref[...] = jnp.zeros_like(o_ref)


def _experts(blk_expert, blk_valid, n_real, x_sorted, w_up, b_up3, w_down, b_down3, sel, bm, tf):
    n_rows = x_sorted.shape[0]
    n_f = D_FF // tf

    def real(i, nr):
        return jnp.minimum(i, nr[0] - 1)

    def f_eff(i, f, nr):
        return jnp.where(i < nr[0], f, n_f - 1)

    grid_spec = pltpu.PrefetchScalarGridSpec(
        num_scalar_prefetch=3,
        grid=(n_rows // bm, n_f),
        in_specs=[
            pl.BlockSpec((bm, D_MODEL), lambda i, f, be, nv, nr: (real(i, nr), 0)),
            pl.BlockSpec((1, D_MODEL, 2 * tf), lambda i, f, be, nv, nr: (be[i], 0, f_eff(i, f, nr))),
            pl.BlockSpec((1, 1, 2 * tf), lambda i, f, be, nv, nr: (be[i], 0, f_eff(i, f, nr))),
            pl.BlockSpec((1, tf, D_MODEL), lambda i, f, be, nv, nr: (be[i], f_eff(i, f, nr), 0)),
            pl.BlockSpec((1, 1, D_MODEL), lambda i, f, be, nv, nr: (be[i], 0, 0)),
            pl.BlockSpec((2 * tf, tf), lambda i, f, be, nv, nr: (0, 0)),
        ],
        out_specs=pl.BlockSpec((bm, D_MODEL), lambda i, f, be, nv, nr: (i, 0)),
    )
    return pl.pallas_call(
        _expert_kernel,
        grid_spec=grid_spec,
        out_shape=jax.ShapeDtypeStruct((n_rows, D_MODEL), F32),
        compiler_params=_cparams(("arbitrary", "arbitrary")),
        name="moe_experts",
    )(blk_expert, blk_valid, n_real, x_sorted, w_up, b_up3, w_down, b_down3, sel)


def _combine_kernel(dest_ref, dest_next_ref, rows_ref, h_ref, gate_ref, g_ref, op_ref, os_ref, buf, sem,
                    *, n_p_tiles):
    tm = h_ref.shape[0]
    i = pl.program_id(0)
    last = pl.num_programs(0) - 1
    slot = i % 2

    def issue(dest, dst_slot):
        def body(t2, carry):
            for u in range(2):
                t = t2 * 2 + u
                for j in range(TOP_K):
                    pltpu.make_async_copy(rows_ref.at[pl.ds(dest[t * TOP_K + j], 1)],
                                          buf.at[dst_slot, pl.ds(j * tm + t, 1)], sem.at[dst_slot]).start()
            return carry

        lax.fori_loop(0, tm // 2, body, 0)

    @pl.when(i == 0)
    def _():
        issue(dest_ref, 0)

    @pl.when(i < last)
    def _():
        issue(dest_next_ref, 1 - slot)

    pltpu.make_async_copy(rows_ref.at[pl.ds(0, TOP_K * tm)], buf.at[slot], sem.at[slot]).wait()
    gates = gate_ref[...]
    y = h_ref[...]
    for j in range(TOP_K):
        y = y + buf[slot, j * tm:(j + 1) * tm, :] * gates[:, j:j + 1]
    out = _rms(y, g_ref[...])
    is_prompt = pl.program_id(0) < n_p_tiles

    @pl.when(is_prompt)
    def _():
        op_ref[...] = out

    @pl.when(jnp.logical_not(is_prompt))
    def _():
        os_ref[...] = out


def _combine(dest, out_rows, h, gates, g, n_p, tm):
    t_tok = h.shape[0]
    n_p_tiles = n_p // tm
    n_steps = t_tok // tm
    return pl.pallas_call(
        functools.partial(_combine_kernel, n_p_tiles=n_p_tiles),
        grid=(n_steps,),
        in_specs=[
            pl.BlockSpec((tm * TOP_K,), lambda i: (i,), memory_space=pltpu.SMEM),
            pl.BlockSpec((tm * TOP_K,), lambda i: (jnp.minimum(i + 1, n_steps - 1),), memory_space=pltpu.SMEM),
            pl.BlockSpec(memory_space=pl.ANY),
            pl.BlockSpec((tm, D_MODEL), lambda i: (i, 0)),
            pl.BlockSpec((tm, LANES), lambda i: (i, 0)),
            pl.BlockSpec((1, D_MODEL), lambda i: (0, 0)),
        ],
        out_specs=[pl.BlockSpec((tm, D_MODEL), lambda i: (jnp.minimum(i, n_p_tiles - 1), 0)),
                   pl.BlockSpec((tm, D_MODEL), lambda i: (jnp.maximum(i - n_p_tiles, 0), 0))],
        out_shape=[jax.ShapeDtypeStruct((n_p, D_MODEL), F32),
                   jax.ShapeDtypeStruct((t_tok - n_p, D_MODEL), F32)],
        scratch_shapes=[pltpu.VMEM((2, TOP_K * tm, D_MODEL), F32), pltpu.SemaphoreType.DMA((2,))],
        compiler_params=_cparams(("arbitrary",)),
        name="moe_combine",
    )(dest, dest, out_rows, h, gates, g)


def _q_head_order():
    order = []
    for pair in range(2):
        for j in range(4):
            order += [8 * pair + j, 8 * pair + 4 + j]
    return order


def _q_col_perm():
    cols = []
    for h in _q_head_order():
        cols += list(range(h * HEAD_DIM, (h + 1) * HEAD_DIM))
    return jnp.asarray(cols, jnp.int32)


def _pad_rwkv_cols(m):
    def padw(a, w):
        return jnp.pad(a, [(0, 0)] * (a.ndim - 1) + [(0, w - a.shape[-1])])
    return jnp.concatenate([
        m[..., :XW_OFF],
        padw(m[..., XW_OFF:XW_OFF + DECAY_LORA], LANES),
        padw(m[..., XW_OFF + DECAY_LORA:XW_OFF + DECAY_LORA + ICLR_LORA], LANES),
        padw(m[..., XW_OFF + DECAY_LORA + ICLR_LORA:], XG_PAD),
    ], axis=-1)


def _unpad_rwkv_cols(m):
    return jnp.concatenate([
        m[..., :XW_OFF + DECAY_LORA],
        m[..., XA_OFF:XA_OFF + ICLR_LORA],
        m[..., XG_OFF:XG_OFF + GATE_LORA],
    ], axis=-1)


def _pad_rows(m, rows):
    return jnp.pad(m, ((0, rows - m.shape[0]), (0, 0)))


def _rope_tables(pos):
    half = HEAD_DIM // 2
    inv_freq = ROPE_THETA ** (-jnp.arange(half, dtype=F32) / half)
    ang = pos.astype(F32)[:, None] * inv_freq[None, :]
    cos = jnp.cos(ang)
    sin = jnp.sin(ang)
    cos_t = jnp.tile(jnp.concatenate([cos, cos], axis=-1), (1, LANES // HEAD_DIM))
    sin_t = jnp.tile(jnp.concatenate([-sin, sin], axis=-1), (1, LANES // HEAD_DIM))
    return cos_t, sin_t


def _route(top_idx, bm):
    n_tok = top_idx.shape[0]
    e_flat = top_idx.reshape(-1)
    onehot = (e_flat[:, None] == jnp.arange(N_EXPERTS, dtype=jnp.int32)[None, :]).astype(jnp.int32)
    csum = jnp.cumsum(onehot, axis=0)
    rank = jnp.sum(csum * onehot, axis=1) - 1
    counts = csum[-1]
    padded = ((counts + bm - 1) // bm) * bm
    p_end = jnp.cumsum(padded)
    p_start = p_end - padded
    dest = (p_start[e_flat] + rank).astype(jnp.int32)
    n_assign = n_tok * TOP_K
    n_blocks = -(-n_assign // bm) + N_EXPERTS
    n_rows = n_blocks * bm
    tok_flat = jnp.repeat(jnp.arange(n_tok, dtype=jnp.int32), TOP_K)
    row_tok = jnp.zeros((n_rows,), jnp.int32).at[dest].set(tok_flat, unique_indices=True)
    blk_start = jnp.arange(n_blocks, dtype=jnp.int32) * bm
    blk_expert = jnp.minimum(jnp.searchsorted(p_end, blk_start, side="right"), N_EXPERTS - 1).astype(jnp.int32)
    blk_valid = jnp.clip(counts[blk_expert] - (blk_start - p_start[blk_expert]), 0, bm).astype(jnp.int32)
    sub_start = jnp.arange(n_rows // MOE_SUB, dtype=jnp.int32) * MOE_SUB
    sub_expert = blk_expert[sub_start // bm]
    sub_valid = jnp.clip(counts[sub_expert] - (sub_start - p_start[sub_expert]), 0, MOE_SUB).astype(jnp.int32)
    n_real = (p_end[-1:] // bm).astype(jnp.int32)
    return dest, row_tok, blk_expert, blk_valid, sub_valid, n_real


def _pick(n, prefs):
    for p in prefs:
        if n % p == 0:
            return p
    return n


def kernel(x_prompt, x_sample, cache_k, cache_v, state_wkv, state_shift, attn_norm_g, w_in, attn_sinks, mu_shift, decay_w0, decay_lora_up, iclr_a0, iclr_lora_up, gate_lora_up, k_k, k_a, r_k, lnx_w, lnx_b, w_out, ffn_norm_g, router_w, router_b, w_up, b_up, w_down, b_down, final_norm_g):
    depth = w_in.shape[0]
    assert depth == 1
    batch, seq, _ = x_prompt.shape
    dec_b, dec_t, _ = x_sample.shape
    n_p = batch * seq
    n_s = dec_b * dec_t
    l = 0

    qperm = _q_col_perm()
    w_in_l = w_in[l]
    w_attn = jnp.concatenate([w_in_l[:, :ATTN_WIDTH][:, qperm], w_in_l[:, ATTN_WIDTH:ATTN_WIDTH + 2 * KV_WIDTH]],
                             axis=1).astype(BF16)
    w_rwkv = _pad_rwkv_cols(w_in_l[:, ATTN_WIDTH + 2 * KV_WIDTH:]).astype(BF16)
    w_oa = w_out[l][:ATTN_WIDTH][qperm].astype(BF16)
    w_or = w_out[l][ATTN_WIDTH:].astype(BF16)
    g_attn = attn_norm_g[l][None, :]
    g_ffn = ffn_norm_g[l][None, :]
    sinks_true = attn_sinks[l].astype(F32)
    seg = (jnp.arange(GROUP)[:, None] // HEAD_DIM == jnp.arange(GROUP)[None, :] // HEAD_DIM).astype(BF16)
    prm = dict(
        mu=_pad_rwkv_cols(mu_shift[l][None, :]),
        w0=decay_w0[l][None, :], wd=_pad_rows(decay_lora_up[l], LANES).astype(BF16),
        a0=iclr_a0[l][None, :], wa=_pad_rows(iclr_lora_up[l], LANES).astype(BF16),
        wg=_pad_rows(gate_lora_up[l], XG_PAD).astype(BF16),
        k_k=k_k[l][None, :], k_a=k_a[l][None, :], r_k=r_k[l].reshape(1, RWKV_WIDTH),
        lnx_w=lnx_w[l][None, :], lnx_b=lnx_b[l][None, :], seg=seg)
    rw_pad = jnp.pad(router_w[l], ((0, 0), (0, LANES - N_EXPERTS)))
    rw_hi = rw_pad.astype(BF16)
    rw_lo = (rw_pad - rw_hi.astype(F32)).astype(BF16)
    rw_split = jnp.stack([rw_hi, rw_lo])
    rb_pad = jnp.concatenate([router_b[l], jnp.full((LANES - N_EXPERTS,), NEG_BIG, F32)])[None, :]
    b_up3 = b_up[l][:, None, :]
    b_dn = b_down[l][:, None, :]
    tf = 256
    sel = (jnp.arange(2 * tf)[:, None] == 2 * jnp.arange(tf)[None, :]).astype(BF16)

    tm_p = _pick(seq, (512, 256, 128))
    cos_p, sin_p = _rope_tables(jnp.arange(seq, dtype=jnp.int32))
    q_p, k_p, v_p = _attn_proj(x_prompt.reshape(n_p, D_MODEL), g_attn, w_attn, cos_p, sin_p, tm_p)
    tm_s = _pick(n_s, (512, 256, 128, 64, 32, 16, 8))
    pos_s = PAST_LEN + (jnp.arange(n_s, dtype=jnp.int32) % dec_t)
    cos_s, sin_s = _rope_tables(pos_s)
    q_s, k_s, v_s = _attn_proj(x_sample.reshape(n_s, D_MODEL), g_attn, w_attn, cos_s, sin_s, tm_s)
    tn = 512
    pr_p = _rwkv_proj(x_prompt.reshape(n_p, D_MODEL), g_attn, w_rwkv, _pick(n_p, (1024, 512, 256, 128)), tn)
    pr_s = _rwkv_proj(x_sample.reshape(n_s, D_MODEL), g_attn, w_rwkv, tm_s, tn)

    oa_p = _attn_prompt(q_p, k_p, v_p, sinks_true, batch, seq)
    wb = cache_k.shape[2]
    sink_rows = jnp.stack([
        jnp.repeat(jnp.stack([sinks_true[8 * pair + 4 * e + j] for e in range(2) for j in range(4)]), dec_t)
        for pair in range(2)])[:, :, None]
    oa_s, nk_s, nv_s = _attn_sample(
        q_s.reshape(dec_b, dec_t, ATTN_WIDTH), k_s.reshape(dec_b, dec_t, KV_WIDTH),
        v_s.reshape(dec_b, dec_t, KV_WIDTH), cache_k[l].reshape(dec_b, wb, KV_WIDTH),
        cache_v[l].reshape(dec_b, wb, KV_WIDTH), sink_rows, _pick(dec_b, (8, 4, 2, 1)))

    zero_state = jnp.zeros((batch, RWKV_WIDTH // HEAD_DIM, HEAD_DIM, HEAD_DIM), F32)
    zero_shift = jnp.zeros((batch, 1, RWKV_PROJ_PAD), F32)
    or_p, st_p, sh_p = _rwkv_mix4(pr_p, zero_state, zero_shift, prm, 1, _pick(seq, (64, 32, 16, 8)),
                                  npar=1)
    seq_per_step = _pick(dec_b, (64 // dec_t, 1)) if 64 % dec_t == 0 else 1
    or_s, st_s, sh_s = _rwkv_mix4(pr_s, state_wkv[l], _pad_rwkv_cols(state_shift[l])[:, None, :], prm,
                                  seq_per_step, dec_t)

    n_all = n_p + n_s
    tm_o = _pick(math.gcd(n_p, n_s), (512, 256, 128, 64, 32, 16, 8))
    h, hn, idx_pad, gate_pad = _out_router(
        x_prompt.reshape(n_p, D_MODEL), x_sample.reshape(n_s, D_MODEL), oa_p, oa_s.reshape(n_s, ATTN_WIDTH),
        or_p, or_s, w_oa, w_or, g_ffn, rw_split, rb_pad, tm_o)

    bm = 1024 if n_all * TOP_K >= 1024 * N_EXPERTS else MOE_SUB
    dest, row_tok, blk_expert, blk_valid, sub_valid, n_real = _route(idx_pad[:, :TOP_K], bm)
    x_sorted = _gather_rows(sub_valid, row_tok, hn, MOE_SUB)
    out_rows = _experts(blk_expert, blk_valid, n_real, x_sorted, w_up[l], b_up3, w_down[l], b_dn, sel, bm, tf)
    y_p, y_s = _combine(dest, out_rows, h, gate_pad, final_norm_g[None, :], n_p,
                        _pick(math.gcd(n_p, n_s), (128, 64, 32, 16, 8)))

    y_prompt = y_p.reshape(batch, seq, D_MODEL)
    y_sample = y_s.reshape(dec_b, dec_t, D_MODEL)
    kp = k_p.reshape(batch, seq, 4, HEAD_DIM)[:, -WINDOW:][None]
    vp = v_p.reshape(batch, seq, 4, HEAD_DIM)[:, -WINDOW:][None]
    wp = st_p[None]
    sp = _unpad_rwkv_cols(sh_p[:, 0, :])[None]
    ks = nk_s.reshape(dec_b, wb, 4, HEAD_DIM)[None]
    vs = nv_s.reshape(dec_b, wb, 4, HEAD_DIM)[None]
    ws = st_s[None]
    ss = _unpad_rwkv_cols(sh_s[:, 0, :])[None]
    return (y_prompt, y_sample, kp, vp, wp, sp, ks, vs, ws, ss)
```

```python
import functools
import math

import jax
import jax.numpy as jnp
from jax import lax
from jax.experimental import pallas as pl
from jax.experimental.pallas import tpu as pltpu

F32 = jnp.float32
BF16 = jnp.bfloat16

D_MODEL = 2048
HEAD_DIM = 64
LANES = 128
ATTN_WIDTH = 1024
KV_WIDTH = 256
ATTN_HEADS = 16
RWKV_WIDTH = 1024
DECAY_LORA = 64
ICLR_LORA = 64
GATE_LORA = 160
RWKV_PROJ = 3 * RWKV_WIDTH + DECAY_LORA + ICLR_LORA + GATE_LORA
XW_OFF = 3 * RWKV_WIDTH
XA_OFF = XW_OFF + LANES
XG_OFF = XA_OFF + LANES
XG_PAD = 2 * LANES
RWKV_PROJ_PAD = XG_OFF + XG_PAD
WINDOW = 128
ROPE_THETA = 10000.0
PAST_LEN = 8192
N_EXPERTS = 32
TOP_K = 4
D_FF = 2048
SWIGLU_ALPHA = 1.702
SWIGLU_LIMIT = 7.0
NORM_EPS = 1e-5
LNX_EPS = 64e-5
NEG_BIG = -1e30
VMEM_LIMIT = 56 * 1024 * 1024


def _cparams(sem):
    return pltpu.CompilerParams(dimension_semantics=sem, vmem_limit_bytes=VMEM_LIMIT)


_NN = (((1,), (0,)), ((), ()))
_NT = (((1,), (1,)), ((), ()))
_TN = (((0,), (0,)), ((), ()))


def _mm(a, b, dims=_NN):
    return lax.dot_general(a, b, dims, preferred_element_type=F32)


def _split2(x):
    hi = x.astype(BF16)
    lo = (x - hi.astype(F32)).astype(BF16)
    return hi, lo


def _split3(x):
    hi = x.astype(BF16)
    r1 = x - hi.astype(F32)
    mid = r1.astype(BF16)
    lo = (r1 - mid.astype(F32)).astype(BF16)
    return hi, mid, lo


def _mm_exact_b(a, b_bf16, dims=_NN):
    h, m, l = _split3(a)
    return _mm(h, b_bf16, dims) + (_mm(m, b_bf16, dims) + _mm(l, b_bf16, dims))


def _mm_exact_a(a_bf16, b, dims=_NN):
    h, m, l = _split3(b)
    return _mm(a_bf16, h, dims) + (_mm(a_bf16, m, dims) + _mm(a_bf16, l, dims))


def _rms(x, g):
    ms = jnp.mean(x * x, axis=-1, keepdims=True)
    return (x * lax.rsqrt(ms + NORM_EPS)) * g


def _attn_proj_kernel(x_ref, g_ref, w_ref, cos_ref, sin_ref, q_ref, k_ref, v_ref):
    xn = _rms(x_ref[...], g_ref[...]).astype(BF16)
    a = _mm(xn, w_ref[...])
    cos = cos_ref[...]
    sin = sin_ref[...]
    lane = lax.broadcasted_iota(jnp.int32, cos.shape, 1)
    first_half = (lane % HEAD_DIM) < (HEAD_DIM // 2)
    n_rot = (ATTN_WIDTH + KV_WIDTH) // LANES
    for j in range(n_rot):
        t = a[:, j * LANES:(j + 1) * LANES]
        swapped = jnp.where(first_half, pltpu.roll(t, LANES - HEAD_DIM // 2, 1), pltpu.roll(t, HEAD_DIM // 2, 1))
        rot = t * cos + swapped * sin
        if j < ATTN_WIDTH // LANES:
            q_ref[:, j * LANES:(j + 1) * LANES] = (rot * (HEAD_DIM ** -0.5)).astype(BF16)
        else:
            jj = j - ATTN_WIDTH // LANES
            k_ref[:, jj * LANES:(jj + 1) * LANES] = rot
    v_ref[...] = a[:, ATTN_WIDTH + KV_WIDTH:]


def _attn_proj(x, g, w_attn, cos_t, sin_t, tm):
    t_tok = x.shape[0]
    n_pos_blocks = cos_t.shape[0] // tm
    return pl.pallas_call(
        _attn_proj_kernel,
        grid=(t_tok // tm,),
        in_specs=[
            pl.BlockSpec((tm, D_MODEL), lambda i: (i, 0)),
            pl.BlockSpec((1, D_MODEL), lambda i: (0, 0)),
            pl.BlockSpec((D_MODEL, ATTN_WIDTH + 2 * KV_WIDTH), lambda i: (0, 0)),
            pl.BlockSpec((tm, LANES), lambda i: (i % n_pos_blocks, 0)),
            pl.BlockSpec((tm, LANES), lambda i: (i % n_pos_blocks, 0)),
        ],
        out_specs=[
            pl.BlockSpec((tm, ATTN_WIDTH), lambda i: (i, 0)),
            pl.BlockSpec((tm, KV_WIDTH), lambda i: (i, 0)),
            pl.BlockSpec((tm, KV_WIDTH), lambda i: (i, 0)),
        ],
        out_shape=[
            jax.ShapeDtypeStruct((t_tok, ATTN_WIDTH), BF16),
            jax.ShapeDtypeStruct((t_tok, KV_WIDTH), F32),
            jax.ShapeDtypeStruct((t_tok, KV_WIDTH), F32),
        ],
        compiler_params=_cparams(("parallel",)),
        name="attn_proj",
    )(x, g, w_attn, cos_t, sin_t)


def _rwkv_proj_kernel(x_ref, g_ref, w_ref, p_ref, xn_ref):
    @pl.when(pl.program_id(1) == 0)
    def _():
        xn_ref[...] = _rms(x_ref[...], g_ref[...]).astype(BF16)

    p_ref[...] = _mm(xn_ref[...], w_ref[...])


def _rwkv_proj(x, g, w_rwkv, tm, tn):
    t_tok = x.shape[0]
    return pl.pallas_call(
        _rwkv_proj_kernel,
        grid=(t_tok // tm, RWKV_PROJ_PAD // tn),
        in_specs=[
            pl.BlockSpec((tm, D_MODEL), lambda i, n: (i, 0)),
            pl.BlockSpec((1, D_MODEL), lambda i, n: (0, 0)),
            pl.BlockSpec((D_MODEL, tn), lambda i, n: (0, n)),
        ],
        out_specs=pl.BlockSpec((tm, tn), lambda i, n: (i, n)),
        out_shape=jax.ShapeDtypeStruct((t_tok, RWKV_PROJ_PAD), F32),
        scratch_shapes=[pltpu.VMEM((tm, D_MODEL), BF16)],
        compiler_params=_cparams(("parallel", "arbitrary")),
        name="rwkv_proj",
    )(x, g, w_rwkv)


def _softmax_pv(s, mask, sink, vt):
    s = jnp.where(mask, s, NEG_BIG)
    m = jnp.maximum(jnp.max(s, axis=-1, keepdims=True), sink)
    p = jnp.exp(s - m)
    denom = jnp.sum(p, axis=-1, keepdims=True) + jnp.exp(sink - m)
    return _mm(p.astype(BF16), vt) * (1.0 / denom)


def _attn_prompt_kernel(q_ref, kp_ref, kc_ref, vp_ref, vc_ref, sink_ref, o_ref):
    i = pl.program_id(1)
    blk = q_ref.shape[0]
    kk = jnp.concatenate([kp_ref[...], kc_ref[...]], axis=0).astype(BF16)
    vv = jnp.concatenate([vp_ref[...], vc_ref[...]], axis=0).astype(BF16)
    a = lax.broadcasted_iota(jnp.int32, (blk, 2 * blk), 0)
    c = lax.broadcasted_iota(jnp.int32, (blk, 2 * blk), 1)
    mask = (c > a) & (c <= a + blk) & ((c >= blk) | (i > 0))
    lane_lo = lax.broadcasted_iota(jnp.int32, (blk, LANES), 1) < HEAD_DIM
    n_pairs = KV_WIDTH // LANES
    kts = [kk[:, pair * LANES:(pair + 1) * LANES] for pair in range(n_pairs)]
    vts = [vv[:, pair * LANES:(pair + 1) * LANES] for pair in range(n_pairs)]
    heads = [(pair, j, e) for pair in range(n_pairs) for j in range(4) for e in range(2)]
    scores = []
    for pair, j, e in heads:
        tile = pair * 4 + j
        qt = q_ref[:, tile * LANES:(tile + 1) * LANES]
        qm = jnp.where(lane_lo if e == 0 else jnp.logical_not(lane_lo), qt, jnp.zeros_like(qt))
        scores.append(_mm(qm, kts[pair], _NT))
    probs, scales = [], []
    for s, (pair, j, e) in zip(scores, heads):
        sink = sink_ref[8 * pair + 4 * e + j]
        s = jnp.where(mask, s, NEG_BIG)
        m = jnp.maximum(jnp.max(s, axis=-1, keepdims=True), sink)
        p = jnp.exp(s - m)
        probs.append(p.astype(BF16))
        scales.append(1.0 / (jnp.sum(p, axis=-1, keepdims=True) + jnp.exp(sink - m)))
    outs = [_mm(p, vts[pair]) * sc for p, sc, (pair, j, e) in zip(probs, scales, heads)]
    for tile in range(ATTN_WIDTH // LANES):
        o_ref[:, tile * LANES:(tile + 1) * LANES] = jnp.where(lane_lo, outs[2 * tile], outs[2 * tile + 1]).astype(BF16)


def _attn_prompt(q, k, v, sinks, batch, seq):
    blk = WINDOW
    nb = seq // blk
    cur = lambda b, i: (b * nb + i, 0)
    prev = lambda b, i: (b * nb + jnp.maximum(i - 1, 0), 0)
    return pl.pallas_call(
        _attn_prompt_kernel,
        grid=(batch, nb),
        in_specs=[
            pl.BlockSpec((blk, ATTN_WIDTH), cur),
            pl.BlockSpec((blk, KV_WIDTH), prev),
            pl.BlockSpec((blk, KV_WIDTH), cur),
            pl.BlockSpec((blk, KV_WIDTH), prev),
            pl.BlockSpec((blk, KV_WIDTH), cur),
            pl.BlockSpec(memory_space=pltpu.SMEM),
        ],
        out_specs=pl.BlockSpec((blk, ATTN_WIDTH), cur),
        out_shape=jax.ShapeDtypeStruct((batch * seq, ATTN_WIDTH), BF16),
        compiler_params=_cparams(("parallel", "arbitrary")),
        name="attn_prompt",
    )(q, k, k, v, v, sinks)


def _attn_sample_kernel(q_ref, kn_ref, vn_ref, ck_ref, cv_ref, sink_ref, o_ref, nk_ref, nv_ref):
    bb, t_len, _ = q_ref.shape
    wb = ck_ref.shape[1]
    n_keys = wb + t_len
    rows = 8 * t_len
    a = lax.broadcasted_iota(jnp.int32, (rows, n_keys), 0) % t_len
    c = lax.broadcasted_iota(jnp.int32, (rows, n_keys), 1)
    mask = ((c < wb) & (c > a + (wb - WINDOW))) | ((c >= wb) & (c - wb <= a))
    lane_lo = lax.broadcasted_iota(jnp.int32, (t_len, LANES), 1) < HEAD_DIM

    def body(b, carry):
        ck = ck_ref[b]
        cv = cv_ref[b]
        kn = kn_ref[b]
        vn = vn_ref[b]
        nk_ref[b, 0:wb - t_len, :] = ck[t_len:, :]
        nk_ref[b, wb - t_len:wb, :] = kn
        nv_ref[b, 0:wb - t_len, :] = cv[t_len:, :]
        nv_ref[b, wb - t_len:wb, :] = vn
        k_all = jnp.concatenate([ck, kn], axis=0).astype(BF16)
        v_all = jnp.concatenate([cv, vn], axis=0).astype(BF16)
        qb = q_ref[b]
        pairs = range(KV_WIDTH // LANES)
        scores = []
        for pair in pairs:
            stack = []
            for e in range(2):
                for j in range(4):
                    tile = pair * 4 + j
                    qt = qb[:, tile * LANES:(tile + 1) * LANES]
                    stack.append(jnp.where(lane_lo if e == 0 else jnp.logical_not(lane_lo), qt, jnp.zeros_like(qt)))
            qs = jnp.concatenate(stack, axis=0)
            scores.append(_mm(qs, k_all[:, pair * LANES:(pair + 1) * LANES], _NT))
        outs = [_softmax_pv(scores[pair], mask, sink_ref[pair], v_all[:, pair * LANES:(pair + 1) * LANES])
                for pair in pairs]
        for pair in pairs:
            o = outs[pair]
            for j in range(4):
                tile = pair * 4 + j
                lo = o[j * t_len:(j + 1) * t_len, :]
                hi = o[(4 + j) * t_len:(5 + j) * t_len, :]
                o_ref[b, :, tile * LANES:(tile + 1) * LANES] = jnp.where(lane_lo, lo, hi).astype(BF16)
        return carry

    lax.fori_loop(0, bb, body, 0)


def _attn_sample(q3, k3, v3, ck, cv, sink_rows, bb):
    dec_b, t_len, _ = q3.shape
    wb = ck.shape[1]
    blk3 = lambda w: pl.BlockSpec((bb, t_len, w), lambda i: (i, 0, 0))
    cache = pl.BlockSpec((bb, wb, KV_WIDTH), lambda i: (i, 0, 0))
    return pl.pallas_call(
        _attn_sample_kernel,
        grid=(dec_b // bb,),
        in_specs=[blk3(ATTN_WIDTH), blk3(KV_WIDTH), blk3(KV_WIDTH), cache, cache,
                  pl.BlockSpec((2, 8 * t_len, 1), lambda i: (0, 0, 0))],
        out_specs=[blk3(ATTN_WIDTH), cache, cache],
        out_shape=[
            jax.ShapeDtypeStruct((dec_b, t_len, ATTN_WIDTH), BF16),
            jax.ShapeDtypeStruct(ck.shape, F32),
            jax.ShapeDtypeStruct(cv.shape, F32),
        ],
        compiler_params=_cparams(("parallel",)),
        name="attn_sample",
    )(q3, k3, v3, ck, cv, sink_rows)


GROUP = 4 * HEAD_DIM


def _bf(x):
    return x.astype(BF16)


def _seg_sum4(x, seg_mat):
    outs = []
    for t in range(x.shape[1] // GROUP):
        outs.append(_mm_exact_b(x[:, t * GROUP:(t + 1) * GROUP], seg_mat))
    return jnp.concatenate(outs, axis=1)


def _stack_heads(x, head_masks):
    zero = jnp.zeros_like(x)
    return jnp.concatenate([jnp.where(m, x, zero) for m in head_masks], axis=0)


def _sum_blocks(x, r):
    return (x[0:r] + x[r:2 * r]) + (x[2 * r:3 * r] + x[3 * r:4 * r])


def _place_heads(blocks):
    rows = []
    for h, blk in enumerate(blocks):
        rows.append(jnp.concatenate([blk if j == h else jnp.zeros_like(blk) for j in range(len(blocks))], axis=1))
    return jnp.concatenate(rows, axis=0)


def _rwkv4_kernel(p_ref, s_in_ref, sh_ref, mu_ref, w0_ref, wd_ref, a0_ref, wa_ref, wg_ref, kk_ref, ka_ref,
                  rk_ref, lnw_ref, lnb_ref, seg_ref, o_ref, s_out_ref, sh_out_ref, state_scr, prev_scr,
                  *, nseq, tlen, npar):
    rows = nseq * tlen
    heads = GROUP // HEAD_DIM
    log_t = tlen.bit_length() - 1
    log_r = rows.bit_length() - 1
    log_h = HEAD_DIM.bit_length() - 1

    ri = lax.broadcasted_iota(jnp.int32, (rows, rows), 0)
    cj = lax.broadcasted_iota(jnp.int32, (rows, rows), 1)
    bri = lax.broadcasted_iota(jnp.int32, (heads * rows, heads * rows), 0)
    bcj = lax.broadcasted_iota(jnp.int32, (heads * rows, heads * rows), 1)
    same = ((bri >> log_r) == (bcj >> log_r)) & ((bri >> log_t) == (bcj >> log_t))
    lane = lax.broadcasted_iota(jnp.int32, (rows, GROUP), 1)
    sr = lax.broadcasted_iota(jnp.int32, (GROUP, GROUP), 0)
    sc = lax.broadcasted_iota(jnp.int32, (GROUP, GROUP), 1)
    masks = dict(
        tri=jnp.where(((ri >> log_t) == (cj >> log_t)) & (ri >= cj), 1.0, 0.0).astype(BF16),
        strict_bd=same & (bri > bcj),
        incl_bd=same & (bri >= bcj),
        eye_bd=jnp.where(bri == bcj, 1.0, 0.0).astype(F32),
        head_masks=[(lane >> log_h) == h for h in range(heads)],
        state_bd=(sr >> log_h) == (sc >> log_h),
    )
    refs = (p_ref, s_in_ref, sh_ref, mu_ref, w0_ref, wd_ref, a0_ref, wa_ref, wg_ref, kk_ref, ka_ref, rk_ref,
            lnw_ref, lnb_ref, seg_ref, o_ref, s_out_ref, sh_out_ref, state_scr, prev_scr)
    for j in range(npar):
        _rwkv4_block(j, refs, masks, nseq=nseq, tlen=tlen)


def _rwkv4_block(j, refs, masks, *, nseq, tlen):
    (p_ref, s_in_ref, sh_ref, mu_ref, w0_ref, wd_ref, a0_ref, wa_ref, wg_ref, kk_ref, ka_ref, rk_ref,
     lnw_ref, lnb_ref, seg_ref, o_ref, s_out_ref, sh_out_ref, state_scr, prev_scr) = refs
    ci = pl.program_id(1)
    rows = nseq * tlen
    n_groups = RWKV_WIDTH // GROUP
    heads = GROUP // HEAD_DIM
    log_t = tlen.bit_length() - 1
    carry = nseq == 1

    p = p_ref[j] if carry else p_ref[...]
    rowi = lax.broadcasted_iota(jnp.int32, (rows, 1), 0)
    rolled = pltpu.roll(p, 1, 0)
    if carry:
        @pl.when(ci == 0)
        def _():
            prev_scr[j] = sh_ref[j]
            for g in range(n_groups):
                state_scr[j * n_groups + g] = _place_heads([s_in_ref[j, heads * g + h] for h in range(heads)])

        p_prev = jnp.where(rowi == 0, prev_scr[j], rolled)
        prev_scr[j] = p[rows - 1:rows, :]
    else:
        p_prev = rolled
        for s in range(nseq):
            p_prev = jnp.where(rowi == s * tlen, sh_ref[s], p_prev)
            sh_out_ref[s] = p[(s + 1) * tlen - 1:(s + 1) * tlen, :]

    xs = p + (p_prev - p) * mu_ref[...]
    r = xs[:, 0:RWKV_WIDTH]
    k = xs[:, RWKV_WIDTH:2 * RWKV_WIDTH]
    v = xs[:, 2 * RWKV_WIDTH:3 * RWKV_WIDTH]
    xw = xs[:, XW_OFF:XW_OFF + LANES]
    xa = xs[:, XA_OFF:XA_OFF + LANES]
    xg = xs[:, XG_OFF:XG_OFF + XG_PAD]
    z = w0_ref[...] + _mm(_bf(jnp.tanh(xw)), wd_ref[...])
    w_log = -jax.nn.softplus(-z) - 0.5
    lw = -jnp.exp(w_log)
    a = jax.nn.sigmoid(a0_ref[...] + _mm(_bf(xa), wa_ref[...]))
    gate = _mm(_bf(jax.nn.sigmoid(xg)), wg_ref[...])
    seg = seg_ref[...]
    kk = k * kk_ref[...]
    kap = kk / jnp.maximum(jnp.sqrt(_seg_sum4(kk * kk, seg)), 1e-12)
    k2 = k * (1.0 + (a - 1.0) * ka_ref[...])
    b = kap * a
    bonus = _seg_sum4(r * k2 * rk_ref[...], seg) * v

    cum = _mm_exact_a(masks["tri"], lw)
    eg = jnp.exp(cum)
    kap_t = kap * jnp.exp(cum - lw)
    r_t = r * eg
    einv = jnp.exp(-cum)
    b_t = b * einv
    k_t = k2 * einv

    strict_bd, incl_bd, eye_bd = masks["strict_bd"], masks["incl_bd"], masks["eye_bd"]
    head_masks, state_bd = masks["head_masks"], masks["state_bd"]
    zero_bd = jnp.zeros((heads * rows, heads * rows), F32)
    zero_st = jnp.zeros((GROUP, GROUP), F32)

    groups = range(n_groups)
    hr = heads * rows
    sls = [slice(g * GROUP, (g + 1) * GROUP) for g in groups]
    kap_g = [kap_t[:, sl] for sl in sls]
    r_g = [r_t[:, sl] for sl in sls]
    b_g = [b_t[:, sl] for sl in sls]
    k_g = [k_t[:, sl] for sl in sls]
    v_g = [v[:, sl] for sl in sls]
    lkr = [jnp.concatenate([_stack_heads(_bf(kap_g[g]), head_masks), _stack_heads(_bf(r_g[g]), head_masks)], axis=0)
           for g in groups]
    gb = [_mm(lkr[g], _stack_heads(_bf(b_g[g]), head_masks), _NT) for g in groups]
    gk = [_mm(lkr[g], _stack_heads(_bf(k_g[g]), head_masks), _NT) for g in groups]
    n = [-jnp.where(strict_bd, gb[g][:hr], zero_bd) for g in groups]
    t_inv = [eye_bd + n[g] for g in groups]
    for _ in range(max(log_t - 1, 0)):
        nb = [_bf(n[g]) for g in groups]
        n = [_mm(nb[g], nb[g]) for g in groups]
        t_inv = [t_inv[g] + _mm(_bf(t_inv[g]), _bf(n[g])) for g in groups]
    a_k = [_bf(jnp.where(strict_bd, gk[g][:hr], zero_bd)) for g in groups]
    r_b = [_bf(jnp.where(incl_bd, gb[g][hr:], zero_bd)) for g in groups]
    r_k = [_bf(jnp.where(incl_bd, gk[g][hr:], zero_bd)) for g in groups]
    v_stack = [_stack_heads(_bf(v_g[g]), head_masks) for g in groups]

    if carry:
        s0 = [state_scr[j * n_groups + g] for g in groups]
        sk = [_mm(_bf(jnp.concatenate([kap_g[g], r_g[g]], axis=0)), _bf(s0[g]), _NT) for g in groups]
        k_s = [sk[g][:rows] for g in groups]
        r_s = [sk[g][rows:] for g in groups]
    else:
        s0, k_s, r_s = [], [], []
        for g in groups:
            s0_list, ks_list, rs_list = [], [], []
            for s in range(nseq):
                rs = slice(s * tlen, (s + 1) * tlen)
                st = _place_heads([s_in_ref[s, heads * g + h] for h in range(heads)])
                sk = _mm(_bf(jnp.concatenate([kap_g[g][rs], r_g[g][rs]], axis=0)), _bf(st), _NT)
                s0_list.append(st)
                ks_list.append(sk[:tlen])
                rs_list.append(sk[tlen:])
            s0.append(s0_list)
            k_s.append(jnp.concatenate(ks_list, axis=0))
            r_s.append(jnp.concatenate(rs_list, axis=0))

    w = [_sum_blocks(_mm(a_k[g], v_stack[g]), rows) for g in groups]
    rhs = [-(k_s[g] + w[g]) for g in groups]
    u_stack = [_mm(_bf(t_inv[g]), _stack_heads(_bf(rhs[g]), head_masks)) for g in groups]
    u = [_sum_blocks(u_stack[g], rows) for g in groups]
    y_stack = [_mm(r_b[g], _bf(u_stack[g])) + _mm(r_k[g], v_stack[g]) for g in groups]
    ys = [r_s[g] + _sum_blocks(y_stack[g], rows) for g in groups]

    for g in groups:
        if carry:
            d = _mm(_bf(jnp.concatenate([u[g], v_g[g]], axis=0)), _bf(jnp.concatenate([b_g[g], k_g[g]], axis=0)), _TN)
            state_scr[j * n_groups + g] = (s0[g] + jnp.where(state_bd, d, zero_st)) * eg[rows - 1:rows, sls[g]]
        else:
            for s in range(nseq):
                rs = slice(s * tlen, (s + 1) * tlen)
                d = _mm(_bf(jnp.concatenate([u[g][rs], v_g[g][rs]], axis=0)),
                        _bf(jnp.concatenate([b_g[g][rs], k_g[g][rs]], axis=0)), _TN)
                s_new = (s0[g][s] + jnp.where(state_bd, d, zero_st)) * eg[(s + 1) * tlen - 1:(s + 1) * tlen, sls[g]]
                for h in range(heads):
                    hs = slice(h * HEAD_DIM, (h + 1) * HEAD_DIM)
                    s_out_ref[s, heads * g + h] = s_new[hs, hs]

    y = jnp.concatenate(ys, axis=1)
    inv_n = 1.0 / HEAD_DIM
    mean = _seg_sum4(y, seg) * inv_n
    yc = y - mean
    var = _seg_sum4(yc * yc, seg) * inv_n
    yn = yc * lax.rsqrt(var + LNX_EPS) * lnw_ref[...] + lnb_ref[...]
    out = ((yn + bonus) * gate).astype(o_ref.dtype)
    if carry:
        o_ref[j] = out

        @pl.when(ci == pl.num_programs(1) - 1)
        def _():
            sh_out_ref[j] = prev_scr[j]
            for g in range(n_groups):
                st = state_scr[j * n_groups + g]
                for h in range(heads):
                    hs = slice(h * HEAD_DIM, (h + 1) * HEAD_DIM)
                    s_out_ref[j, heads * g + h] = st[hs, hs]
    else:
        o_ref[...] = out


def _rwkv_mix4(p2, s_in, shift3, prm, nseq, tlen, npar=1):
    batch = s_in.shape[0]
    t_len = p2.shape[0] // batch
    n_chunks = t_len // tlen
    rows = nseq * tlen
    n_heads = RWKV_WIDTH // HEAD_DIM
    per_step = nseq * npar
    const2 = lambda shape: pl.BlockSpec(shape, lambda b, c: (0, 0))
    vec = const2((1, RWKV_WIDTH))
    state_spec = pl.BlockSpec((per_step, n_heads, HEAD_DIM, HEAD_DIM), lambda b, c: (b, 0, 0, 0))
    shift_spec = pl.BlockSpec((per_step, 1, RWKV_PROJ_PAD), lambda b, c: (b, 0, 0))
    if nseq == 1:
        p_in = p2.reshape(batch, t_len, RWKV_PROJ_PAD)
        p_spec = pl.BlockSpec((npar, tlen, RWKV_PROJ_PAD), lambda b, c: (b, c, 0))
        o_spec = pl.BlockSpec((npar, tlen, RWKV_WIDTH), lambda b, c: (b, c, 0))
        o_shape = jax.ShapeDtypeStruct((batch, t_len, RWKV_WIDTH), BF16)
    else:
        p_in = p2
        p_spec = pl.BlockSpec((rows, RWKV_PROJ_PAD), lambda b, c: (b * n_chunks + c, 0))
        o_spec = pl.BlockSpec((rows, RWKV_WIDTH), lambda b, c: (b * n_chunks + c, 0))
        o_shape = jax.ShapeDtypeStruct((batch * t_len, RWKV_WIDTH), BF16)
    o, s_out, sh_out = pl.pallas_call(
        functools.partial(_rwkv4_kernel, nseq=nseq, tlen=tlen, npar=npar),
        grid=(batch // per_step, n_chunks),
        in_specs=[
            p_spec,
            state_spec,
            shift_spec,
            const2((1, RWKV_PROJ_PAD)),
            vec,
            const2((LANES, RWKV_WIDTH)),
            vec,
            const2((LANES, RWKV_WIDTH)),
            const2((XG_PAD, RWKV_WIDTH)),
            vec, vec, vec, vec, vec,
            const2((GROUP, GROUP)),
        ],
        out_specs=[o_spec, state_spec, shift_spec],
        out_shape=[
            o_shape,
            jax.ShapeDtypeStruct(s_in.shape, F32),
            jax.ShapeDtypeStruct(shift3.shape, F32),
        ],
        scratch_shapes=[pltpu.VMEM((npar * (RWKV_WIDTH // GROUP), GROUP, GROUP), F32),
                        pltpu.VMEM((npar, 1, RWKV_PROJ_PAD), F32)],
        compiler_params=_cparams(("parallel", "arbitrary")),
        name="rwkv_mix_t%d" % tlen,
    )(p_in, s_in, shift3, prm["mu"], prm["w0"], prm["wd"], prm["a0"], prm["wa"], prm["wg"], prm["k_k"],
      prm["k_a"], prm["r_k"], prm["lnx_w"], prm["lnx_b"], prm["seg"])
    return o.reshape(batch * t_len, RWKV_WIDTH), s_out, sh_out


def _out_router_kernel(xp_ref, xs_ref, oap_ref, oas_ref, orp_ref, ors_ref, wa_ref, wr_ref, g_ref, rw_ref, rb_ref,
                       h_ref, hn_ref, idx_ref, gate_ref, *, n_p_tiles):
    body = functools.partial(_out_router_body, wa_ref=wa_ref, wr_ref=wr_ref, g_ref=g_ref, rw_ref=rw_ref,
                             rb_ref=rb_ref, h_ref=h_ref, hn_ref=hn_ref, idx_ref=idx_ref, gate_ref=gate_ref)
    is_prompt = pl.program_id(0) < n_p_tiles
    pl.when(is_prompt)(functools.partial(body, xp_ref, oap_ref, orp_ref))
    pl.when(jnp.logical_not(is_prompt))(functools.partial(body, xs_ref, oas_ref, ors_ref))


def _out_router_body(x_ref, oa_ref, or_ref, *, wa_ref, wr_ref, g_ref, rw_ref, rb_ref, h_ref, hn_ref, idx_ref,
                     gate_ref):
    h = x_ref[...] + _mm(oa_ref[...], wa_ref[...]) + _mm(or_ref[...], wr_ref[...])
    h_ref[...] = h
    hn = _rms(h, g_ref[...])
    hh, hl = _split2(hn)
    bits = lax.bitcast_convert_type(hh.astype(F32), jnp.int32)
    half = D_MODEL // 2
    hn_ref[...] = ((bits[:, :half] >> 16) & jnp.int32(0xFFFF)) | (bits[:, half:] & jnp.int32(-65536))
    logits = _mm(hh, rw_ref[0]) + (_mm(hh, rw_ref[1]) + _mm(hl, rw_ref[0])) + rb_ref[...]
    lane = lax.broadcasted_iota(jnp.int32, logits.shape, 1)
    vals = []
    idxs = []
    cur = logits
    for _ in range(TOP_K):
        m = jnp.max(cur, axis=-1, keepdims=True)
        sel = jnp.min(jnp.where(cur == m, lane, LANES), axis=-1, keepdims=True)
        vals.append(m)
        idxs.append(sel)
        cur = jnp.where(lane == sel, -jnp.inf, cur)
    es = [jnp.exp(vj - vals[0]) for vj in vals]
    tot = es[0] + es[1] + es[2] + es[3]
    idx_out = jnp.zeros(logits.shape, jnp.int32)
    gate_out = jnp.zeros(logits.shape, F32)
    for j in range(TOP_K):
        idx_out = jnp.where(lane == j, idxs[j], idx_out)
        gate_out = jnp.where(lane == j, es[j] / tot, gate_out)
    idx_ref[...] = idx_out
    gate_ref[...] = gate_out


def _out_router(x_p, x_s, oa_p, oa_s, or_p, or_s, w_oa, w_or, g, rw_split, rb_pad, tm):
    n_p_tiles = x_p.shape[0] // tm
    t_tok = x_p.shape[0] + x_s.shape[0]
    row = lambda w: pl.BlockSpec((tm, w), lambda i: (i, 0))
    row_p = lambda w: pl.BlockSpec((tm, w), lambda i: (jnp.minimum(i, n_p_tiles - 1), 0))
    row_s = lambda w: pl.BlockSpec((tm, w), lambda i: (jnp.maximum(i - n_p_tiles, 0), 0))
    full = lambda shape: pl.BlockSpec(shape, lambda i: tuple(0 for _ in shape))
    return pl.pallas_call(
        functools.partial(_out_router_kernel, n_p_tiles=n_p_tiles),
        grid=(t_tok // tm,),
        in_specs=[row_p(D_MODEL), row_s(D_MODEL), row_p(ATTN_WIDTH), row_s(ATTN_WIDTH), row_p(RWKV_WIDTH),
                  row_s(RWKV_WIDTH), full((ATTN_WIDTH, D_MODEL)), full((RWKV_WIDTH, D_MODEL)), full((1, D_MODEL)),
                  full((2, D_MODEL, LANES)), full((1, LANES))],
        out_specs=[row(D_MODEL), row(D_MODEL // 2), row(LANES), row(LANES)],
        out_shape=[
            jax.ShapeDtypeStruct((t_tok, D_MODEL), F32),
            jax.ShapeDtypeStruct((t_tok, D_MODEL // 2), jnp.int32),
            jax.ShapeDtypeStruct((t_tok, LANES), jnp.int32),
            jax.ShapeDtypeStruct((t_tok, LANES), F32),
        ],
        compiler_params=_cparams(("arbitrary",)),
        name="out_router",
    )(x_p, x_s, oa_p, oa_s, or_p, or_s, w_oa, w_or, g, rw_split, rb_pad)


GATHER_UNROLL = 8
MOE_SUB = 256


def _gather_kernel(nv_ref, tok_ref, tok_next_ref, hn_ref, o_ref, buf, sem):
    rows = buf.shape[1]
    i = pl.program_id(0)
    last = pl.num_programs(0) - 1
    slot = i % 2

    def issue(tok, dst_slot):
        def body(r8, carry):
            for u in range(GATHER_UNROLL):
                r = r8 * GATHER_UNROLL + u
                pltpu.make_async_copy(hn_ref.at[pl.ds(tok[r], 1)], buf.at[dst_slot, pl.ds(r, 1)],
                                      sem.at[dst_slot]).start(priority=u % 2)
            return carry

        lax.fori_loop(0, rows // GATHER_UNROLL, body, 0)

    @pl.when((i == 0) & (nv_ref[0] > 0))
    def _():
        issue(tok_ref, 0)

    @pl.when((i < last) & (nv_ref[jnp.minimum(i + 1, last)] > 0))
    def _():
        issue(tok_next_ref, 1 - slot)

    @pl.when(nv_ref[i] > 0)
    def _():
        pltpu.make_async_copy(hn_ref.at[pl.ds(0, rows)], buf.at[slot], sem.at[slot]).wait()
        words = buf[slot]
        half = words.shape[1]
        o_ref[:, :half] = lax.bitcast_convert_type(words << 16, F32).astype(o_ref.dtype)
        o_ref[:, half:] = lax.bitcast_convert_type(words & jnp.int32(-65536), F32).astype(o_ref.dtype)

    @pl.when(nv_ref[i] == 0)
    def _():
        o_ref[...] = jnp.zeros_like(o_ref)


def _gather_rows(sub_valid, row_tok, hn, rows_per_step):
    n_rows = row_tok.shape[0]
    n_steps = n_rows // rows_per_step
    grid_spec = pltpu.PrefetchScalarGridSpec(
        num_scalar_prefetch=1,
        grid=(n_steps,),
        in_specs=[
            pl.BlockSpec((rows_per_step,), lambda i, nv: (i,), memory_space=pltpu.SMEM),
            pl.BlockSpec((rows_per_step,), lambda i, nv: (jnp.minimum(i + 1, n_steps - 1),),
                         memory_space=pltpu.SMEM),
            pl.BlockSpec(memory_space=pl.ANY),
        ],
        out_specs=pl.BlockSpec((rows_per_step, D_MODEL), lambda i, nv: (i, 0)),
        scratch_shapes=[pltpu.VMEM((2, rows_per_step, D_MODEL // 2), jnp.int32), pltpu.SemaphoreType.DMA((2,))],
    )
    return pl.pallas_call(
        _gather_kernel,
        grid_spec=grid_spec,
        out_shape=jax.ShapeDtypeStruct((n_rows, D_MODEL), BF16),
        compiler_params=_cparams(("arbitrary",)),
        name="moe_gather",
    )(sub_valid, row_tok, row_tok, hn)


def _expert_kernel(be_ref, nv_ref, nr_ref, x_ref, wu_ref, bu_ref, wd_ref, bd_ref, sel_ref, o_ref):
    i = pl.program_id(0)
    f = pl.program_id(1)
    n_sub = x_ref.shape[0] // MOE_SUB
    used_sub = (nv_ref[i] + (MOE_SUB - 1)) // MOE_SUB

    def body(m_rows):
        rows = slice(0, m_rows)
        z = _mm(x_ref[rows, :], wu_ref[0].astype(BF16)) + bu_ref[0]
        zn = pltpu.roll(z, z.shape[1] - 1, 1)
        glu = jnp.minimum(z, SWIGLU_LIMIT)
        lin = jnp.clip(zn, -SWIGLU_LIMIT, SWIGLU_LIMIT)
        act = (glu * jax.nn.sigmoid(SWIGLU_ALPHA * glu) * (lin + 1.0)).astype(BF16)
        actc = _mm(act, sel_ref[...]).astype(BF16)
        wd = wd_ref[0].astype(BF16)

        @pl.when(f == 0)
        def _():
            o_ref[rows, :] = _mm(actc, wd) + bd_ref[0]
            if m_rows < o_ref.shape[0]:
                o_ref[m_rows:, :] = jnp.zeros((o_ref.shape[0] - m_rows, o_ref.shape[1]), o_ref.dtype)

        @pl.when(f > 0)
        def _():
            o_ref[rows, :] += _mm(actc, wd)

    for k in range(1, n_sub + 1):
        pl.when(used_sub == k)(functools.partial(body, k * MOE_SUB))

    @pl.when((used_sub == 0) & (f == 0))
    def _():
        o_ref[...] = jnp.zeros_like(o_ref)


def _experts(blk_expert, blk_valid, n_real, x_sorted, w_up, b_up3, w_down, b_down3, sel, bm, tf):
    n_rows = x_sorted.shape[0]
    n_f = D_FF // tf

    def real(i, nr):
        return jnp.minimum(i, nr[0] - 1)

    def f_eff(i, f, nr):
        return jnp.where(i < nr[0], f, n_f - 1)

    grid_spec = pltpu.PrefetchScalarGridSpec(
        num_scalar_prefetch=3,
        grid=(n_rows // bm, n_f),
        in_specs=[
            pl.BlockSpec((bm, D_MODEL), lambda i, f, be, nv, nr: (real(i, nr), 0)),
            pl.BlockSpec((1, D_MODEL, 2 * tf), lambda i, f, be, nv, nr: (be[i], 0, f_eff(i, f, nr))),
            pl.BlockSpec((1, 1, 2 * tf), lambda i, f, be, nv, nr: (be[i], 0, f_eff(i, f, nr))),
            pl.BlockSpec((1, tf, D_MODEL), lambda i, f, be, nv, nr: (be[i], f_eff(i, f, nr), 0)),
            pl.BlockSpec((1, 1, D_MODEL), lambda i, f, be, nv, nr: (be[i], 0, 0)),
            pl.BlockSpec((2 * tf, tf), lambda i, f, be, nv, nr: (0, 0)),
        ],
        out_specs=pl.BlockSpec((bm, D_MODEL), lambda i, f, be, nv, nr: (i, 0)),
    )
    return pl.pallas_call(
        _expert_kernel,
        grid_spec=grid_spec,
        out_shape=jax.ShapeDtypeStruct((n_rows, D_MODEL), F32),
        compiler_params=_cparams(("arbitrary", "arbitrary")),
        name="moe_experts",
    )(blk_expert, blk_valid, n_real, x_sorted, w_up, b_up3, w_down, b_down3, sel)


def _combine_kernel(dest_ref, dest_next_ref, rows_ref, h_ref, gate_ref, g_ref, op_ref, os_ref, buf, sem,
                    *, n_p_tiles):
    tm = h_ref.shape[0]
    i = pl.program_id(0)
    last = pl.num_programs(0) - 1
    slot = i % 2

    def issue(dest, dst_slot):
        def body(t2, carry):
            for u in range(2):
                t = t2 * 2 + u
                for j in range(TOP_K):
                    pltpu.make_async_copy(rows_ref.at[pl.ds(dest[t * TOP_K + j], 1)],
                                          buf.at[dst_slot, pl.ds(j * tm + t, 1)],
                                          sem.at[dst_slot]).start(priority=j % 2)
            return carry

        lax.fori_loop(0, tm // 2, body, 0)

    @pl.when(i == 0)
    def _():
        issue(dest_ref, 0)

    @pl.when(i < last)
    def _():
        issue(dest_next_ref, 1 - slot)

    pltpu.make_async_copy(rows_ref.at[pl.ds(0, TOP_K * tm)], buf.at[slot], sem.at[slot]).wait()
    gates = gate_ref[...]
    y = h_ref[...]
    for j in range(TOP_K):
        y = y + buf[slot, j * tm:(j + 1) * tm, :] * gates[:, j:j + 1]
    out = _rms(y, g_ref[...])
    is_prompt = pl.program_id(0) < n_p_tiles

    @pl.when(is_prompt)
    def _():
        op_ref[...] = out

    @pl.when(jnp.logical_not(is_prompt))
    def _():
        os_ref[...] = out


def _combine(dest, out_rows, h, gates, g, n_p, tm):
    t_tok = h.shape[0]
    n_p_tiles = n_p // tm
    n_steps = t_tok // tm
    return pl.pallas_call(
        functools.partial(_combine_kernel, n_p_tiles=n_p_tiles),
        grid=(n_steps,),
        in_specs=[
            pl.BlockSpec((tm * TOP_K,), lambda i: (i,), memory_space=pltpu.SMEM),
            pl.BlockSpec((tm * TOP_K,), lambda i: (jnp.minimum(i + 1, n_steps - 1),), memory_space=pltpu.SMEM),
            pl.BlockSpec(memory_space=pl.ANY),
            pl.BlockSpec((tm, D_MODEL), lambda i: (i, 0)),
            pl.BlockSpec((tm, LANES), lambda i: (i, 0)),
            pl.BlockSpec((1, D_MODEL), lambda i: (0, 0)),
        ],
        out_specs=[pl.BlockSpec((tm, D_MODEL), lambda i: (jnp.minimum(i, n_p_tiles - 1), 0)),
                   pl.BlockSpec((tm, D_MODEL), lambda i: (jnp.maximum(i - n_p_tiles, 0), 0))],
        out_shape=[jax.ShapeDtypeStruct((n_p, D_MODEL), F32),
                   jax.ShapeDtypeStruct((t_tok - n_p, D_MODEL), F32)],
        scratch_shapes=[pltpu.VMEM((2, TOP_K * tm, D_MODEL), F32), pltpu.SemaphoreType.DMA((2,))],
        compiler_params=_cparams(("arbitrary",)),
        name="moe_combine",
    )(dest, dest, out_rows, h, gates, g)


def _q_head_order():
    order = []
    for pair in range(2):
        for j in range(4):
            order += [8 * pair + j, 8 * pair + 4 + j]
    return order


def _q_col_perm():
    cols = []
    for h in _q_head_order():
        cols += list(range(h * HEAD_DIM, (h + 1) * HEAD_DIM))
    return jnp.asarray(cols, jnp.int32)


def _pad_rwkv_cols(m):
    def padw(a, w):
        return jnp.pad(a, [(0, 0)] * (a.ndim - 1) + [(0, w - a.shape[-1])])
    return jnp.concatenate([
        m[..., :XW_OFF],
        padw(m[..., XW_OFF:XW_OFF + DECAY_LORA], LANES),
        padw(m[..., XW_OFF + DECAY_LORA:XW_OFF + DECAY_LORA + ICLR_LORA], LANES),
        padw(m[..., XW_OFF + DECAY_LORA + ICLR_LORA:], XG_PAD),
    ], axis=-1)


def _unpad_rwkv_cols(m):
    return jnp.concatenate([
        m[..., :XW_OFF + DECAY_LORA],
        m[..., XA_OFF:XA_OFF + ICLR_LORA],
        m[..., XG_OFF:XG_OFF + GATE_LORA],
    ], axis=-1)


def _pad_rows(m, rows):
    return jnp.pad(m, ((0, rows - m.shape[0]), (0, 0)))


def _rope_tables(pos):
    half = HEAD_DIM // 2
    inv_freq = ROPE_THETA ** (-jnp.arange(half, dtype=F32) / half)
    ang = pos.astype(F32)[:, None] * inv_freq[None, :]
    cos = jnp.cos(ang)
    sin = jnp.sin(ang)
    cos_t = jnp.tile(jnp.concatenate([cos, cos], axis=-1), (1, LANES // HEAD_DIM))
    sin_t = jnp.tile(jnp.concatenate([-sin, sin], axis=-1), (1, LANES // HEAD_DIM))
    return cos_t, sin_t


def _route(top_idx, bm):
    n_tok = top_idx.shape[0]
    e_flat = top_idx.reshape(-1)
    onehot = (e_flat[:, None] == jnp.arange(N_EXPERTS, dtype=jnp.int32)[None, :]).astype(jnp.int32)
    csum = jnp.cumsum(onehot, axis=0)
    rank = jnp.sum(csum * onehot, axis=1) - 1
    counts = csum[-1]
    padded = ((counts + bm - 1) // bm) * bm
    p_end = jnp.cumsum(padded)
    p_start = p_end - padded
    dest = (p_start[e_flat] + rank).astype(jnp.int32)
    n_assign = n_tok * TOP_K
    n_blocks = -(-n_assign // bm) + N_EXPERTS
    n_rows = n_blocks * bm
    tok_flat = jnp.repeat(jnp.arange(n_tok, dtype=jnp.int32), TOP_K)
    row_tok = jnp.zeros((n_rows,), jnp.int32).at[dest].set(tok_flat, unique_indices=True)
    blk_start = jnp.arange(n_blocks, dtype=jnp.int32) * bm
    blk_expert = jnp.minimum(jnp.searchsorted(p_end, blk_start, side="right"), N_EXPERTS - 1).astype(jnp.int32)
    blk_valid = jnp.clip(counts[blk_expert] - (blk_start - p_start[blk_expert]), 0, bm).astype(jnp.int32)
    sub_start = jnp.arange(n_rows // MOE_SUB, dtype=jnp.int32) * MOE_SUB
    sub_expert = blk_expert[sub_start // bm]
    sub_valid = jnp.clip(counts[sub_expert] - (sub_start - p_start[sub_expert]), 0, MOE_SUB).astype(jnp.int32)
    n_real = (p_end[-1:] // bm).astype(jnp.int32)
    return dest, row_tok, blk_expert, blk_valid, sub_valid, n_real


def _pick(n, prefs):
    for p in prefs:
        if n % p == 0:
            return p
    return n


def kernel(x_prompt, x_sample, cache_k, cache_v, state_wkv, state_shift, attn_norm_g, w_in, attn_sinks, mu_shift, decay_w0, decay_lora_up, iclr_a0, iclr_lora_up, gate_lora_up, k_k, k_a, r_k, lnx_w, lnx_b, w_out, ffn_norm_g, router_w, router_b, w_up, b_up, w_down, b_down, final_norm_g):
    depth = w_in.shape[0]
    assert depth == 1
    batch, seq, _ = x_prompt.shape
    dec_b, dec_t, _ = x_sample.shape
    n_p = batch * seq
    n_s = dec_b * dec_t
    l = 0

    qperm = _q_col_perm()
    w_in_l = w_in[l]
    w_attn = jnp.concatenate([w_in_l[:, :ATTN_WIDTH][:, qperm], w_in_l[:, ATTN_WIDTH:ATTN_WIDTH + 2 * KV_WIDTH]],
                             axis=1).astype(BF16)
    w_rwkv = _pad_rwkv_cols(w_in_l[:, ATTN_WIDTH + 2 * KV_WIDTH:]).astype(BF16)
    w_oa = w_out[l][:ATTN_WIDTH][qperm].astype(BF16)
    w_or = w_out[l][ATTN_WIDTH:].astype(BF16)
    g_attn = attn_norm_g[l][None, :]
    g_ffn = ffn_norm_g[l][None, :]
    sinks_true = attn_sinks[l].astype(F32)
    seg = (jnp.arange(GROUP)[:, None] // HEAD_DIM == jnp.arange(GROUP)[None, :] // HEAD_DIM).astype(BF16)
    prm = dict(
        mu=_pad_rwkv_cols(mu_shift[l][None, :]),
        w0=decay_w0[l][None, :], wd=_pad_rows(decay_lora_up[l], LANES).astype(BF16),
        a0=iclr_a0[l][None, :], wa=_pad_rows(iclr_lora_up[l], LANES).astype(BF16),
        wg=_pad_rows(gate_lora_up[l], XG_PAD).astype(BF16),
        k_k=k_k[l][None, :], k_a=k_a[l][None, :], r_k=r_k[l].reshape(1, RWKV_WIDTH),
        lnx_w=lnx_w[l][None, :], lnx_b=lnx_b[l][None, :], seg=seg)
    rw_pad = jnp.pad(router_w[l], ((0, 0), (0, LANES - N_EXPERTS)))
    rw_hi = rw_pad.astype(BF16)
    rw_lo = (rw_pad - rw_hi.astype(F32)).astype(BF16)
    rw_split = jnp.stack([rw_hi, rw_lo])
    rb_pad = jnp.concatenate([router_b[l], jnp.full((LANES - N_EXPERTS,), NEG_BIG, F32)])[None, :]
    b_up3 = b_up[l][:, None, :]
    b_dn = b_down[l][:, None, :]
    tf = 256
    sel = (jnp.arange(2 * tf)[:, None] == 2 * jnp.arange(tf)[None, :]).astype(BF16)

    tm_p = _pick(seq, (512, 256, 128))
    cos_p, sin_p = _rope_tables(jnp.arange(seq, dtype=jnp.int32))
    q_p, k_p, v_p = _attn_proj(x_prompt.reshape(n_p, D_MODEL), g_attn, w_attn, cos_p, sin_p, tm_p)
    tm_s = _pick(n_s, (512, 256, 128, 64, 32, 16, 8))
    pos_s = PAST_LEN + (jnp.arange(n_s, dtype=jnp.int32) % dec_t)
    cos_s, sin_s = _rope_tables(pos_s)
    q_s, k_s, v_s = _attn_proj(x_sample.reshape(n_s, D_MODEL), g_attn, w_attn, cos_s, sin_s, tm_s)
    tn = RWKV_PROJ_PAD // 4
    pr_p = _rwkv_proj(x_prompt.reshape(n_p, D_MODEL), g_attn, w_rwkv, _pick(n_p, (1024, 512, 256, 128)), tn)
    pr_s = _rwkv_proj(x_sample.reshape(n_s, D_MODEL), g_attn, w_rwkv, tm_s, tn)

    oa_p = _attn_prompt(q_p, k_p, v_p, sinks_true, batch, seq)
    wb = cache_k.shape[2]
    sink_rows = jnp.stack([
        jnp.repeat(jnp.stack([sinks_true[8 * pair + 4 * e + j] for e in range(2) for j in range(4)]), dec_t)
        for pair in range(2)])[:, :, None]
    oa_s, nk_s, nv_s = _attn_sample(
        q_s.reshape(dec_b, dec_t, ATTN_WIDTH), k_s.reshape(dec_b, dec_t, KV_WIDTH),
        v_s.reshape(dec_b, dec_t, KV_WIDTH), cache_k[l].reshape(dec_b, wb, KV_WIDTH),
        cache_v[l].reshape(dec_b, wb, KV_WIDTH), sink_rows, _pick(dec_b, (8, 4, 2, 1)))

    zero_state = jnp.zeros((batch, RWKV_WIDTH // HEAD_DIM, HEAD_DIM, HEAD_DIM), F32)
    zero_shift = jnp.zeros((batch, 1, RWKV_PROJ_PAD), F32)
    or_p, st_p, sh_p = _rwkv_mix4(pr_p, zero_state, zero_shift, prm, 1, _pick(seq, (64, 32, 16, 8)),
                                  npar=1)
    seq_per_step = _pick(dec_b, (64 // dec_t, 1)) if 64 % dec_t == 0 else 1
    or_s, st_s, sh_s = _rwkv_mix4(pr_s, state_wkv[l], _pad_rwkv_cols(state_shift[l])[:, None, :], prm,
                                  seq_per_step, dec_t)

    n_all = n_p + n_s
    tm_o = _pick(math.gcd(n_p, n_s), (512, 256, 128, 64, 32, 16, 8))
    h, hn, idx_pad, gate_pad = _out_router(
        x_prompt.reshape(n_p, D_MODEL), x_sample.reshape(n_s, D_MODEL), oa_p, oa_s.reshape(n_s, ATTN_WIDTH),
        or_p, or_s, w_oa, w_or, g_ffn, rw_split, rb_pad, tm_o)

    bm = 1024 if n_all * TOP_K >= 1024 * N_EXPERTS else MOE_SUB
    dest, row_tok, blk_expert, blk_valid, sub_valid, n_real = _route(idx_pad[:, :TOP_K], bm)
    gather_rows = 2 * MOE_SUB if bm % (2 * MOE_SUB) == 0 else MOE_SUB
    step_valid = sub_valid.reshape(-1, gather_rows // MOE_SUB).sum(axis=1)
    x_sorted = _gather_rows(step_valid, row_tok, hn, gather_rows)
    out_rows = _experts(blk_expert, blk_valid, n_real, x_sorted, w_up[l], b_up3, w_down[l], b_dn, sel, bm, tf)
    y_p, y_s = _combine(dest, out_rows, h, gate_pad, final_norm_g[None, :], n_p,
                        _pick(math.gcd(n_p, n_s), (256, 128, 64, 32, 16, 8)))

    y_prompt = y_p.reshape(batch, seq, D_MODEL)
    y_sample = y_s.reshape(dec_b, dec_t, D_MODEL)
    kp = k_p.reshape(batch, seq, 4, HEAD_DIM)[:, -WINDOW:][None]
    vp = v_p.reshape(batch, seq, 4, HEAD_DIM)[:, -WINDOW:][None]
    wp = st_p[None]
    sp = _unpad_rwkv_cols(sh_p[:, 0, :])[None]
    ks = nk_s.reshape(dec_b, wb, 4, HEAD_DIM)[None]
    vs = nv_s.reshape(dec_b, wb, 4, HEAD_DIM)[None]
    ws = st_s[None]
    ss = _unpad_rwkv_cols(sh_s[:, 0, :])[None]
    return (y_prompt, y_sample, kp, vp, wp, sp, ks, vs, ws, ss)
```

```python
import functools
import math

import jax
import jax.numpy as jnp
from jax import lax
from jax.experimental import pallas as pl
from jax.experimental.pallas import tpu as pltpu

F32 = jnp.float32
BF16 = jnp.bfloat16

D_MODEL = 2048
HEAD_DIM = 64
LANES = 128
ATTN_WIDTH = 1024
KV_WIDTH = 256
ATTN_HEADS = 16
RWKV_WIDTH = 1024
DECAY_LORA = 64
ICLR_LORA = 64
GATE_LORA = 160
RWKV_PROJ = 3 * RWKV_WIDTH + DECAY_LORA + ICLR_LORA + GATE_LORA
XW_OFF = 3 * RWKV_WIDTH
XA_OFF = XW_OFF + LANES
XG_OFF = XA_OFF + LANES
XG_PAD = 2 * LANES
RWKV_PROJ_PAD = XG_OFF + XG_PAD
WINDOW = 128
ROPE_THETA = 10000.0
PAST_LEN = 8192
N_EXPERTS = 32
TOP_K = 4
D_FF = 2048
SWIGLU_ALPHA = 1.702
SWIGLU_LIMIT = 7.0
NORM_EPS = 1e-5
LNX_EPS = 64e-5
NEG_BIG = -1e30
VMEM_LIMIT = 56 * 1024 * 1024


def _cparams(sem):
    return pltpu.CompilerParams(dimension_semantics=sem, vmem_limit_bytes=VMEM_LIMIT)


_NN = (((1,), (0,)), ((), ()))
_NT = (((1,), (1,)), ((), ()))
_TN = (((0,), (0,)), ((), ()))


def _mm(a, b, dims=_NN):
    return lax.dot_general(a, b, dims, preferred_element_type=F32)


def _split2(x):
    hi = x.astype(BF16)
    lo = (x - hi.astype(F32)).astype(BF16)
    return hi, lo


def _split3(x):
    hi = x.astype(BF16)
    r1 = x - hi.astype(F32)
    mid = r1.astype(BF16)
    lo = (r1 - mid.astype(F32)).astype(BF16)
    return hi, mid, lo


def _mm_exact_b(a, b_bf16, dims=_NN):
    h, m, l = _split3(a)
    return _mm(h, b_bf16, dims) + (_mm(m, b_bf16, dims) + _mm(l, b_bf16, dims))


def _mm_exact_a(a_bf16, b, dims=_NN):
    h, m, l = _split3(b)
    return _mm(a_bf16, h, dims) + (_mm(a_bf16, m, dims) + _mm(a_bf16, l, dims))


def _rms(x, g):
    ms = jnp.mean(x * x, axis=-1, keepdims=True)
    return (x * lax.rsqrt(ms + NORM_EPS)) * g


def _attn_proj_kernel(x_ref, g_ref, w_ref, cos_ref, sin_ref, q_ref, k_ref, v_ref):
    xn = _rms(x_ref[...], g_ref[...]).astype(BF16)
    a = _mm(xn, w_ref[...])
    cos = cos_ref[...]
    sin = sin_ref[...]
    lane = lax.broadcasted_iota(jnp.int32, cos.shape, 1)
    first_half = (lane % HEAD_DIM) < (HEAD_DIM // 2)
    n_rot = (ATTN_WIDTH + KV_WIDTH) // LANES
    for j in range(n_rot):
        t = a[:, j * LANES:(j + 1) * LANES]
        swapped = jnp.where(first_half, pltpu.roll(t, LANES - HEAD_DIM // 2, 1), pltpu.roll(t, HEAD_DIM // 2, 1))
        rot = t * cos + swapped * sin
        if j < ATTN_WIDTH // LANES:
            q_ref[:, j * LANES:(j + 1) * LANES] = (rot * (HEAD_DIM ** -0.5)).astype(BF16)
        else:
            jj = j - ATTN_WIDTH // LANES
            k_ref[:, jj * LANES:(jj + 1) * LANES] = rot
    v_ref[...] = a[:, ATTN_WIDTH + KV_WIDTH:]


def _attn_proj(x, g, w_attn, cos_t, sin_t, tm):
    t_tok = x.shape[0]
    n_pos_blocks = cos_t.shape[0] // tm
    return pl.pallas_call(
        _attn_proj_kernel,
        grid=(t_tok // tm,),
        in_specs=[
            pl.BlockSpec((tm, D_MODEL), lambda i: (i, 0)),
            pl.BlockSpec((1, D_MODEL), lambda i: (0, 0)),
            pl.BlockSpec((D_MODEL, ATTN_WIDTH + 2 * KV_WIDTH), lambda i: (0, 0)),
            pl.BlockSpec((tm, LANES), lambda i: (i % n_pos_blocks, 0)),
            pl.BlockSpec((tm, LANES), lambda i: (i % n_pos_blocks, 0)),
        ],
        out_specs=[
            pl.BlockSpec((tm, ATTN_WIDTH), lambda i: (i, 0)),
            pl.BlockSpec((tm, KV_WIDTH), lambda i: (i, 0)),
            pl.BlockSpec((tm, KV_WIDTH), lambda i: (i, 0)),
        ],
        out_shape=[
            jax.ShapeDtypeStruct((t_tok, ATTN_WIDTH), BF16),
            jax.ShapeDtypeStruct((t_tok, KV_WIDTH), F32),
            jax.ShapeDtypeStruct((t_tok, KV_WIDTH), F32),
        ],
        compiler_params=_cparams(("parallel",)),
        name="attn_proj",
    )(x, g, w_attn, cos_t, sin_t)


def _rwkv_proj_kernel(x_ref, g_ref, w_ref, p_ref, xn_ref):
    @pl.when(pl.program_id(1) == 0)
    def _():
        xn_ref[...] = _rms(x_ref[...], g_ref[...]).astype(BF16)

    p_ref[...] = _mm(xn_ref[...], w_ref[...])


def _rwkv_proj(x, g, w_rwkv, tm, tn):
    t_tok = x.shape[0]
    return pl.pallas_call(
        _rwkv_proj_kernel,
        grid=(t_tok // tm, RWKV_PROJ_PAD // tn),
        in_specs=[
            pl.BlockSpec((tm, D_MODEL), lambda i, n: (i, 0)),
            pl.BlockSpec((1, D_MODEL), lambda i, n: (0, 0)),
            pl.BlockSpec((D_MODEL, tn), lambda i, n: (0, n)),
        ],
        out_specs=pl.BlockSpec((tm, tn), lambda i, n: (i, n)),
        out_shape=jax.ShapeDtypeStruct((t_tok, RWKV_PROJ_PAD), F32),
        scratch_shapes=[pltpu.VMEM((tm, D_MODEL), BF16)],
        compiler_params=_cparams(("parallel", "arbitrary")),
        name="rwkv_proj",
    )(x, g, w_rwkv)


def _softmax_pv(s, mask, sink, vt):
    s = jnp.where(mask, s, NEG_BIG)
    m = jnp.maximum(jnp.max(s, axis=-1, keepdims=True), sink)
    p = jnp.exp(s - m)
    denom = jnp.sum(p, axis=-1, keepdims=True) + jnp.exp(sink - m)
    return _mm(p.astype(BF16), vt) * (1.0 / denom)


def _attn_prompt_kernel(q_ref, kp_ref, kc_ref, vp_ref, vc_ref, sink_ref, o_ref):
    i = pl.program_id(1)
    blk = q_ref.shape[0]
    kk = jnp.concatenate([kp_ref[...], kc_ref[...]], axis=0).astype(BF16)
    vv = jnp.concatenate([vp_ref[...], vc_ref[...]], axis=0).astype(BF16)
    a = lax.broadcasted_iota(jnp.int32, (blk, 2 * blk), 0)
    c = lax.broadcasted_iota(jnp.int32, (blk, 2 * blk), 1)
    mask = (c > a) & (c <= a + blk) & ((c >= blk) | (i > 0))
    lane_lo = lax.broadcasted_iota(jnp.int32, (blk, LANES), 1) < HEAD_DIM
    n_pairs = KV_WIDTH // LANES
    kts = [kk[:, pair * LANES:(pair + 1) * LANES] for pair in range(n_pairs)]
    vts = [vv[:, pair * LANES:(pair + 1) * LANES] for pair in range(n_pairs)]
    heads = [(pair, j, e) for pair in range(n_pairs) for j in range(4) for e in range(2)]
    scores = []
    for pair, j, e in heads:
        tile = pair * 4 + j
        qt = q_ref[:, tile * LANES:(tile + 1) * LANES]
        qm = jnp.where(lane_lo if e == 0 else jnp.logical_not(lane_lo), qt, jnp.zeros_like(qt))
        scores.append(_mm(qm, kts[pair], _NT))
    probs, scales = [], []
    for s, (pair, j, e) in zip(scores, heads):
        sink = sink_ref[8 * pair + 4 * e + j]
        s = jnp.where(mask, s, NEG_BIG)
        m = jnp.maximum(jnp.max(s, axis=-1, keepdims=True), sink)
        p = jnp.exp(s - m)
        probs.append(p.astype(BF16))
        scales.append(1.0 / (jnp.sum(p, axis=-1, keepdims=True) + jnp.exp(sink - m)))
    outs = [_mm(p, vts[pair]) * sc for p, sc, (pair, j, e) in zip(probs, scales, heads)]
    for tile in range(ATTN_WIDTH // LANES):
        o_ref[:, tile * LANES:(tile + 1) * LANES] = jnp.where(lane_lo, outs[2 * tile], outs[2 * tile + 1]).astype(BF16)


def _attn_prompt(q, k, v, sinks, batch, seq):
    blk = WINDOW
    nb = seq // blk
    cur = lambda b, i: (b * nb + i, 0)
    prev = lambda b, i: (b * nb + jnp.maximum(i - 1, 0), 0)
    return pl.pallas_call(
        _attn_prompt_kernel,
        grid=(batch, nb),
        in_specs=[
            pl.BlockSpec((blk, ATTN_WIDTH), cur),
            pl.BlockSpec((blk, KV_WIDTH), prev),
            pl.BlockSpec((blk, KV_WIDTH), cur),
            pl.BlockSpec((blk, KV_WIDTH), prev),
            pl.BlockSpec((blk, KV_WIDTH), cur),
            pl.BlockSpec(memory_space=pltpu.SMEM),
        ],
        out_specs=pl.BlockSpec((blk, ATTN_WIDTH), cur),
        out_shape=jax.ShapeDtypeStruct((batch * seq, ATTN_WIDTH), BF16),
        compiler_params=_cparams(("parallel", "arbitrary")),
        name="attn_prompt",
    )(q, k, k, v, v, sinks)


def _attn_sample_kernel(q_ref, kn_ref, vn_ref, ck_ref, cv_ref, sink_ref, o_ref, nk_ref, nv_ref):
    bb, t_len, _ = q_ref.shape
    wb = ck_ref.shape[1]
    n_keys = wb + t_len
    rows = 8 * t_len
    a = lax.broadcasted_iota(jnp.int32, (rows, n_keys), 0) % t_len
    c = lax.broadcasted_iota(jnp.int32, (rows, n_keys), 1)
    mask = ((c < wb) & (c > a + (wb - WINDOW))) | ((c >= wb) & (c - wb <= a))
    lane_lo = lax.broadcasted_iota(jnp.int32, (t_len, LANES), 1) < HEAD_DIM

    def body(b, carry):
        ck = ck_ref[b]
        cv = cv_ref[b]
        kn = kn_ref[b]
        vn = vn_ref[b]
        nk_ref[b, 0:wb - t_len, :] = ck[t_len:, :]
        nk_ref[b, wb - t_len:wb, :] = kn
        nv_ref[b, 0:wb - t_len, :] = cv[t_len:, :]
        nv_ref[b, wb - t_len:wb, :] = vn
        k_all = jnp.concatenate([ck, kn], axis=0).astype(BF16)
        v_all = jnp.concatenate([cv, vn], axis=0).astype(BF16)
        qb = q_ref[b]
        pairs = range(KV_WIDTH // LANES)
        scores = []
        for pair in pairs:
            stack = []
            for e in range(2):
                for j in range(4):
                    tile = pair * 4 + j
                    qt = qb[:, tile * LANES:(tile + 1) * LANES]
                    stack.append(jnp.where(lane_lo if e == 0 else jnp.logical_not(lane_lo), qt, jnp.zeros_like(qt)))
            qs = jnp.concatenate(stack, axis=0)
            scores.append(_mm(qs, k_all[:, pair * LANES:(pair + 1) * LANES], _NT))
        outs = [_softmax_pv(scores[pair], mask, sink_ref[pair], v_all[:, pair * LANES:(pair + 1) * LANES])
                for pair in pairs]
        for pair in pairs:
            o = outs[pair]
            for j in range(4):
                tile = pair * 4 + j
                lo = o[j * t_len:(j + 1) * t_len, :]
                hi = o[(4 + j) * t_len:(5 + j) * t_len, :]
                o_ref[b, :, tile * LANES:(tile + 1) * LANES] = jnp.where(lane_lo, lo, hi).astype(BF16)
        return carry

    lax.fori_loop(0, bb, body, 0)


def _attn_sample(q3, k3, v3, ck, cv, sink_rows, bb):
    dec_b, t_len, _ = q3.shape
    wb = ck.shape[1]
    blk3 = lambda w: pl.BlockSpec((bb, t_len, w), lambda i: (i, 0, 0))
    cache = pl.BlockSpec((bb, wb, KV_WIDTH), lambda i: (i, 0, 0))
    return pl.pallas_call(
        _attn_sample_kernel,
        grid=(dec_b // bb,),
        in_specs=[blk3(ATTN_WIDTH), blk3(KV_WIDTH), blk3(KV_WIDTH), cache, cache,
                  pl.BlockSpec((2, 8 * t_len, 1), lambda i: (0, 0, 0))],
        out_specs=[blk3(ATTN_WIDTH), cache, cache],
        out_shape=[
            jax.ShapeDtypeStruct((dec_b, t_len, ATTN_WIDTH), BF16),
            jax.ShapeDtypeStruct(ck.shape, F32),
            jax.ShapeDtypeStruct(cv.shape, F32),
        ],
        compiler_params=_cparams(("parallel",)),
        name="attn_sample",
    )(q3, k3, v3, ck, cv, sink_rows)


GROUP = 4 * HEAD_DIM


def _bf(x):
    return x.astype(BF16)


def _seg_sum4(x, seg_mat):
    outs = []
    for t in range(x.shape[1] // GROUP):
        outs.append(_mm_exact_b(x[:, t * GROUP:(t + 1) * GROUP], seg_mat))
    return jnp.concatenate(outs, axis=1)


def _stack_heads(x, head_masks):
    zero = jnp.zeros_like(x)
    return jnp.concatenate([jnp.where(m, x, zero) for m in head_masks], axis=0)


def _sum_blocks(x, r):
    return (x[0:r] + x[r:2 * r]) + (x[2 * r:3 * r] + x[3 * r:4 * r])


def _place_heads(blocks):
    rows = []
    for h, blk in enumerate(blocks):
        rows.append(jnp.concatenate([blk if j == h else jnp.zeros_like(blk) for j in range(len(blocks))], axis=1))
    return jnp.concatenate(rows, axis=0)


def _rwkv4_kernel(p_ref, s_in_ref, sh_ref, mu_ref, w0_ref, wd_ref, a0_ref, wa_ref, wg_ref, kk_ref, ka_ref,
                  rk_ref, lnw_ref, lnb_ref, seg_ref, o_ref, s_out_ref, sh_out_ref, state_scr, prev_scr,
                  *, nseq, tlen, npar):
    rows = nseq * tlen
    heads = GROUP // HEAD_DIM
    log_t = tlen.bit_length() - 1
    log_r = rows.bit_length() - 1
    log_h = HEAD_DIM.bit_length() - 1

    ri = lax.broadcasted_iota(jnp.int32, (rows, rows), 0)
    cj = lax.broadcasted_iota(jnp.int32, (rows, rows), 1)
    bri = lax.broadcasted_iota(jnp.int32, (heads * rows, heads * rows), 0)
    bcj = lax.broadcasted_iota(jnp.int32, (heads * rows, heads * rows), 1)
    same = ((bri >> log_r) == (bcj >> log_r)) & ((bri >> log_t) == (bcj >> log_t))
    lane = lax.broadcasted_iota(jnp.int32, (rows, GROUP), 1)
    sr = lax.broadcasted_iota(jnp.int32, (GROUP, GROUP), 0)
    sc = lax.broadcasted_iota(jnp.int32, (GROUP, GROUP), 1)
    masks = dict(
        tri=jnp.where(((ri >> log_t) == (cj >> log_t)) & (ri >= cj), 1.0, 0.0).astype(BF16),
        strict_bd=same & (bri > bcj),
        incl_bd=same & (bri >= bcj),
        eye_bd=jnp.where(bri == bcj, 1.0, 0.0).astype(F32),
        head_masks=[(lane >> log_h) == h for h in range(heads)],
        state_bd=(sr >> log_h) == (sc >> log_h),
    )
    refs = (p_ref, s_in_ref, sh_ref, mu_ref, w0_ref, wd_ref, a0_ref, wa_ref, wg_ref, kk_ref, ka_ref, rk_ref,
            lnw_ref, lnb_ref, seg_ref, o_ref, s_out_ref, sh_out_ref, state_scr, prev_scr)
    for j in range(npar):
        _rwkv4_block(j, refs, masks, nseq=nseq, tlen=tlen)


def _rwkv4_block(j, refs, masks, *, nseq, tlen):
    (p_ref, s_in_ref, sh_ref, mu_ref, w0_ref, wd_ref, a0_ref, wa_ref, wg_ref, kk_ref, ka_ref, rk_ref,
     lnw_ref, lnb_ref, seg_ref, o_ref, s_out_ref, sh_out_ref, state_scr, prev_scr) = refs
    ci = pl.program_id(1)
    rows = nseq * tlen
    n_groups = RWKV_WIDTH // GROUP
    heads = GROUP // HEAD_DIM
    log_t = tlen.bit_length() - 1
    carry = nseq == 1

    p = p_ref[j] if carry else p_ref[...]
    rowi = lax.broadcasted_iota(jnp.int32, (rows, 1), 0)
    rolled = pltpu.roll(p, 1, 0)
    if carry:
        @pl.when(ci == 0)
        def _():
            prev_scr[j] = sh_ref[j]
            for g in range(n_groups):
                state_scr[j * n_groups + g] = _place_heads([s_in_ref[j, heads * g + h] for h in range(heads)])

        p_prev = jnp.where(rowi == 0, prev_scr[j], rolled)
        prev_scr[j] = p[rows - 1:rows, :]
    else:
        p_prev = rolled
        for s in range(nseq):
            p_prev = jnp.where(rowi == s * tlen, sh_ref[s], p_prev)
            sh_out_ref[s] = p[(s + 1) * tlen - 1:(s + 1) * tlen, :]

    xs = p + (p_prev - p) * mu_ref[...]
    r = xs[:, 0:RWKV_WIDTH]
    k = xs[:, RWKV_WIDTH:2 * RWKV_WIDTH]
    v = xs[:, 2 * RWKV_WIDTH:3 * RWKV_WIDTH]
    xw = xs[:, XW_OFF:XW_OFF + LANES]
    xa = xs[:, XA_OFF:XA_OFF + LANES]
    xg = xs[:, XG_OFF:XG_OFF + XG_PAD]
    z = w0_ref[...] + _mm(_bf(jnp.tanh(xw)), wd_ref[...])
    w_log = -jax.nn.softplus(-z) - 0.5
    lw = -jnp.exp(w_log)
    a = jax.nn.sigmoid(a0_ref[...] + _mm(_bf(xa), wa_ref[...]))
    gate = _mm(_bf(jax.nn.sigmoid(xg)), wg_ref[...])
    seg = seg_ref[...]
    kk = k * kk_ref[...]
    kap = kk / jnp.maximum(jnp.sqrt(_seg_sum4(kk * kk, seg)), 1e-12)
    k2 = k * (1.0 + (a - 1.0) * ka_ref[...])
    b = kap * a
    bonus = _seg_sum4(r * k2 * rk_ref[...], seg) * v

    cum = _mm_exact_a(masks["tri"], lw)
    eg = jnp.exp(cum)
    kap_t = kap * jnp.exp(cum - lw)
    r_t = r * eg
    einv = jnp.exp(-cum)
    b_t = b * einv
    k_t = k2 * einv

    strict_bd, incl_bd, eye_bd = masks["strict_bd"], masks["incl_bd"], masks["eye_bd"]
    head_masks, state_bd = masks["head_masks"], masks["state_bd"]
    zero_bd = jnp.zeros((heads * rows, heads * rows), F32)
    zero_st = jnp.zeros((GROUP, GROUP), F32)

    groups = range(n_groups)
    hr = heads * rows
    sls = [slice(g * GROUP, (g + 1) * GROUP) for g in groups]
    kap_g = [kap_t[:, sl] for sl in sls]
    r_g = [r_t[:, sl] for sl in sls]
    b_g = [b_t[:, sl] for sl in sls]
    k_g = [k_t[:, sl] for sl in sls]
    v_g = [v[:, sl] for sl in sls]
    lkr = [jnp.concatenate([_stack_heads(_bf(kap_g[g]), head_masks), _stack_heads(_bf(r_g[g]), head_masks)], axis=0)
           for g in groups]
    gb = [_mm(lkr[g], _stack_heads(_bf(b_g[g]), head_masks), _NT) for g in groups]
    gk = [_mm(lkr[g], _stack_heads(_bf(k_g[g]), head_masks), _NT) for g in groups]
    n = [-jnp.where(strict_bd, gb[g][:hr], zero_bd) for g in groups]
    t_inv = [eye_bd + n[g] for g in groups]
    for _ in range(max(log_t - 1, 0)):
        nb = [_bf(n[g]) for g in groups]
        n = [_mm(nb[g], nb[g]) for g in groups]
        t_inv = [t_inv[g] + _mm(_bf(t_inv[g]), _bf(n[g])) for g in groups]
    a_k = [_bf(jnp.where(strict_bd, gk[g][:hr], zero_bd)) for g in groups]
    r_b = [_bf(jnp.where(incl_bd, gb[g][hr:], zero_bd)) for g in groups]
    r_k = [_bf(jnp.where(incl_bd, gk[g][hr:], zero_bd)) for g in groups]
    v_stack = [_stack_heads(_bf(v_g[g]), head_masks) for g in groups]

    if carry:
        s0 = [state_scr[j * n_groups + g] for g in groups]
        sk = [_mm(_bf(jnp.concatenate([kap_g[g], r_g[g]], axis=0)), _bf(s0[g]), _NT) for g in groups]
        k_s = [sk[g][:rows] for g in groups]
        r_s = [sk[g][rows:] for g in groups]
    else:
        s0, k_s, r_s = [], [], []
        for g in groups:
            s0_list, ks_list, rs_list = [], [], []
            for s in range(nseq):
                rs = slice(s * tlen, (s + 1) * tlen)
                st = _place_heads([s_in_ref[s, heads * g + h] for h in range(heads)])
                sk = _mm(_bf(jnp.concatenate([kap_g[g][rs], r_g[g][rs]], axis=0)), _bf(st), _NT)
                s0_list.append(st)
                ks_list.append(sk[:tlen])
                rs_list.append(sk[tlen:])
            s0.append(s0_list)
            k_s.append(jnp.concatenate(ks_list, axis=0))
            r_s.append(jnp.concatenate(rs_list, axis=0))

    w = [_sum_blocks(_mm(a_k[g], v_stack[g]), rows) for g in groups]
    rhs = [-(k_s[g] + w[g]) for g in groups]
    u_stack = [_mm(_bf(t_inv[g]), _stack_heads(_bf(rhs[g]), head_masks)) for g in groups]
    u = [_sum_blocks(u_stack[g], rows) for g in groups]
    y_stack = [_mm(r_b[g], _bf(u_stack[g])) + _mm(r_k[g], v_stack[g]) for g in groups]
    ys = [r_s[g] + _sum_blocks(y_stack[g], rows) for g in groups]

    for g in groups:
        if carry:
            d = _mm(_bf(jnp.concatenate([u[g], v_g[g]], axis=0)), _bf(jnp.concatenate([b_g[g], k_g[g]], axis=0)), _TN)
            state_scr[j * n_groups + g] = (s0[g] + jnp.where(state_bd, d, zero_st)) * eg[rows - 1:rows, sls[g]]
        else:
            for s in range(nseq):
                rs = slice(s * tlen, (s + 1) * tlen)
                d = _mm(_bf(jnp.concatenate([u[g][rs], v_g[g][rs]], axis=0)),
                        _bf(jnp.concatenate([b_g[g][rs], k_g[g][rs]], axis=0)), _TN)
                s_new = (s0[g][s] + jnp.where(state_bd, d, zero_st)) * eg[(s + 1) * tlen - 1:(s + 1) * tlen, sls[g]]
                for h in range(heads):
                    hs = slice(h * HEAD_DIM, (h + 1) * HEAD_DIM)
                    s_out_ref[s, heads * g + h] = s_new[hs, hs]

    y = jnp.concatenate(ys, axis=1)
    inv_n = 1.0 / HEAD_DIM
    mean = _seg_sum4(y, seg) * inv_n
    yc = y - mean
    var = _seg_sum4(yc * yc, seg) * inv_n
    yn = yc * lax.rsqrt(var + LNX_EPS) * lnw_ref[...] + lnb_ref[...]
    out = ((yn + bonus) * gate).astype(o_ref.dtype)
    if carry:
        o_ref[j] = out

        @pl.when(ci == pl.num_programs(1) - 1)
        def _():
            sh_out_ref[j] = prev_scr[j]
            for g in range(n_groups):
                st = state_scr[j * n_groups + g]
                for h in range(heads):
                    hs = slice(h * HEAD_DIM, (h + 1) * HEAD_DIM)
                    s_out_ref[j, heads * g + h] = st[hs, hs]
    else:
        o_ref[...] = out


def _rwkv_mix4(p2, s_in, shift3, prm, nseq, tlen, npar=1):
    batch = s_in.shape[0]
    t_len = p2.shape[0] // batch
    n_chunks = t_len // tlen
    rows = nseq * tlen
    n_heads = RWKV_WIDTH // HEAD_DIM
    per_step = nseq * npar
    const2 = lambda shape: pl.BlockSpec(shape, lambda b, c: (0, 0))
    vec = const2((1, RWKV_WIDTH))
    state_spec = pl.BlockSpec((per_step, n_heads, HEAD_DIM, HEAD_DIM), lambda b, c: (b, 0, 0, 0))
    shift_spec = pl.BlockSpec((per_step, 1, RWKV_PROJ_PAD), lambda b, c: (b, 0, 0))
    if nseq == 1:
        p_in = p2.reshape(batch, t_len, RWKV_PROJ_PAD)
        p_spec = pl.BlockSpec((npar, tlen, RWKV_PROJ_PAD), lambda b, c: (b, c, 0))
        o_spec = pl.BlockSpec((npar, tlen, RWKV_WIDTH), lambda b, c: (b, c, 0))
        o_shape = jax.ShapeDtypeStruct((batch, t_len, RWKV_WIDTH), BF16)
    else:
        p_in = p2
        p_spec = pl.BlockSpec((rows, RWKV_PROJ_PAD), lambda b, c: (b * n_chunks + c, 0))
        o_spec = pl.BlockSpec((rows, RWKV_WIDTH), lambda b, c: (b * n_chunks + c, 0))
        o_shape = jax.ShapeDtypeStruct((batch * t_len, RWKV_WIDTH), BF16)
    o, s_out, sh_out = pl.pallas_call(
        functools.partial(_rwkv4_kernel, nseq=nseq, tlen=tlen, npar=npar),
        grid=(batch // per_step, n_chunks),
        in_specs=[
            p_spec,
            state_spec,
            shift_spec,
            const2((1, RWKV_PROJ_PAD)),
            vec,
            const2((LANES, RWKV_WIDTH)),
            vec,
            const2((LANES, RWKV_WIDTH)),
            const2((XG_PAD, RWKV_WIDTH)),
            vec, vec, vec, vec, vec,
            const2((GROUP, GROUP)),
        ],
        out_specs=[o_spec, state_spec, shift_spec],
        out_shape=[
            o_shape,
            jax.ShapeDtypeStruct(s_in.shape, F32),
            jax.ShapeDtypeStruct(shift3.shape, F32),
        ],
        scratch_shapes=[pltpu.VMEM((npar * (RWKV_WIDTH // GROUP), GROUP, GROUP), F32),
                        pltpu.VMEM((npar, 1, RWKV_PROJ_PAD), F32)],
        compiler_params=_cparams(("parallel", "arbitrary")),
        name="rwkv_mix_t%d" % tlen,
    )(p_in, s_in, shift3, prm["mu"], prm["w0"], prm["wd"], prm["a0"], prm["wa"], prm["wg"], prm["k_k"],
      prm["k_a"], prm["r_k"], prm["lnx_w"], prm["lnx_b"], prm["seg"])
    return o.reshape(batch * t_len, RWKV_WIDTH), s_out, sh_out


def _out_router_kernel(xp_ref, xs_ref, oap_ref, oas_ref, orp_ref, ors_ref, wa_ref, wr_ref, g_ref, rw_ref, rb_ref,
                       h_ref, hn_ref, idx_ref, gate_ref, *, n_p_tiles):
    body = functools.partial(_out_router_body, wa_ref=wa_ref, wr_ref=wr_ref, g_ref=g_ref, rw_ref=rw_ref,
                             rb_ref=rb_ref, h_ref=h_ref, hn_ref=hn_ref, idx_ref=idx_ref, gate_ref=gate_ref)
    is_prompt = pl.program_id(0) < n_p_tiles
    pl.when(is_prompt)(functools.partial(body, xp_ref, oap_ref, orp_ref))
    pl.when(jnp.logical_not(is_prompt))(functools.partial(body, xs_ref, oas_ref, ors_ref))


def _out_router_body(x_ref, oa_ref, or_ref, *, wa_ref, wr_ref, g_ref, rw_ref, rb_ref, h_ref, hn_ref, idx_ref,
                     gate_ref):
    h = x_ref[...] + _mm(oa_ref[...], wa_ref[...]) + _mm(or_ref[...], wr_ref[...])
    h_ref[...] = h
    hn = _rms(h, g_ref[...])
    hh, hl = _split2(hn)
    bits = lax.bitcast_convert_type(hh.astype(F32), jnp.int32)
    half = D_MODEL // 2
    hn_ref[...] = ((bits[:, :half] >> 16) & jnp.int32(0xFFFF)) | (bits[:, half:] & jnp.int32(-65536))
    logits = _mm(hh, rw_ref[0]) + (_mm(hh, rw_ref[1]) + _mm(hl, rw_ref[0])) + rb_ref[...]
    lane = lax.broadcasted_iota(jnp.int32, logits.shape, 1)
    vals = []
    idxs = []
    cur = logits
    for _ in range(TOP_K):
        m = jnp.max(cur, axis=-1, keepdims=True)
        sel = jnp.min(jnp.where(cur == m, lane, LANES), axis=-1, keepdims=True)
        vals.append(m)
        idxs.append(sel)
        cur = jnp.where(lane == sel, -jnp.inf, cur)
    es = [jnp.exp(vj - vals[0]) for vj in vals]
    tot = es[0] + es[1] + es[2] + es[3]
    idx_out = jnp.zeros(logits.shape, jnp.int32)
    gate_out = jnp.zeros(logits.shape, F32)
    for j in range(TOP_K):
        idx_out = jnp.where(lane == j, idxs[j], idx_out)
        gate_out = jnp.where(lane == j, es[j] / tot, gate_out)
    idx_ref[...] = idx_out
    gate_ref[...] = gate_out


def _out_router(x_p, x_s, oa_p, oa_s, or_p, or_s, w_oa, w_or, g, rw_split, rb_pad, tm):
    n_p_tiles = x_p.shape[0] // tm
    t_tok = x_p.shape[0] + x_s.shape[0]
    row = lambda w: pl.BlockSpec((tm, w), lambda i: (i, 0))
    row_p = lambda w: pl.BlockSpec((tm, w), lambda i: (jnp.minimum(i, n_p_tiles - 1), 0))
    row_s = lambda w: pl.BlockSpec((tm, w), lambda i: (jnp.maximum(i - n_p_tiles, 0), 0))
    full = lambda shape: pl.BlockSpec(shape, lambda i: tuple(0 for _ in shape))
    return pl.pallas_call(
        functools.partial(_out_router_kernel, n_p_tiles=n_p_tiles),
        grid=(t_tok // tm,),
        in_specs=[row_p(D_MODEL), row_s(D_MODEL), row_p(ATTN_WIDTH), row_s(ATTN_WIDTH), row_p(RWKV_WIDTH),
                  row_s(RWKV_WIDTH), full((ATTN_WIDTH, D_MODEL)), full((RWKV_WIDTH, D_MODEL)), full((1, D_MODEL)),
                  full((2, D_MODEL, LANES)), full((1, LANES))],
        out_specs=[row(D_MODEL), row(D_MODEL // 2), row(LANES), row(LANES)],
        out_shape=[
            jax.ShapeDtypeStruct((t_tok, D_MODEL), F32),
            jax.ShapeDtypeStruct((t_tok, D_MODEL // 2), jnp.int32),
            jax.ShapeDtypeStruct((t_tok, LANES), jnp.int32),
            jax.ShapeDtypeStruct((t_tok, LANES), F32),
        ],
        compiler_params=_cparams(("arbitrary",)),
        name="out_router",
    )(x_p, x_s, oa_p, oa_s, or_p, or_s, w_oa, w_or, g, rw_split, rb_pad)


GATHER_UNROLL = 8
MOE_SUB = 128
GATHER_ROWS = 512


def _gather_kernel(nv_ref, tok_ref, tok_next_ref, hn_ref, o_ref, buf, sem):
    rows = buf.shape[1]
    i = pl.program_id(0)
    last = pl.num_programs(0) - 1
    slot = i % 2

    def issue(tok, dst_slot):
        def body(r8, carry):
            for u in range(GATHER_UNROLL):
                r = r8 * GATHER_UNROLL + u
                pltpu.make_async_copy(hn_ref.at[pl.ds(tok[r], 1)], buf.at[dst_slot, pl.ds(r, 1)],
                                      sem.at[dst_slot]).start()
            return carry

        lax.fori_loop(0, rows // GATHER_UNROLL, body, 0)

    @pl.when((i == 0) & (nv_ref[0] > 0))
    def _():
        issue(tok_ref, 0)

    @pl.when((i < last) & (nv_ref[jnp.minimum(i + 1, last)] > 0))
    def _():
        issue(tok_next_ref, 1 - slot)

    @pl.when(nv_ref[i] > 0)
    def _():
        pltpu.make_async_copy(hn_ref.at[pl.ds(0, rows)], buf.at[slot], sem.at[slot]).wait()
        words = buf[slot]
        half = words.shape[1]
        o_ref[:, :half] = lax.bitcast_convert_type(words << 16, F32).astype(o_ref.dtype)
        o_ref[:, half:] = lax.bitcast_convert_type(words & jnp.int32(-65536), F32).astype(o_ref.dtype)

    @pl.when(nv_ref[i] == 0)
    def _():
        o_ref[...] = jnp.zeros_like(o_ref)


def _gather_rows(sub_valid, row_tok, hn, rows_per_step):
    n_rows = row_tok.shape[0]
    n_steps = n_rows // rows_per_step
    grid_spec = pltpu.PrefetchScalarGridSpec(
        num_scalar_prefetch=1,
        grid=(n_steps,),
        in_specs=[
            pl.BlockSpec((rows_per_step,), lambda i, nv: (i,), memory_space=pltpu.SMEM),
            pl.BlockSpec((rows_per_step,), lambda i, nv: (jnp.minimum(i + 1, n_steps - 1),),
                         memory_space=pltpu.SMEM),
            pl.BlockSpec(memory_space=pl.ANY),
        ],
        out_specs=pl.BlockSpec((rows_per_step, D_MODEL), lambda i, nv: (i, 0)),
        scratch_shapes=[pltpu.VMEM((2, rows_per_step, D_MODEL // 2), jnp.int32), pltpu.SemaphoreType.DMA((2,))],
    )
    return pl.pallas_call(
        _gather_kernel,
        grid_spec=grid_spec,
        out_shape=jax.ShapeDtypeStruct((n_rows, D_MODEL), BF16),
        compiler_params=_cparams(("arbitrary",)),
        name="moe_gather",
    )(sub_valid, row_tok, row_tok, hn)


def _expert_kernel(be_ref, nv_ref, nr_ref, x_ref, wu_ref, bu_ref, wd_ref, bd_ref, sel_ref, o_ref):
    i = pl.program_id(0)
    f = pl.program_id(1)
    n_sub = x_ref.shape[0] // MOE_SUB
    used_sub = (nv_ref[i] + (MOE_SUB - 1)) // MOE_SUB

    def body(m_rows):
        rows = slice(0, m_rows)
        z = _mm(x_ref[rows, :], wu_ref[0].astype(BF16)) + bu_ref[0]
        zn = pltpu.roll(z, z.shape[1] - 1, 1)
        glu = jnp.minimum(z, SWIGLU_LIMIT)
        lin = jnp.clip(zn, -SWIGLU_LIMIT, SWIGLU_LIMIT)
        act = (glu * jax.nn.sigmoid(SWIGLU_ALPHA * glu) * (lin + 1.0)).astype(BF16)
        actc = _mm(act, sel_ref[...]).astype(BF16)
        wd = wd_ref[0].astype(BF16)

        @pl.when(f == 0)
        def _():
            o_ref[rows, :] = _mm(actc, wd) + bd_ref[0]
            if m_rows < o_ref.shape[0]:
                o_ref[m_rows:, :] = jnp.zeros((o_ref.shape[0] - m_rows, o_ref.shape[1]), o_ref.dtype)

        @pl.when(f > 0)
        def _():
            o_ref[rows, :] += _mm(actc, wd)

    for k in range(1, n_sub + 1):
        pl.when(used_sub == k)(functools.partial(body, k * MOE_SUB))

    @pl.when((used_sub == 0) & (f == 0))
    def _():
        o_ref[...] = jnp.zeros_like(o_ref)


def _experts(blk_expert, blk_valid, n_real, x_sorted, w_up, b_up3, w_down, b_down3, sel, bm, tf):
    n_rows = x_sorted.shape[0]
    n_f = D_FF // tf

    def real(i, nr):
        return jnp.minimum(i, nr[0] - 1)

    def f_eff(i, f, nr):
        return jnp.where(i < nr[0], f, n_f - 1)

    grid_spec = pltpu.PrefetchScalarGridSpec(
        num_scalar_prefetch=3,
        grid=(n_rows // bm, n_f),
        in_specs=[
            pl.BlockSpec((bm, D_MODEL), lambda i, f, be, nv, nr: (real(i, nr), 0)),
            pl.BlockSpec((1, D_MODEL, 2 * tf), lambda i, f, be, nv, nr: (be[i], 0, f_eff(i, f, nr))),
            pl.BlockSpec((1, 1, 2 * tf), lambda i, f, be, nv, nr: (be[i], 0, f_eff(i, f, nr))),
            pl.BlockSpec((1, tf, D_MODEL), lambda i, f, be, nv, nr: (be[i], f_eff(i, f, nr), 0)),
            pl.BlockSpec((1, 1, D_MODEL), lambda i, f, be, nv, nr: (be[i], 0, 0)),
            pl.BlockSpec((2 * tf, tf), lambda i, f, be, nv, nr: (0, 0)),
        ],
        out_specs=pl.BlockSpec((bm, D_MODEL), lambda i, f, be, nv, nr: (i, 0)),
    )
    return pl.pallas_call(
        _expert_kernel,
        grid_spec=grid_spec,
        out_shape=jax.ShapeDtypeStruct((n_rows, D_MODEL), F32),
        compiler_params=_cparams(("arbitrary", "arbitrary")),
        name="moe_experts",
    )(blk_expert, blk_valid, n_real, x_sorted, w_up, b_up3, w_down, b_down3, sel)


def _combine_kernel(dest_ref, dest_next_ref, rows_ref, h_ref, gate_ref, g_ref, op_ref, os_ref, buf, sem,
                    *, n_p_tiles):
    tm = h_ref.shape[0]
    i = pl.program_id(0)
    last = pl.num_programs(0) - 1
    slot = i % 2

    def issue(dest, dst_slot):
        def body(t2, carry):
            for u in range(2):
                t = t2 * 2 + u
                for j in range(TOP_K):
                    pltpu.make_async_copy(rows_ref.at[pl.ds(dest[t * TOP_K + j], 1)],
                                          buf.at[dst_slot, pl.ds(j * tm + t, 1)], sem.at[dst_slot]).start()
            return carry

        lax.fori_loop(0, tm // 2, body, 0)

    @pl.when(i == 0)
    def _():
        issue(dest_ref, 0)

    @pl.when(i < last)
    def _():
        issue(dest_next_ref, 1 - slot)

    pltpu.make_async_copy(rows_ref.at[pl.ds(0, TOP_K * tm)], buf.at[slot], sem.at[slot]).wait()
    gates = gate_ref[...]
    y = h_ref[...]
    for j in range(TOP_K):
        y = y + buf[slot, j * tm:(j + 1) * tm, :] * gates[:, j:j + 1]
    out = _rms(y, g_ref[...])
    is_prompt = pl.program_id(0) < n_p_tiles

    @pl.when(is_prompt)
    def _():
        op_ref[...] = out

    @pl.when(jnp.logical_not(is_prompt))
    def _():
        os_ref[...] = out


def _combine(dest, out_rows, h, gates, g, n_p, tm):
    t_tok = h.shape[0]
    n_p_tiles = n_p // tm
    n_steps = t_tok // tm
    return pl.pallas_call(
        functools.partial(_combine_kernel, n_p_tiles=n_p_tiles),
        grid=(n_steps,),
        in_specs=[
            pl.BlockSpec((tm * TOP_K,), lambda i: (i,), memory_space=pltpu.SMEM),
            pl.BlockSpec((tm * TOP_K,), lambda i: (jnp.minimum(i + 1, n_steps - 1),), memory_space=pltpu.SMEM),
            pl.BlockSpec(memory_space=pl.ANY),
            pl.BlockSpec((tm, D_MODEL), lambda i: (i, 0)),
            pl.BlockSpec((tm, LANES), lambda i: (i, 0)),
            pl.BlockSpec((1, D_MODEL), lambda i: (0, 0)),
        ],
        out_specs=[pl.BlockSpec((tm, D_MODEL), lambda i: (jnp.minimum(i, n_p_tiles - 1), 0)),
                   pl.BlockSpec((tm, D_MODEL), lambda i: (jnp.maximum(i - n_p_tiles, 0), 0))],
        out_shape=[jax.ShapeDtypeStruct((n_p, D_MODEL), F32),
                   jax.ShapeDtypeStruct((t_tok - n_p, D_MODEL), F32)],
        scratch_shapes=[pltpu.VMEM((2, TOP_K * tm, D_MODEL), F32), pltpu.SemaphoreType.DMA((2,))],
        compiler_params=_cparams(("arbitrary",)),
        name="moe_combine",
    )(dest, dest, out_rows, h, gates, g)


def _q_head_order():
    order = []
    for pair in range(2):
        for j in range(4):
            order += [8 * pair + j, 8 * pair + 4 + j]
    return order


def _q_col_perm():
    cols = []
    for h in _q_head_order():
        cols += list(range(h * HEAD_DIM, (h + 1) * HEAD_DIM))
    return jnp.asarray(cols, jnp.int32)


def _pad_rwkv_cols(m):
    def padw(a, w):
        return jnp.pad(a, [(0, 0)] * (a.ndim - 1) + [(0, w - a.shape[-1])])
    return jnp.concatenate([
        m[..., :XW_OFF],
        padw(m[..., XW_OFF:XW_OFF + DECAY_LORA], LANES),
        padw(m[..., XW_OFF + DECAY_LORA:XW_OFF + DECAY_LORA + ICLR_LORA], LANES),
        padw(m[..., XW_OFF + DECAY_LORA + ICLR_LORA:], XG_PAD),
    ], axis=-1)


def _unpad_rwkv_cols(m):
    return jnp.concatenate([
        m[..., :XW_OFF + DECAY_LORA],
        m[..., XA_OFF:XA_OFF + ICLR_LORA],
        m[..., XG_OFF:XG_OFF + GATE_LORA],
    ], axis=-1)


def _pad_rows(m, rows):
    return jnp.pad(m, ((0, rows - m.shape[0]), (0, 0)))


def _rope_tables(pos):
    half = HEAD_DIM // 2
    inv_freq = ROPE_THETA ** (-jnp.arange(half, dtype=F32) / half)
    ang = pos.astype(F32)[:, None] * inv_freq[None, :]
    cos = jnp.cos(ang)
    sin = jnp.sin(ang)
    cos_t = jnp.tile(jnp.concatenate([cos, cos], axis=-1), (1, LANES // HEAD_DIM))
    sin_t = jnp.tile(jnp.concatenate([-sin, sin], axis=-1), (1, LANES // HEAD_DIM))
    return cos_t, sin_t


def _route(top_idx, bm):
    n_tok = top_idx.shape[0]
    e_flat = top_idx.reshape(-1)
    onehot = (e_flat[:, None] == jnp.arange(N_EXPERTS, dtype=jnp.int32)[None, :]).astype(jnp.int32)
    csum = jnp.cumsum(onehot, axis=0)
    rank = jnp.sum(csum * onehot, axis=1) - 1
    counts = csum[-1]
    padded = ((counts + bm - 1) // bm) * bm
    p_end = jnp.cumsum(padded)
    p_start = p_end - padded
    dest = (p_start[e_flat] + rank).astype(jnp.int32)
    n_assign = n_tok * TOP_K
    n_blocks = -(-n_assign // bm) + N_EXPERTS
    n_rows = n_blocks * bm
    tok_flat = jnp.repeat(jnp.arange(n_tok, dtype=jnp.int32), TOP_K)
    row_tok = jnp.zeros((n_rows,), jnp.int32).at[dest].set(tok_flat, unique_indices=True)
    blk_start = jnp.arange(n_blocks, dtype=jnp.int32) * bm
    blk_expert = jnp.minimum(jnp.searchsorted(p_end, blk_start, side="right"), N_EXPERTS - 1).astype(jnp.int32)
    blk_valid = jnp.clip(counts[blk_expert] - (blk_start - p_start[blk_expert]), 0, bm).astype(jnp.int32)
    sub_start = jnp.arange(n_rows // MOE_SUB, dtype=jnp.int32) * MOE_SUB
    sub_expert = blk_expert[sub_start // bm]
    sub_valid = jnp.clip(counts[sub_expert] - (sub_start - p_start[sub_expert]), 0, MOE_SUB).astype(jnp.int32)
    n_real = (p_end[-1:] // bm).astype(jnp.int32)
    return dest, row_tok, blk_expert, blk_valid, sub_valid, n_real


def _pick(n, prefs):
    for p in prefs:
        if n % p == 0:
            return p
    return n


def kernel(x_prompt, x_sample, cache_k, cache_v, state_wkv, state_shift, attn_norm_g, w_in, attn_sinks, mu_shift, decay_w0, decay_lora_up, iclr_a0, iclr_lora_up, gate_lora_up, k_k, k_a, r_k, lnx_w, lnx_b, w_out, ffn_norm_g, router_w, router_b, w_up, b_up, w_down, b_down, final_norm_g):
    depth = w_in.shape[0]
    assert depth == 1
    batch, seq, _ = x_prompt.shape
    dec_b, dec_t, _ = x_sample.shape
    n_p = batch * seq
    n_s = dec_b * dec_t
    l = 0

    qperm = _q_col_perm()
    w_in_l = w_in[l]
    w_attn = jnp.concatenate([w_in_l[:, :ATTN_WIDTH][:, qperm], w_in_l[:, ATTN_WIDTH:ATTN_WIDTH + 2 * KV_WIDTH]],
                             axis=1).astype(BF16)
    w_rwkv = _pad_rwkv_cols(w_in_l[:, ATTN_WIDTH + 2 * KV_WIDTH:]).astype(BF16)
    w_oa = w_out[l][:ATTN_WIDTH][qperm].astype(BF16)
    w_or = w_out[l][ATTN_WIDTH:].astype(BF16)
    g_attn = attn_norm_g[l][None, :]
    g_ffn = ffn_norm_g[l][None, :]
    sinks_true = attn_sinks[l].astype(F32)
    seg = (jnp.arange(GROUP)[:, None] // HEAD_DIM == jnp.arange(GROUP)[None, :] // HEAD_DIM).astype(BF16)
    prm = dict(
        mu=_pad_rwkv_cols(mu_shift[l][None, :]),
        w0=decay_w0[l][None, :], wd=_pad_rows(decay_lora_up[l], LANES).astype(BF16),
        a0=iclr_a0[l][None, :], wa=_pad_rows(iclr_lora_up[l], LANES).astype(BF16),
        wg=_pad_rows(gate_lora_up[l], XG_PAD).astype(BF16),
        k_k=k_k[l][None, :], k_a=k_a[l][None, :], r_k=r_k[l].reshape(1, RWKV_WIDTH),
        lnx_w=lnx_w[l][None, :], lnx_b=lnx_b[l][None, :], seg=seg)
    rw_pad = jnp.pad(router_w[l], ((0, 0), (0, LANES - N_EXPERTS)))
    rw_hi = rw_pad.astype(BF16)
    rw_lo = (rw_pad - rw_hi.astype(F32)).astype(BF16)
    rw_split = jnp.stack([rw_hi, rw_lo])
    rb_pad = jnp.concatenate([router_b[l], jnp.full((LANES - N_EXPERTS,), NEG_BIG, F32)])[None, :]
    b_up3 = b_up[l][:, None, :]
    b_dn = b_down[l][:, None, :]
    tf = 256
    sel = (jnp.arange(2 * tf)[:, None] == 2 * jnp.arange(tf)[None, :]).astype(BF16)

    tm_p = _pick(seq, (512, 256, 128))
    cos_p, sin_p = _rope_tables(jnp.arange(seq, dtype=jnp.int32))
    q_p, k_p, v_p = _attn_proj(x_prompt.reshape(n_p, D_MODEL), g_attn, w_attn, cos_p, sin_p, tm_p)
    tm_s = _pick(n_s, (512, 256, 128, 64, 32, 16, 8))
    pos_s = PAST_LEN + (jnp.arange(n_s, dtype=jnp.int32) % dec_t)
    cos_s, sin_s = _rope_tables(pos_s)
    q_s, k_s, v_s = _attn_proj(x_sample.reshape(n_s, D_MODEL), g_attn, w_attn, cos_s, sin_s, tm_s)
    tn = RWKV_PROJ_PAD // 4
    pr_p = _rwkv_proj(x_prompt.reshape(n_p, D_MODEL), g_attn, w_rwkv, _pick(n_p, (1024, 512, 256, 128)), tn)
    pr_s = _rwkv_proj(x_sample.reshape(n_s, D_MODEL), g_attn, w_rwkv, tm_s, tn)

    oa_p = _attn_prompt(q_p, k_p, v_p, sinks_true, batch, seq)
    wb = cache_k.shape[2]
    sink_rows = jnp.stack([
        jnp.repeat(jnp.stack([sinks_true[8 * pair + 4 * e + j] for e in range(2) for j in range(4)]), dec_t)
        for pair in range(2)])[:, :, None]
    oa_s, nk_s, nv_s = _attn_sample(
        q_s.reshape(dec_b, dec_t, ATTN_WIDTH), k_s.reshape(dec_b, dec_t, KV_WIDTH),
        v_s.reshape(dec_b, dec_t, KV_WIDTH), cache_k[l].reshape(dec_b, wb, KV_WIDTH),
        cache_v[l].reshape(dec_b, wb, KV_WIDTH), sink_rows, _pick(dec_b, (8, 4, 2, 1)))

    zero_state = jnp.zeros((batch, RWKV_WIDTH // HEAD_DIM, HEAD_DIM, HEAD_DIM), F32)
    zero_shift = jnp.zeros((batch, 1, RWKV_PROJ_PAD), F32)
    or_p, st_p, sh_p = _rwkv_mix4(pr_p, zero_state, zero_shift, prm, 1, _pick(seq, (64, 32, 16, 8)),
                                  npar=1)
    seq_per_step = _pick(dec_b, (64 // dec_t, 1)) if 64 % dec_t == 0 else 1
    or_s, st_s, sh_s = _rwkv_mix4(pr_s, state_wkv[l], _pad_rwkv_cols(state_shift[l])[:, None, :], prm,
                                  seq_per_step, dec_t)

    n_all = n_p + n_s
    tm_o = _pick(math.gcd(n_p, n_s), (512, 256, 128, 64, 32, 16, 8))
    h, hn, idx_pad, gate_pad = _out_router(
        x_prompt.reshape(n_p, D_MODEL), x_sample.reshape(n_s, D_MODEL), oa_p, oa_s.reshape(n_s, ATTN_WIDTH),
        or_p, or_s, w_oa, w_or, g_ffn, rw_split, rb_pad, tm_o)

    bm = 1024 if n_all * TOP_K >= 1024 * N_EXPERTS else 2 * MOE_SUB
    dest, row_tok, blk_expert, blk_valid, sub_valid, n_real = _route(idx_pad[:, :TOP_K], bm)
    gather_rows = GATHER_ROWS if bm % GATHER_ROWS == 0 else MOE_SUB
    step_valid = sub_valid.reshape(-1, gather_rows // MOE_SUB).sum(axis=1)
    x_sorted = _gather_rows(step_valid, row_tok, hn, gather_rows)
    out_rows = _experts(blk_expert, blk_valid, n_real, x_sorted, w_up[l], b_up3, w_down[l], b_dn, sel, bm, tf)
    y_p, y_s = _combine(dest, out_rows, h, gate_pad, final_norm_g[None, :], n_p,
                        _pick(math.gcd(n_p, n_s), (256, 128, 64, 32, 16, 8)))

    y_prompt = y_p.reshape(batch, seq, D_MODEL)
    y_sample = y_s.reshape(dec_b, dec_t, D_MODEL)
    kp = k_p.reshape(batch, seq, 4, HEAD_DIM)[:, -WINDOW:][None]
    vp = v_p.reshape(batch, seq, 4, HEAD_DIM)[:, -WINDOW:][None]
    wp = st_p[None]
    sp = _unpad_rwkv_cols(sh_p[:, 0, :])[None]
    ks = nk_s.reshape(dec_b, wb, 4, HEAD_DIM)[None]
    vs = nv_s.reshape(dec_b, wb, 4, HEAD_DIM)[None]
    ws = st_s[None]
    ss = _unpad_rwkv_cols(sh_s[:, 0, :])[None]
    return (y_prompt, y_sample, kp, vp, wp, sp, ks, vs, ws, ss)
```

```python
import functools
import math

import jax
import jax.numpy as jnp
from jax import lax
from jax.experimental import pallas as pl
from jax.experimental.pallas import tpu as pltpu

F32 = jnp.float32
BF16 = jnp.bfloat16

D_MODEL = 2048
HEAD_DIM = 64
LANES = 128
ATTN_WIDTH = 1024
KV_WIDTH = 256
ATTN_HEADS = 16
RWKV_WIDTH = 1024
DECAY_LORA = 64
ICLR_LORA = 64
GATE_LORA = 160
RWKV_PROJ = 3 * RWKV_WIDTH + DECAY_LORA + ICLR_LORA + GATE_LORA
XW_OFF = 3 * RWKV_WIDTH
XA_OFF = XW_OFF + LANES
XG_OFF = XA_OFF + LANES
XG_PAD = 2 * LANES
RWKV_PROJ_PAD = XG_OFF + XG_PAD
WINDOW = 128
ROPE_THETA = 10000.0
PAST_LEN = 8192
N_EXPERTS = 32
TOP_K = 4
D_FF = 2048
SWIGLU_ALPHA = 1.702
SWIGLU_LIMIT = 7.0
NORM_EPS = 1e-5
LNX_EPS = 64e-5
NEG_BIG = -1e30
VMEM_LIMIT = 56 * 1024 * 1024


def _cparams(sem):
    return pltpu.CompilerParams(dimension_semantics=sem, vmem_limit_bytes=VMEM_LIMIT)


_NN = (((1,), (0,)), ((), ()))
_NT = (((1,), (1,)), ((), ()))
_TN = (((0,), (0,)), ((), ()))


def _mm(a, b, dims=_NN):
    return lax.dot_general(a, b, dims, preferred_element_type=F32)


def _split2(x):
    hi = x.astype(BF16)
    lo = (x - hi.astype(F32)).astype(BF16)
    return hi, lo


def _split3(x):
    hi = x.astype(BF16)
    r1 = x - hi.astype(F32)
    mid = r1.astype(BF16)
    lo = (r1 - mid.astype(F32)).astype(BF16)
    return hi, mid, lo


def _mm_exact_b(a, b_bf16, dims=_NN):
    h, m, l = _split3(a)
    return _mm(h, b_bf16, dims) + (_mm(m, b_bf16, dims) + _mm(l, b_bf16, dims))


def _mm_exact_a(a_bf16, b, dims=_NN):
    h, m, l = _split3(b)
    return _mm(a_bf16, h, dims) + (_mm(a_bf16, m, dims) + _mm(a_bf16, l, dims))


def _rms(x, g):
    ms = jnp.mean(x * x, axis=-1, keepdims=True)
    return (x * lax.rsqrt(ms + NORM_EPS)) * g


def _attn_proj_kernel(x_ref, g_ref, w_ref, cos_ref, sin_ref, q_ref, k_ref, v_ref):
    xn = _rms(x_ref[...], g_ref[...]).astype(BF16)
    a = _mm(xn, w_ref[...])
    cos = cos_ref[...]
    sin = sin_ref[...]
    lane = lax.broadcasted_iota(jnp.int32, cos.shape, 1)
    first_half = (lane % HEAD_DIM) < (HEAD_DIM // 2)
    n_rot = (ATTN_WIDTH + KV_WIDTH) // LANES
    for j in range(n_rot):
        t = a[:, j * LANES:(j + 1) * LANES]
        swapped = jnp.where(first_half, pltpu.roll(t, LANES - HEAD_DIM // 2, 1), pltpu.roll(t, HEAD_DIM // 2, 1))
        rot = t * cos + swapped * sin
        if j < ATTN_WIDTH // LANES:
            q_ref[:, j * LANES:(j + 1) * LANES] = (rot * (HEAD_DIM ** -0.5)).astype(BF16)
        else:
            jj = j - ATTN_WIDTH // LANES
            k_ref[:, jj * LANES:(jj + 1) * LANES] = rot
    v_ref[...] = a[:, ATTN_WIDTH + KV_WIDTH:]


def _attn_proj(x, g, w_attn, cos_t, sin_t, tm):
    t_tok = x.shape[0]
    n_pos_blocks = cos_t.shape[0] // tm
    return pl.pallas_call(
        _attn_proj_kernel,
        grid=(t_tok // tm,),
        in_specs=[
            pl.BlockSpec((tm, D_MODEL), lambda i: (i, 0)),
            pl.BlockSpec((1, D_MODEL), lambda i: (0, 0)),
            pl.BlockSpec((D_MODEL, ATTN_WIDTH + 2 * KV_WIDTH), lambda i: (0, 0)),
            pl.BlockSpec((tm, LANES), lambda i: (i % n_pos_blocks, 0)),
            pl.BlockSpec((tm, LANES), lambda i: (i % n_pos_blocks, 0)),
        ],
        out_specs=[
            pl.BlockSpec((tm, ATTN_WIDTH), lambda i: (i, 0)),
            pl.BlockSpec((tm, KV_WIDTH), lambda i: (i, 0)),
            pl.BlockSpec((tm, KV_WIDTH), lambda i: (i, 0)),
        ],
        out_shape=[
            jax.ShapeDtypeStruct((t_tok, ATTN_WIDTH), BF16),
            jax.ShapeDtypeStruct((t_tok, KV_WIDTH), F32),
            jax.ShapeDtypeStruct((t_tok, KV_WIDTH), F32),
        ],
        compiler_params=_cparams(("parallel",)),
        name="attn_proj",
    )(x, g, w_attn, cos_t, sin_t)


def _rwkv_proj_kernel(x_ref, g_ref, w_ref, p_ref, xn_ref):
    @pl.when(pl.program_id(1) == 0)
    def _():
        xn_ref[...] = _rms(x_ref[...], g_ref[...]).astype(BF16)

    p_ref[...] = _mm(xn_ref[...], w_ref[...])


def _rwkv_proj(x, g, w_rwkv, tm, tn):
    t_tok = x.shape[0]
    return pl.pallas_call(
        _rwkv_proj_kernel,
        grid=(t_tok // tm, RWKV_PROJ_PAD // tn),
        in_specs=[
            pl.BlockSpec((tm, D_MODEL), lambda i, n: (i, 0)),
            pl.BlockSpec((1, D_MODEL), lambda i, n: (0, 0)),
            pl.BlockSpec((D_MODEL, tn), lambda i, n: (0, n)),
        ],
        out_specs=pl.BlockSpec((tm, tn), lambda i, n: (i, n)),
        out_shape=jax.ShapeDtypeStruct((t_tok, RWKV_PROJ_PAD), F32),
        scratch_shapes=[pltpu.VMEM((tm, D_MODEL), BF16)],
        compiler_params=_cparams(("parallel", "arbitrary")),
        name="rwkv_proj",
    )(x, g, w_rwkv)


def _softmax_pv(s, mask, sink, vt):
    s = jnp.where(mask, s, NEG_BIG)
    m = jnp.maximum(jnp.max(s, axis=-1, keepdims=True), sink)
    p = jnp.exp(s - m)
    denom = jnp.sum(p, axis=-1, keepdims=True) + jnp.exp(sink - m)
    return _mm(p.astype(BF16), vt) * (1.0 / denom)


def _attn_prompt_kernel(q_ref, kp_ref, kc_ref, vp_ref, vc_ref, sink_ref, o_ref):
    i = pl.program_id(1)
    blk = q_ref.shape[0]
    kk = jnp.concatenate([kp_ref[...], kc_ref[...]], axis=0).astype(BF16)
    vv = jnp.concatenate([vp_ref[...], vc_ref[...]], axis=0).astype(BF16)
    a = lax.broadcasted_iota(jnp.int32, (blk, 2 * blk), 0)
    c = lax.broadcasted_iota(jnp.int32, (blk, 2 * blk), 1)
    mask = (c > a) & (c <= a + blk) & ((c >= blk) | (i > 0))
    lane_lo = lax.broadcasted_iota(jnp.int32, (blk, LANES), 1) < HEAD_DIM
    n_pairs = KV_WIDTH // LANES
    kts = [kk[:, pair * LANES:(pair + 1) * LANES] for pair in range(n_pairs)]
    vts = [vv[:, pair * LANES:(pair + 1) * LANES] for pair in range(n_pairs)]
    heads = [(pair, j, e) for pair in range(n_pairs) for j in range(4) for e in range(2)]
    scores = []
    for pair, j, e in heads:
        tile = pair * 4 + j
        qt = q_ref[:, tile * LANES:(tile + 1) * LANES]
        qm = jnp.where(lane_lo if e == 0 else jnp.logical_not(lane_lo), qt, jnp.zeros_like(qt))
        scores.append(_mm(qm, kts[pair], _NT))
    probs, scales = [], []
    for s, (pair, j, e) in zip(scores, heads):
        sink = sink_ref[8 * pair + 4 * e + j]
        s = jnp.where(mask, s, NEG_BIG)
        m = jnp.maximum(jnp.max(s, axis=-1, keepdims=True), sink)
        p = jnp.exp(s - m)
        probs.append(p.astype(BF16))
        scales.append(1.0 / (jnp.sum(p, axis=-1, keepdims=True) + jnp.exp(sink - m)))
    outs = [_mm(p, vts[pair]) * sc for p, sc, (pair, j, e) in zip(probs, scales, heads)]
    for tile in range(ATTN_WIDTH // LANES):
        o_ref[:, tile * LANES:(tile + 1) * LANES] = jnp.where(lane_lo, outs[2 * tile], outs[2 * tile + 1]).astype(BF16)


def _attn_prompt(q, k, v, sinks, batch, seq):
    blk = WINDOW
    nb = seq // blk
    cur = lambda b, i: (b * nb + i, 0)
    prev = lambda b, i: (b * nb + jnp.maximum(i - 1, 0), 0)
    return pl.pallas_call(
        _attn_prompt_kernel,
        grid=(batch, nb),
        in_specs=[
            pl.BlockSpec((blk, ATTN_WIDTH), cur),
            pl.BlockSpec((blk, KV_WIDTH), prev),
            pl.BlockSpec((blk, KV_WIDTH), cur),
            pl.BlockSpec((blk, KV_WIDTH), prev),
            pl.BlockSpec((blk, KV_WIDTH), cur),
            pl.BlockSpec(memory_space=pltpu.SMEM),
        ],
        out_specs=pl.BlockSpec((blk, ATTN_WIDTH), cur),
        out_shape=jax.ShapeDtypeStruct((batch * seq, ATTN_WIDTH), BF16),
        compiler_params=_cparams(("parallel", "arbitrary")),
        name="attn_prompt",
    )(q, k, k, v, v, sinks)


def _attn_sample_kernel(q_ref, kn_ref, vn_ref, ck_ref, cv_ref, sink_ref, o_ref, nk_ref, nv_ref):
    bb, t_len, _ = q_ref.shape
    wb = ck_ref.shape[1]
    n_keys = wb + t_len
    rows = 8 * t_len
    a = lax.broadcasted_iota(jnp.int32, (rows, n_keys), 0) % t_len
    c = lax.broadcasted_iota(jnp.int32, (rows, n_keys), 1)
    mask = ((c < wb) & (c > a + (wb - WINDOW))) | ((c >= wb) & (c - wb <= a))
    lane_lo = lax.broadcasted_iota(jnp.int32, (t_len, LANES), 1) < HEAD_DIM

    def body(b, carry):
        ck = ck_ref[b]
        cv = cv_ref[b]
        kn = kn_ref[b]
        vn = vn_ref[b]
        nk_ref[b, 0:wb - t_len, :] = ck[t_len:, :]
        nk_ref[b, wb - t_len:wb, :] = kn
        nv_ref[b, 0:wb - t_len, :] = cv[t_len:, :]
        nv_ref[b, wb - t_len:wb, :] = vn
        k_all = jnp.concatenate([ck, kn], axis=0).astype(BF16)
        v_all = jnp.concatenate([cv, vn], axis=0).astype(BF16)
        qb = q_ref[b]
        pairs = range(KV_WIDTH // LANES)
        scores = []
        for pair in pairs:
            stack = []
            for e in range(2):
                for j in range(4):
                    tile = pair * 4 + j
                    qt = qb[:, tile * LANES:(tile + 1) * LANES]
                    stack.append(jnp.where(lane_lo if e == 0 else jnp.logical_not(lane_lo), qt, jnp.zeros_like(qt)))
            qs = jnp.concatenate(stack, axis=0)
            scores.append(_mm(qs, k_all[:, pair * LANES:(pair + 1) * LANES], _NT))
        outs = [_softmax_pv(scores[pair], mask, sink_ref[pair], v_all[:, pair * LANES:(pair + 1) * LANES])
                for pair in pairs]
        for pair in pairs:
            o = outs[pair]
            for j in range(4):
                tile = pair * 4 + j
                lo = o[j * t_len:(j + 1) * t_len, :]
                hi = o[(4 + j) * t_len:(5 + j) * t_len, :]
                o_ref[b, :, tile * LANES:(tile + 1) * LANES] = jnp.where(lane_lo, lo, hi).astype(BF16)
        return carry

    lax.fori_loop(0, bb, body, 0)


def _attn_sample(q3, k3, v3, ck, cv, sink_rows, bb):
    dec_b, t_len, _ = q3.shape
    wb = ck.shape[1]
    blk3 = lambda w: pl.BlockSpec((bb, t_len, w), lambda i: (i, 0, 0))
    cache = pl.BlockSpec((bb, wb, KV_WIDTH), lambda i: (i, 0, 0))
    return pl.pallas_call(
        _attn_sample_kernel,
        grid=(dec_b // bb,),
        in_specs=[blk3(ATTN_WIDTH), blk3(KV_WIDTH), blk3(KV_WIDTH), cache, cache,
                  pl.BlockSpec((2, 8 * t_len, 1), lambda i: (0, 0, 0))],
        out_specs=[blk3(ATTN_WIDTH), cache, cache],
        out_shape=[
            jax.ShapeDtypeStruct((dec_b, t_len, ATTN_WIDTH), BF16),
            jax.ShapeDtypeStruct(ck.shape, F32),
            jax.ShapeDtypeStruct(cv.shape, F32),
        ],
        compiler_params=_cparams(("parallel",)),
        name="attn_sample",
    )(q3, k3, v3, ck, cv, sink_rows)


GROUP = 4 * HEAD_DIM


def _bf(x):
    return x.astype(BF16)


def _seg_sum4(x, seg_mat):
    outs = []
    for t in range(x.shape[1] // GROUP):
        outs.append(_mm_exact_b(x[:, t * GROUP:(t + 1) * GROUP], seg_mat))
    return jnp.concatenate(outs, axis=1)


def _stack_heads(x, head_masks):
    zero = jnp.zeros_like(x)
    return jnp.concatenate([jnp.where(m, x, zero) for m in head_masks], axis=0)


def _sum_blocks(x, r):
    return (x[0:r] + x[r:2 * r]) + (x[2 * r:3 * r] + x[3 * r:4 * r])


def _place_heads(blocks):
    rows = []
    for h, blk in enumerate(blocks):
        rows.append(jnp.concatenate([blk if j == h else jnp.zeros_like(blk) for j in range(len(blocks))], axis=1))
    return jnp.concatenate(rows, axis=0)


def _rwkv4_kernel(p_ref, s_in_ref, sh_ref, mu_ref, w0_ref, wd_ref, a0_ref, wa_ref, wg_ref, kk_ref, ka_ref,
                  rk_ref, lnw_ref, lnb_ref, seg_ref, o_ref, s_out_ref, sh_out_ref, state_scr, prev_scr,
                  *, nseq, tlen, npar):
    rows = nseq * tlen
    heads = GROUP // HEAD_DIM
    log_t = tlen.bit_length() - 1
    log_r = rows.bit_length() - 1
    log_h = HEAD_DIM.bit_length() - 1

    ri = lax.broadcasted_iota(jnp.int32, (rows, rows), 0)
    cj = lax.broadcasted_iota(jnp.int32, (rows, rows), 1)
    bri = lax.broadcasted_iota(jnp.int32, (heads * rows, heads * rows), 0)
    bcj = lax.broadcasted_iota(jnp.int32, (heads * rows, heads * rows), 1)
    same = ((bri >> log_r) == (bcj >> log_r)) & ((bri >> log_t) == (bcj >> log_t))
    lane = lax.broadcasted_iota(jnp.int32, (rows, GROUP), 1)
    sr = lax.broadcasted_iota(jnp.int32, (GROUP, GROUP), 0)
    sc = lax.broadcasted_iota(jnp.int32, (GROUP, GROUP), 1)
    masks = dict(
        tri=jnp.where(((ri >> log_t) == (cj >> log_t)) & (ri >= cj), 1.0, 0.0).astype(BF16),
        strict_bd=same & (bri > bcj),
        incl_bd=same & (bri >= bcj),
        eye_bd=jnp.where(bri == bcj, 1.0, 0.0).astype(F32),
        head_masks=[(lane >> log_h) == h for h in range(heads)],
        state_bd=(sr >> log_h) == (sc >> log_h),
    )
    refs = (p_ref, s_in_ref, sh_ref, mu_ref, w0_ref, wd_ref, a0_ref, wa_ref, wg_ref, kk_ref, ka_ref, rk_ref,
            lnw_ref, lnb_ref, seg_ref, o_ref, s_out_ref, sh_out_ref, state_scr, prev_scr)
    for j in range(npar):
        _rwkv4_block(j, refs, masks, nseq=nseq, tlen=tlen)


def _rwkv4_block(j, refs, masks, *, nseq, tlen):
    (p_ref, s_in_ref, sh_ref, mu_ref, w0_ref, wd_ref, a0_ref, wa_ref, wg_ref, kk_ref, ka_ref, rk_ref,
     lnw_ref, lnb_ref, seg_ref, o_ref, s_out_ref, sh_out_ref, state_scr, prev_scr) = refs
    ci = pl.program_id(1)
    rows = nseq * tlen
    n_groups = RWKV_WIDTH // GROUP
    heads = GROUP // HEAD_DIM
    log_t = tlen.bit_length() - 1
    carry = nseq == 1

    p = p_ref[j] if carry else p_ref[...]
    rowi = lax.broadcasted_iota(jnp.int32, (rows, 1), 0)
    rolled = pltpu.roll(p, 1, 0)
    if carry:
        @pl.when(ci == 0)
        def _():
            prev_scr[j] = sh_ref[j]
            for g in range(n_groups):
                state_scr[j * n_groups + g] = _place_heads([s_in_ref[j, heads * g + h] for h in range(heads)])

        p_prev = jnp.where(rowi == 0, prev_scr[j], rolled)
        prev_scr[j] = p[rows - 1:rows, :]
    else:
        p_prev = rolled
        for s in range(nseq):
            p_prev = jnp.where(rowi == s * tlen, sh_ref[s], p_prev)
            sh_out_ref[s] = p[(s + 1) * tlen - 1:(s + 1) * tlen, :]

    xs = p + (p_prev - p) * mu_ref[...]
    r = xs[:, 0:RWKV_WIDTH]
    k = xs[:, RWKV_WIDTH:2 * RWKV_WIDTH]
    v = xs[:, 2 * RWKV_WIDTH:3 * RWKV_WIDTH]
    xw = xs[:, XW_OFF:XW_OFF + LANES]
    xa = xs[:, XA_OFF:XA_OFF + LANES]
    xg = xs[:, XG_OFF:XG_OFF + XG_PAD]
    z = w0_ref[...] + _mm(_bf(jnp.tanh(xw)), wd_ref[...])
    w_log = -jax.nn.softplus(-z) - 0.5
    lw = -jnp.exp(w_log)
    a = jax.nn.sigmoid(a0_ref[...] + _mm(_bf(xa), wa_ref[...]))
    gate = _mm(_bf(jax.nn.sigmoid(xg)), wg_ref[...])
    seg = seg_ref[...]
    kk = k * kk_ref[...]
    kap = kk / jnp.maximum(jnp.sqrt(_seg_sum4(kk * kk, seg)), 1e-12)
    k2 = k * (1.0 + (a - 1.0) * ka_ref[...])
    b = kap * a
    bonus = _seg_sum4(r * k2 * rk_ref[...], seg) * v

    cum = _mm_exact_a(masks["tri"], lw)
    eg = jnp.exp(cum)
    kap_t = kap * jnp.exp(cum - lw)
    r_t = r * eg
    einv = jnp.exp(-cum)
    b_t = b * einv
    k_t = k2 * einv

    strict_bd, incl_bd, eye_bd = masks["strict_bd"], masks["incl_bd"], masks["eye_bd"]
    head_masks, state_bd = masks["head_masks"], masks["state_bd"]
    zero_bd = jnp.zeros((heads * rows, heads * rows), F32)
    zero_st = jnp.zeros((GROUP, GROUP), F32)

    groups = range(n_groups)
    hr = heads * rows
    sls = [slice(g * GROUP, (g + 1) * GROUP) for g in groups]
    kap_g = [kap_t[:, sl] for sl in sls]
    r_g = [r_t[:, sl] for sl in sls]
    b_g = [b_t[:, sl] for sl in sls]
    k_g = [k_t[:, sl] for sl in sls]
    v_g = [v[:, sl] for sl in sls]
    lkr = [jnp.concatenate([_stack_heads(_bf(kap_g[g]), head_masks), _stack_heads(_bf(r_g[g]), head_masks)], axis=0)
           for g in groups]
    gb = [_mm(lkr[g], _stack_heads(_bf(b_g[g]), head_masks), _NT) for g in groups]
    gk = [_mm(lkr[g], _stack_heads(_bf(k_g[g]), head_masks), _NT) for g in groups]
    n = [-jnp.where(strict_bd, gb[g][:hr], zero_bd) for g in groups]
    t_inv = [eye_bd + n[g] for g in groups]
    for _ in range(max(log_t - 1, 0)):
        nb = [_bf(n[g]) for g in groups]
        n = [_mm(nb[g], nb[g]) for g in groups]
        t_inv = [t_inv[g] + _mm(_bf(t_inv[g]), _bf(n[g])) for g in groups]
    a_k = [_bf(jnp.where(strict_bd, gk[g][:hr], zero_bd)) for g in groups]
    r_b = [_bf(jnp.where(incl_bd, gb[g][hr:], zero_bd)) for g in groups]
    r_k = [_bf(jnp.where(incl_bd, gk[g][hr:], zero_bd)) for g in groups]
    v_stack = [_stack_heads(_bf(v_g[g]), head_masks) for g in groups]

    if carry:
        s0 = [state_scr[j * n_groups + g] for g in groups]
        sk = [_mm(_bf(jnp.concatenate([kap_g[g], r_g[g]], axis=0)), _bf(s0[g]), _NT) for g in groups]
        k_s = [sk[g][:rows] for g in groups]
        r_s = [sk[g][rows:] for g in groups]
    else:
        s0, k_s, r_s = [], [], []
        for g in groups:
            s0_list, ks_list, rs_list = [], [], []
            for s in range(nseq):
                rs = slice(s * tlen, (s + 1) * tlen)
                st = _place_heads([s_in_ref[s, heads * g + h] for h in range(heads)])
                sk = _mm(_bf(jnp.concatenate([kap_g[g][rs], r_g[g][rs]], axis=0)), _bf(st), _NT)
                s0_list.append(st)
                ks_list.append(sk[:tlen])
                rs_list.append(sk[tlen:])
            s0.append(s0_list)
            k_s.append(jnp.concatenate(ks_list, axis=0))
            r_s.append(jnp.concatenate(rs_list, axis=0))

    w = [_sum_blocks(_mm(a_k[g], v_stack[g]), rows) for g in groups]
    rhs = [-(k_s[g] + w[g]) for g in groups]
    u_stack = [_mm(_bf(t_inv[g]), _stack_heads(_bf(rhs[g]), head_masks)) for g in groups]
    u = [_sum_blocks(u_stack[g], rows) for g in groups]
    y_stack = [_mm(r_b[g], _bf(u_stack[g])) + _mm(r_k[g], v_stack[g]) for g in groups]
    ys = [r_s[g] + _sum_blocks(y_stack[g], rows) for g in groups]

    for g in groups:
        if carry:
            d = _mm(_bf(jnp.concatenate([u[g], v_g[g]], axis=0)), _bf(jnp.concatenate([b_g[g], k_g[g]], axis=0)), _TN)
            state_scr[j * n_groups + g] = (s0[g] + jnp.where(state_bd, d, zero_st)) * eg[rows - 1:rows, sls[g]]
        else:
            for s in range(nseq):
                rs = slice(s * tlen, (s + 1) * tlen)
                d = _mm(_bf(jnp.concatenate([u[g][rs], v_g[g][rs]], axis=0)),
                        _bf(jnp.concatenate([b_g[g][rs], k_g[g][rs]], axis=0)), _TN)
                s_new = (s0[g][s] + jnp.where(state_bd, d, zero_st)) * eg[(s + 1) * tlen - 1:(s + 1) * tlen, sls[g]]
                for h in range(heads):
                    hs = slice(h * HEAD_DIM, (h + 1) * HEAD_DIM)
                    s_out_ref[s, heads * g + h] = s_new[hs, hs]

    y = jnp.concatenate(ys, axis=1)
    inv_n = 1.0 / HEAD_DIM
    mean = _seg_sum4(y, seg) * inv_n
    yc = y - mean
    var = _seg_sum4(yc * yc, seg) * inv_n
    yn = yc * lax.rsqrt(var + LNX_EPS) * lnw_ref[...] + lnb_ref[...]
    out = ((yn + bonus) * gate).astype(o_ref.dtype)
    if carry:
        o_ref[j] = out

        @pl.when(ci == pl.num_programs(1) - 1)
        def _():
            sh_out_ref[j] = prev_scr[j]
            for g in range(n_groups):
                st = state_scr[j * n_groups + g]
                for h in range(heads):
                    hs = slice(h * HEAD_DIM, (h + 1) * HEAD_DIM)
                    s_out_ref[j, heads * g + h] = st[hs, hs]
    else:
        o_ref[...] = out


def _rwkv_mix4(p2, s_in, shift3, prm, nseq, tlen, npar=1):
    batch = s_in.shape[0]
    t_len = p2.shape[0] // batch
    n_chunks = t_len // tlen
    rows = nseq * tlen
    n_heads = RWKV_WIDTH // HEAD_DIM
    per_step = nseq * npar
    const2 = lambda shape: pl.BlockSpec(shape, lambda b, c: (0, 0))
    vec = const2((1, RWKV_WIDTH))
    state_spec = pl.BlockSpec((per_step, n_heads, HEAD_DIM, HEAD_DIM), lambda b, c: (b, 0, 0, 0))
    shift_spec = pl.BlockSpec((per_step, 1, RWKV_PROJ_PAD), lambda b, c: (b, 0, 0))
    if nseq == 1:
        p_in = p2.reshape(batch, t_len, RWKV_PROJ_PAD)
        p_spec = pl.BlockSpec((npar, tlen, RWKV_PROJ_PAD), lambda b, c: (b, c, 0))
        o_spec = pl.BlockSpec((npar, tlen, RWKV_WIDTH), lambda b, c: (b, c, 0))
        o_shape = jax.ShapeDtypeStruct((batch, t_len, RWKV_WIDTH), BF16)
    else:
        p_in = p2
        p_spec = pl.BlockSpec((rows, RWKV_PROJ_PAD), lambda b, c: (b * n_chunks + c, 0))
        o_spec = pl.BlockSpec((rows, RWKV_WIDTH), lambda b, c: (b * n_chunks + c, 0))
        o_shape = jax.ShapeDtypeStruct((batch * t_len, RWKV_WIDTH), BF16)
    o, s_out, sh_out = pl.pallas_call(
        functools.partial(_rwkv4_kernel, nseq=nseq, tlen=tlen, npar=npar),
        grid=(batch // per_step, n_chunks),
        in_specs=[
            p_spec,
            state_spec,
            shift_spec,
            const2((1, RWKV_PROJ_PAD)),
            vec,
            const2((LANES, RWKV_WIDTH)),
            vec,
            const2((LANES, RWKV_WIDTH)),
            const2((XG_PAD, RWKV_WIDTH)),
            vec, vec, vec, vec, vec,
            const2((GROUP, GROUP)),
        ],
        out_specs=[o_spec, state_spec, shift_spec],
        out_shape=[
            o_shape,
            jax.ShapeDtypeStruct(s_in.shape, F32),
            jax.ShapeDtypeStruct(shift3.shape, F32),
        ],
        scratch_shapes=[pltpu.VMEM((npar * (RWKV_WIDTH // GROUP), GROUP, GROUP), F32),
                        pltpu.VMEM((npar, 1, RWKV_PROJ_PAD), F32)],
        compiler_params=_cparams(("parallel", "arbitrary")),
        name="rwkv_mix_t%d" % tlen,
    )(p_in, s_in, shift3, prm["mu"], prm["w0"], prm["wd"], prm["a0"], prm["wa"], prm["wg"], prm["k_k"],
      prm["k_a"], prm["r_k"], prm["lnx_w"], prm["lnx_b"], prm["seg"])
    return o.reshape(batch * t_len, RWKV_WIDTH), s_out, sh_out


def _out_router_kernel(xp_ref, xs_ref, oap_ref, oas_ref, orp_ref, ors_ref, wa_ref, wr_ref, g_ref, rw_ref, rb_ref,
                       h_ref, hn_ref, idx_ref, gate_ref, *, n_p_tiles):
    body = functools.partial(_out_router_body, wa_ref=wa_ref, wr_ref=wr_ref, g_ref=g_ref, rw_ref=rw_ref,
                             rb_ref=rb_ref, h_ref=h_ref, hn_ref=hn_ref, idx_ref=idx_ref, gate_ref=gate_ref)
    is_prompt = pl.program_id(0) < n_p_tiles
    pl.when(is_prompt)(functools.partial(body, xp_ref, oap_ref, orp_ref))
    pl.when(jnp.logical_not(is_prompt))(functools.partial(body, xs_ref, oas_ref, ors_ref))


def _out_router_body(x_ref, oa_ref, or_ref, *, wa_ref, wr_ref, g_ref, rw_ref, rb_ref, h_ref, hn_ref, idx_ref,
                     gate_ref):
    h = x_ref[...] + _mm(oa_ref[...], wa_ref[...]) + _mm(or_ref[...], wr_ref[...])
    h_ref[...] = h
    hn = _rms(h, g_ref[...])
    hh, hl = _split2(hn)
    bits = lax.bitcast_convert_type(hh.astype(F32), jnp.int32)
    half = D_MODEL // 2
    hn_ref[...] = ((bits[:, :half] >> 16) & jnp.int32(0xFFFF)) | (bits[:, half:] & jnp.int32(-65536))
    logits = _mm(hh, rw_ref[0]) + (_mm(hh, rw_ref[1]) + _mm(hl, rw_ref[0])) + rb_ref[...]
    lane = lax.broadcasted_iota(jnp.int32, logits.shape, 1)
    vals = []
    idxs = []
    cur = logits
    for _ in range(TOP_K):
        m = jnp.max(cur, axis=-1, keepdims=True)
        sel = jnp.min(jnp.where(cur == m, lane, LANES), axis=-1, keepdims=True)
        vals.append(m)
        idxs.append(sel)
        cur = jnp.where(lane == sel, -jnp.inf, cur)
    es = [jnp.exp(vj - vals[0]) for vj in vals]
    tot = es[0] + es[1] + es[2] + es[3]
    idx_out = jnp.zeros(logits.shape, jnp.int32)
    gate_out = jnp.zeros(logits.shape, F32)
    for j in range(TOP_K):
        idx_out = jnp.where(lane == j, idxs[j], idx_out)
        gate_out = jnp.where(lane == j, es[j] / tot, gate_out)
    idx_ref[...] = idx_out
    gate_ref[...] = gate_out


def _out_router(x_p, x_s, oa_p, oa_s, or_p, or_s, w_oa, w_or, g, rw_split, rb_pad, tm):
    n_p_tiles = x_p.shape[0] // tm
    t_tok = x_p.shape[0] + x_s.shape[0]
    row = lambda w: pl.BlockSpec((tm, w), lambda i: (i, 0))
    row_p = lambda w: pl.BlockSpec((tm, w), lambda i: (jnp.minimum(i, n_p_tiles - 1), 0))
    row_s = lambda w: pl.BlockSpec((tm, w), lambda i: (jnp.maximum(i - n_p_tiles, 0), 0))
    full = lambda shape: pl.BlockSpec(shape, lambda i: tuple(0 for _ in shape))
    return pl.pallas_call(
        functools.partial(_out_router_kernel, n_p_tiles=n_p_tiles),
        grid=(t_tok // tm,),
        in_specs=[row_p(D_MODEL), row_s(D_MODEL), row_p(ATTN_WIDTH), row_s(ATTN_WIDTH), row_p(RWKV_WIDTH),
                  row_s(RWKV_WIDTH), full((ATTN_WIDTH, D_MODEL)), full((RWKV_WIDTH, D_MODEL)), full((1, D_MODEL)),
                  full((2, D_MODEL, LANES)), full((1, LANES))],
        out_specs=[row(D_MODEL), row(D_MODEL // 2), row(LANES), row(LANES)],
        out_shape=[
            jax.ShapeDtypeStruct((t_tok, D_MODEL), F32),
            jax.ShapeDtypeStruct((t_tok, D_MODEL // 2), jnp.int32),
            jax.ShapeDtypeStruct((t_tok, LANES), jnp.int32),
            jax.ShapeDtypeStruct((t_tok, LANES), F32),
        ],
        compiler_params=_cparams(("arbitrary",)),
        name="out_router",
    )(x_p, x_s, oa_p, oa_s, or_p, or_s, w_oa, w_or, g, rw_split, rb_pad)


GATHER_UNROLL = 8
MOE_SUB = 256
GATHER_ROWS = 256


def _gather_kernel(nv_ref, tok_ref, tok_next_ref, hn_ref, o_ref, buf, sem):
    rows = buf.shape[1]
    i = pl.program_id(0)
    last = pl.num_programs(0) - 1
    slot = i % 2

    def issue(tok, dst_slot):
        def body(r8, carry):
            for u in range(GATHER_UNROLL):
                r = r8 * GATHER_UNROLL + u
                pltpu.make_async_copy(hn_ref.at[pl.ds(tok[r], 1)], buf.at[dst_slot, pl.ds(r, 1)],
                                      sem.at[dst_slot]).start()
            return carry

        lax.fori_loop(0, rows // GATHER_UNROLL, body, 0)

    @pl.when((i == 0) & (nv_ref[0] > 0))
    def _():
        issue(tok_ref, 0)

    @pl.when((i < last) & (nv_ref[jnp.minimum(i + 1, last)] > 0))
    def _():
        issue(tok_next_ref, 1 - slot)

    @pl.when(nv_ref[i] > 0)
    def _():
        pltpu.make_async_copy(hn_ref.at[pl.ds(0, rows)], buf.at[slot], sem.at[slot]).wait()
        words = buf[slot]
        half = words.shape[1]
        o_ref[:, :half] = lax.bitcast_convert_type(words << 16, F32).astype(o_ref.dtype)
        o_ref[:, half:] = lax.bitcast_convert_type(words & jnp.int32(-65536), F32).astype(o_ref.dtype)

    @pl.when(nv_ref[i] == 0)
    def _():
        o_ref[...] = jnp.zeros_like(o_ref)


def _gather_rows(sub_valid, row_tok, hn, rows_per_step):
    n_rows = row_tok.shape[0]
    n_steps = n_rows // rows_per_step
    grid_spec = pltpu.PrefetchScalarGridSpec(
        num_scalar_prefetch=1,
        grid=(n_steps,),
        in_specs=[
            pl.BlockSpec((rows_per_step,), lambda i, nv: (i,), memory_space=pltpu.SMEM),
            pl.BlockSpec((rows_per_step,), lambda i, nv: (jnp.minimum(i + 1, n_steps - 1),),
                         memory_space=pltpu.SMEM),
            pl.BlockSpec(memory_space=pl.ANY),
        ],
        out_specs=pl.BlockSpec((rows_per_step, D_MODEL), lambda i, nv: (i, 0)),
        scratch_shapes=[pltpu.VMEM((2, rows_per_step, D_MODEL // 2), jnp.int32), pltpu.SemaphoreType.DMA((2,))],
    )
    return pl.pallas_call(
        _gather_kernel,
        grid_spec=grid_spec,
        out_shape=jax.ShapeDtypeStruct((n_rows, D_MODEL), BF16),
        compiler_params=_cparams(("arbitrary",)),
        name="moe_gather",
    )(sub_valid, row_tok, row_tok, hn)


def _expert_kernel(be_ref, nv_ref, nr_ref, x_ref, wu_ref, bu_ref, wd_ref, bd_ref, sel_ref, o_ref):
    i = pl.program_id(0)
    f = pl.program_id(1)
    n_sub = x_ref.shape[0] // MOE_SUB
    used_sub = (nv_ref[i] + (MOE_SUB - 1)) // MOE_SUB

    def body(m_rows):
        rows = slice(0, m_rows)
        z = _mm(x_ref[rows, :], wu_ref[0].astype(BF16)) + bu_ref[0]
        zn = pltpu.roll(z, z.shape[1] - 1, 1)
        glu = jnp.minimum(z, SWIGLU_LIMIT)
        lin = jnp.clip(zn, -SWIGLU_LIMIT, SWIGLU_LIMIT)
        act = (glu * jax.nn.sigmoid(SWIGLU_ALPHA * glu) * (lin + 1.0)).astype(BF16)
        actc = _mm(act, sel_ref[...]).astype(BF16)
        wd = wd_ref[0].astype(BF16)

        @pl.when(f == 0)
        def _():
            o_ref[rows, :] = _mm(actc, wd) + bd_ref[0]
            if m_rows < o_ref.shape[0]:
                o_ref[m_rows:, :] = jnp.zeros((o_ref.shape[0] - m_rows, o_ref.shape[1]), o_ref.dtype)

        @pl.when(f > 0)
        def _():
            o_ref[rows, :] += _mm(actc, wd)

    for k in range(1, n_sub + 1):
        pl.when(used_sub == k)(functools.partial(body, k * MOE_SUB))

    @pl.when((used_sub == 0) & (f == 0))
    def _():
        o_ref[...] = jnp.zeros_like(o_ref)


def _experts(blk_expert, blk_valid, n_real, x_sorted, w_up, b_up3, w_down, b_down3, sel, bm, tf):
    n_rows = x_sorted.shape[0]
    n_f = D_FF // tf

    def real(i, nr):
        return jnp.minimum(i, nr[0] - 1)

    def f_eff(i, f, nr):
        return jnp.where(i < nr[0], f, n_f - 1)

    grid_spec = pltpu.PrefetchScalarGridSpec(
        num_scalar_prefetch=3,
        grid=(n_rows // bm, n_f),
        in_specs=[
            pl.BlockSpec((bm, D_MODEL), lambda i, f, be, nv, nr: (real(i, nr), 0)),
            pl.BlockSpec((1, D_MODEL, 2 * tf), lambda i, f, be, nv, nr: (be[i], 0, f_eff(i, f, nr))),
            pl.BlockSpec((1, 1, 2 * tf), lambda i, f, be, nv, nr: (be[i], 0, f_eff(i, f, nr))),
            pl.BlockSpec((1, tf, D_MODEL), lambda i, f, be, nv, nr: (be[i], f_eff(i, f, nr), 0)),
            pl.BlockSpec((1, 1, D_MODEL), lambda i, f, be, nv, nr: (be[i], 0, 0)),
            pl.BlockSpec((2 * tf, tf), lambda i, f, be, nv, nr: (0, 0)),
        ],
        out_specs=pl.BlockSpec((bm, D_MODEL), lambda i, f, be, nv, nr: (i, 0)),
    )
    return pl.pallas_call(
        _expert_kernel,
        grid_spec=grid_spec,
        out_shape=jax.ShapeDtypeStruct((n_rows, D_MODEL), F32),
        compiler_params=_cparams(("arbitrary", "arbitrary")),
        name="moe_experts",
    )(blk_expert, blk_valid, n_real, x_sorted, w_up, b_up3, w_down, b_down3, sel)


def _combine_kernel(dest_ref, dest_next_ref, rows_ref, h_ref, gate_ref, g_ref, op_ref, os_ref, buf, sem,
                    *, n_p_tiles):
    tm = h_ref.shape[0]
    i = pl.program_id(0)
    last = pl.num_programs(0) - 1
    slot = i % 2

    def issue(dest, dst_slot):
        def body(t2, carry):
            for u in range(2):
                t = t2 * 2 + u
                for j in range(TOP_K):
                    pltpu.make_async_copy(rows_ref.at[pl.ds(dest[t * TOP_K + j], 1)],
                                          buf.at[dst_slot, pl.ds(j * tm + t, 1)], sem.at[dst_slot]).start()
            return carry

        lax.fori_loop(0, tm // 2, body, 0)

    @pl.when(i == 0)
    def _():
        issue(dest_ref, 0)

    @pl.when(i < last)
    def _():
        issue(dest_next_ref, 1 - slot)

    pltpu.make_async_copy(rows_ref.at[pl.ds(0, TOP_K * tm)], buf.at[slot], sem.at[slot]).wait()
    gates = gate_ref[...]
    y = h_ref[...]
    for j in range(TOP_K):
        y = y + buf[slot, j * tm:(j + 1) * tm, :] * gates[:, j:j + 1]
    out = _rms(y, g_ref[...])
    is_prompt = pl.program_id(0) < n_p_tiles

    @pl.when(is_prompt)
    def _():
        op_ref[...] = out

    @pl.when(jnp.logical_not(is_prompt))
    def _():
        os_ref[...] = out


def _combine(dest, out_rows, h, gates, g, n_p, tm):
    t_tok = h.shape[0]
    n_p_tiles = n_p // tm
    n_steps = t_tok // tm
    return pl.pallas_call(
        functools.partial(_combine_kernel, n_p_tiles=n_p_tiles),
        grid=(n_steps,),
        in_specs=[
            pl.BlockSpec((tm * TOP_K,), lambda i: (i,), memory_space=pltpu.SMEM),
            pl.BlockSpec((tm * TOP_K,), lambda i: (jnp.minimum(i + 1, n_steps - 1),), memory_space=pltpu.SMEM),
            pl.BlockSpec(memory_space=pl.ANY),
            pl.BlockSpec((tm, D_MODEL), lambda i: (i, 0)),
            pl.BlockSpec((tm, LANES), lambda i: (i, 0)),
            pl.BlockSpec((1, D_MODEL), lambda i: (0, 0)),
        ],
        out_specs=[pl.BlockSpec((tm, D_MODEL), lambda i: (jnp.minimum(i, n_p_tiles - 1), 0)),
                   pl.BlockSpec((tm, D_MODEL), lambda i: (jnp.maximum(i - n_p_tiles, 0), 0))],
        out_shape=[jax.ShapeDtypeStruct((n_p, D_MODEL), F32),
                   jax.ShapeDtypeStruct((t_tok - n_p, D_MODEL), F32)],
        scratch_shapes=[pltpu.VMEM((2, TOP_K * tm, D_MODEL), F32), pltpu.SemaphoreType.DMA((2,))],
        compiler_params=_cparams(("arbitrary",)),
        name="moe_combine",
    )(dest, dest, out_rows, h, gates, g)


def _q_head_order():
    order = []
    for pair in range(2):
        for j in range(4):
            order += [8 * pair + j, 8 * pair + 4 + j]
    return order


def _q_col_perm():
    cols = []
    for h in _q_head_order():
        cols += list(range(h * HEAD_DIM, (h + 1) * HEAD_DIM))
    return jnp.asarray(cols, jnp.int32)


def _pad_rwkv_cols(m):
    def padw(a, w):
        return jnp.pad(a, [(0, 0)] * (a.ndim - 1) + [(0, w - a.shape[-1])])
    return jnp.concatenate([
        m[..., :XW_OFF],
        padw(m[..., XW_OFF:XW_OFF + DECAY_LORA], LANES),
        padw(m[..., XW_OFF + DECAY_LORA:XW_OFF + DECAY_LORA + ICLR_LORA], LANES),
        padw(m[..., XW_OFF + DECAY_LORA + ICLR_LORA:], XG_PAD),
    ], axis=-1)


def _unpad_rwkv_cols(m):
    return jnp.concatenate([
        m[..., :XW_OFF + DECAY_LORA],
        m[..., XA_OFF:XA_OFF + ICLR_LORA],
        m[..., XG_OFF:XG_OFF + GATE_LORA],
    ], axis=-1)


def _pad_rows(m, rows):
    return jnp.pad(m, ((0, rows - m.shape[0]), (0, 0)))


def _rope_tables(pos):
    half = HEAD_DIM // 2
    inv_freq = ROPE_THETA ** (-jnp.arange(half, dtype=F32) / half)
    ang = pos.astype(F32)[:, None] * inv_freq[None, :]
    cos = jnp.cos(ang)
    sin = jnp.sin(ang)
    cos_t = jnp.tile(jnp.concatenate([cos, cos], axis=-1), (1, LANES // HEAD_DIM))
    sin_t = jnp.tile(jnp.concatenate([-sin, sin], axis=-1), (1, LANES // HEAD_DIM))
    return cos_t, sin_t


def _route(top_idx, bm):
    n_tok = top_idx.shape[0]
    e_flat = top_idx.reshape(-1)
    onehot = (e_flat[:, None] == jnp.arange(N_EXPERTS, dtype=jnp.int32)[None, :]).astype(jnp.int32)
    csum = jnp.cumsum(onehot, axis=0)
    rank = jnp.sum(csum * onehot, axis=1) - 1
    counts = csum[-1]
    padded = ((counts + bm - 1) // bm) * bm
    p_end = jnp.cumsum(padded)
    p_start = p_end - padded
    dest = (p_start[e_flat] + rank).astype(jnp.int32)
    n_assign = n_tok * TOP_K
    n_blocks = -(-n_assign // bm) + N_EXPERTS
    n_rows = n_blocks * bm
    tok_flat = jnp.repeat(jnp.arange(n_tok, dtype=jnp.int32), TOP_K)
    row_tok = (jnp.arange(n_rows, dtype=jnp.int32) % n_tok).at[dest].set(tok_flat, unique_indices=True)
    blk_start = jnp.arange(n_blocks, dtype=jnp.int32) * bm
    blk_expert = jnp.minimum(jnp.searchsorted(p_end, blk_start, side="right"), N_EXPERTS - 1).astype(jnp.int32)
    blk_valid = jnp.clip(counts[blk_expert] - (blk_start - p_start[blk_expert]), 0, bm).astype(jnp.int32)
    sub_start = jnp.arange(n_rows // MOE_SUB, dtype=jnp.int32) * MOE_SUB
    sub_expert = blk_expert[sub_start // bm]
    sub_valid = jnp.clip(counts[sub_expert] - (sub_start - p_start[sub_expert]), 0, MOE_SUB).astype(jnp.int32)
    n_real = (p_end[-1:] // bm).astype(jnp.int32)
    return dest, row_tok, blk_expert, blk_valid, sub_valid, n_real


def _pick(n, prefs):
    for p in prefs:
        if n % p == 0:
            return p
    return n


def kernel(x_prompt, x_sample, cache_k, cache_v, state_wkv, state_shift, attn_norm_g, w_in, attn_sinks, mu_shift, decay_w0, decay_lora_up, iclr_a0, iclr_lora_up, gate_lora_up, k_k, k_a, r_k, lnx_w, lnx_b, w_out, ffn_norm_g, router_w, router_b, w_up, b_up, w_down, b_down, final_norm_g):
    depth = w_in.shape[0]
    assert depth == 1
    batch, seq, _ = x_prompt.shape
    dec_b, dec_t, _ = x_sample.shape
    n_p = batch * seq
    n_s = dec_b * dec_t
    l = 0

    qperm = _q_col_perm()
    w_in_l = w_in[l]
    w_attn = jnp.concatenate([w_in_l[:, :ATTN_WIDTH][:, qperm], w_in_l[:, ATTN_WIDTH:ATTN_WIDTH + 2 * KV_WIDTH]],
                             axis=1).astype(BF16)
    w_rwkv = _pad_rwkv_cols(w_in_l[:, ATTN_WIDTH + 2 * KV_WIDTH:]).astype(BF16)
    w_oa = w_out[l][:ATTN_WIDTH][qperm].astype(BF16)
    w_or = w_out[l][ATTN_WIDTH:].astype(BF16)
    g_attn = attn_norm_g[l][None, :]
    g_ffn = ffn_norm_g[l][None, :]
    sinks_true = attn_sinks[l].astype(F32)
    seg = (jnp.arange(GROUP)[:, None] // HEAD_DIM == jnp.arange(GROUP)[None, :] // HEAD_DIM).astype(BF16)
    prm = dict(
        mu=_pad_rwkv_cols(mu_shift[l][None, :]),
        w0=decay_w0[l][None, :], wd=_pad_rows(decay_lora_up[l], LANES).astype(BF16),
        a0=iclr_a0[l][None, :], wa=_pad_rows(iclr_lora_up[l], LANES).astype(BF16),
        wg=_pad_rows(gate_lora_up[l], XG_PAD).astype(BF16),
        k_k=k_k[l][None, :], k_a=k_a[l][None, :], r_k=r_k[l].reshape(1, RWKV_WIDTH),
        lnx_w=lnx_w[l][None, :], lnx_b=lnx_b[l][None, :], seg=seg)
    rw_pad = jnp.pad(router_w[l], ((0, 0), (0, LANES - N_EXPERTS)))
    rw_hi = rw_pad.astype(BF16)
    rw_lo = (rw_pad - rw_hi.astype(F32)).astype(BF16)
    rw_split = jnp.stack([rw_hi, rw_lo])
    rb_pad = jnp.concatenate([router_b[l], jnp.full((LANES - N_EXPERTS,), NEG_BIG, F32)])[None, :]
    b_up3 = b_up[l][:, None, :]
    b_dn = b_down[l][:, None, :]
    tf = 256
    sel = (jnp.arange(2 * tf)[:, None] == 2 * jnp.arange(tf)[None, :]).astype(BF16)

    tm_p = _pick(seq, (512, 256, 128))
    cos_p, sin_p = _rope_tables(jnp.arange(seq, dtype=jnp.int32))
    q_p, k_p, v_p = _attn_proj(x_prompt.reshape(n_p, D_MODEL), g_attn, w_attn, cos_p, sin_p, tm_p)
    tm_s = _pick(n_s, (512, 256, 128, 64, 32, 16, 8))
    pos_s = PAST_LEN + (jnp.arange(n_s, dtype=jnp.int32) % dec_t)
    cos_s, sin_s = _rope_tables(pos_s)
    q_s, k_s, v_s = _attn_proj(x_sample.reshape(n_s, D_MODEL), g_attn, w_attn, cos_s, sin_s, tm_s)
    tn = RWKV_PROJ_PAD // 4
    pr_p = _rwkv_proj(x_prompt.reshape(n_p, D_MODEL), g_attn, w_rwkv, _pick(n_p, (1024, 512, 256, 128)), tn)
    pr_s = _rwkv_proj(x_sample.reshape(n_s, D_MODEL), g_attn, w_rwkv, tm_s, tn)

    oa_p = _attn_prompt(q_p, k_p, v_p, sinks_true, batch, seq)
    wb = cache_k.shape[2]
    sink_rows = jnp.stack([
        jnp.repeat(jnp.stack([sinks_true[8 * pair + 4 * e + j] for e in range(2) for j in range(4)]), dec_t)
        for pair in range(2)])[:, :, None]
    oa_s, nk_s, nv_s = _attn_sample(
        q_s.reshape(dec_b, dec_t, ATTN_WIDTH), k_s.reshape(dec_b, dec_t, KV_WIDTH),
        v_s.reshape(dec_b, dec_t, KV_WIDTH), cache_k[l].reshape(dec_b, wb, KV_WIDTH),
        cache_v[l].reshape(dec_b, wb, KV_WIDTH), sink_rows, _pick(dec_b, (8, 4, 2, 1)))

    zero_state = jnp.zeros((batch, RWKV_WIDTH // HEAD_DIM, HEAD_DIM, HEAD_DIM), F32)
    zero_shift = jnp.zeros((batch, 1, RWKV_PROJ_PAD), F32)
    or_p, st_p, sh_p = _rwkv_mix4(pr_p, zero_state, zero_shift, prm, 1, _pick(seq, (64, 32, 16, 8)),
                                  npar=1)
    seq_per_step = _pick(dec_b, (64 // dec_t, 1)) if 64 % dec_t == 0 else 1
    or_s, st_s, sh_s = _rwkv_mix4(pr_s, state_wkv[l], _pad_rwkv_cols(state_shift[l])[:, None, :], prm,
                                  seq_per_step, dec_t)

    n_all = n_p + n_s
    tm_o = _pick(math.gcd(n_p, n_s), (512, 256, 128, 64, 32, 16, 8))
    h, hn, idx_pad, gate_pad = _out_router(
        x_prompt.reshape(n_p, D_MODEL), x_sample.reshape(n_s, D_MODEL), oa_p, oa_s.reshape(n_s, ATTN_WIDTH),
        or_p, or_s, w_oa, w_or, g_ffn, rw_split, rb_pad, tm_o)

    bm = 1024 if n_all * TOP_K >= 1024 * N_EXPERTS else 2 * MOE_SUB
    dest, row_tok, blk_expert, blk_valid, sub_valid, n_real = _route(idx_pad[:, :TOP_K], bm)
    gather_rows = GATHER_ROWS if bm % GATHER_ROWS == 0 else MOE_SUB
    step_valid = sub_valid.reshape(-1, gather_rows // MOE_SUB).sum(axis=1)
    x_sorted = _gather_rows(step_valid, row_tok, hn, gather_rows)
    out_rows = _experts(blk_expert, blk_valid, n_real, x_sorted, w_up[l], b_up3, w_down[l], b_dn, sel, bm, tf)
    y_p, y_s = _combine(dest, out_rows, h, gate_pad, final_norm_g[None, :], n_p,
                        _pick(math.gcd(n_p, n_s), (256, 128, 64, 32, 16, 8)))

    y_prompt = y_p.reshape(batch, seq, D_MODEL)
    y_sample = y_s.reshape(dec_b, dec_t, D_MODEL)
    kp = k_p.reshape(batch, seq, 4, HEAD_DIM)[:, -WINDOW:][None]
    vp = v_p.reshape(batch, seq, 4, HEAD_DIM)[:, -WINDOW:][None]
    wp = st_p[None]
    sp = _unpad_rwkv_cols(sh_p[:, 0, :])[None]
    ks = nk_s.reshape(dec_b, wb, 4, HEAD_DIM)[None]
    vs = nv_s.reshape(dec_b, wb, 4, HEAD_DIM)[None]
    ws = st_s[None]
    ss = _unpad_rwkv_cols(sh_s[:, 0, :])[None]
    return (y_prompt, y_sample, kp, vp, wp, sp, ks, vs, ws, ss)
```

```python
import functools
import math

import jax
import jax.numpy as jnp
from jax import lax
from jax.experimental import pallas as pl
from jax.experimental.pallas import tpu as pltpu

F32 = jnp.float32
BF16 = jnp.bfloat16

D_MODEL = 2048
HEAD_DIM = 64
LANES = 128
ATTN_WIDTH = 1024
KV_WIDTH = 256
ATTN_HEADS = 16
RWKV_WIDTH = 1024
DECAY_LORA = 64
ICLR_LORA = 64
GATE_LORA = 160
RWKV_PROJ = 3 * RWKV_WIDTH + DECAY_LORA + ICLR_LORA + GATE_LORA
XW_OFF = 3 * RWKV_WIDTH
XA_OFF = XW_OFF + LANES
XG_OFF = XA_OFF + LANES
XG_PAD = 2 * LANES
RWKV_PROJ_PAD = XG_OFF + XG_PAD
WINDOW = 128
ROPE_THETA = 10000.0
PAST_LEN = 8192
N_EXPERTS = 32
TOP_K = 4
D_FF = 2048
SWIGLU_ALPHA = 1.702
SWIGLU_LIMIT = 7.0
NORM_EPS = 1e-5
LNX_EPS = 64e-5
NEG_BIG = -1e30
VMEM_LIMIT = 56 * 1024 * 1024


def _cparams(sem):
    return pltpu.CompilerParams(dimension_semantics=sem, vmem_limit_bytes=VMEM_LIMIT)


_NN = (((1,), (0,)), ((), ()))
_NT = (((1,), (1,)), ((), ()))
_TN = (((0,), (0,)), ((), ()))


def _mm(a, b, dims=_NN):
    return lax.dot_general(a, b, dims, preferred_element_type=F32)


def _split2(x):
    hi = x.astype(BF16)
    lo = (x - hi.astype(F32)).astype(BF16)
    return hi, lo


def _split3(x):
    hi = x.astype(BF16)
    r1 = x - hi.astype(F32)
    mid = r1.astype(BF16)
    lo = (r1 - mid.astype(F32)).astype(BF16)
    return hi, mid, lo


def _mm_exact_b(a, b_bf16, dims=_NN):
    h, m, l = _split3(a)
    return _mm(h, b_bf16, dims) + (_mm(m, b_bf16, dims) + _mm(l, b_bf16, dims))


def _mm_exact_a(a_bf16, b, dims=_NN):
    h, m, l = _split3(b)
    return _mm(a_bf16, h, dims) + (_mm(a_bf16, m, dims) + _mm(a_bf16, l, dims))


def _rms(x, g):
    ms = jnp.mean(x * x, axis=-1, keepdims=True)
    return (x * lax.rsqrt(ms + NORM_EPS)) * g


def _attn_proj_kernel(x_ref, g_ref, w_ref, cos_ref, sin_ref, q_ref, k_ref, v_ref):
    xn = _rms(x_ref[...], g_ref[...]).astype(BF16)
    a = _mm(xn, w_ref[...])
    cos = cos_ref[...]
    sin = sin_ref[...]
    lane = lax.broadcasted_iota(jnp.int32, cos.shape, 1)
    first_half = (lane % HEAD_DIM) < (HEAD_DIM // 2)
    n_rot = (ATTN_WIDTH + KV_WIDTH) // LANES
    for j in range(n_rot):
        t = a[:, j * LANES:(j + 1) * LANES]
        swapped = jnp.where(first_half, pltpu.roll(t, LANES - HEAD_DIM // 2, 1), pltpu.roll(t, HEAD_DIM // 2, 1))
        rot = t * cos + swapped * sin
        if j < ATTN_WIDTH // LANES:
            q_ref[:, j * LANES:(j + 1) * LANES] = (rot * (HEAD_DIM ** -0.5)).astype(BF16)
        else:
            jj = j - ATTN_WIDTH // LANES
            k_ref[:, jj * LANES:(jj + 1) * LANES] = rot
    v_ref[...] = a[:, ATTN_WIDTH + KV_WIDTH:]


def _attn_proj(x, g, w_attn, cos_t, sin_t, tm):
    t_tok = x.shape[0]
    n_pos_blocks = cos_t.shape[0] // tm
    return pl.pallas_call(
        _attn_proj_kernel,
        grid=(t_tok // tm,),
        in_specs=[
            pl.BlockSpec((tm, D_MODEL), lambda i: (i, 0)),
            pl.BlockSpec((1, D_MODEL), lambda i: (0, 0)),
            pl.BlockSpec((D_MODEL, ATTN_WIDTH + 2 * KV_WIDTH), lambda i: (0, 0)),
            pl.BlockSpec((tm, LANES), lambda i: (i % n_pos_blocks, 0)),
            pl.BlockSpec((tm, LANES), lambda i: (i % n_pos_blocks, 0)),
        ],
        out_specs=[
            pl.BlockSpec((tm, ATTN_WIDTH), lambda i: (i, 0)),
            pl.BlockSpec((tm, KV_WIDTH), lambda i: (i, 0)),
            pl.BlockSpec((tm, KV_WIDTH), lambda i: (i, 0)),
        ],
        out_shape=[
            jax.ShapeDtypeStruct((t_tok, ATTN_WIDTH), BF16),
            jax.ShapeDtypeStruct((t_tok, KV_WIDTH), F32),
            jax.ShapeDtypeStruct((t_tok, KV_WIDTH), F32),
        ],
        compiler_params=_cparams(("parallel",)),
        name="attn_proj",
    )(x, g, w_attn, cos_t, sin_t)


def _rwkv_proj_kernel(x_ref, g_ref, w_ref, p_ref, xn_ref):
    @pl.when(pl.program_id(1) == 0)
    def _():
        xn_ref[...] = _rms(x_ref[...], g_ref[...]).astype(BF16)

    p_ref[...] = _mm(xn_ref[...], w_ref[...])


def _rwkv_proj(x, g, w_rwkv, tm, tn):
    t_tok = x.shape[0]
    return pl.pallas_call(
        _rwkv_proj_kernel,
        grid=(t_tok // tm, RWKV_PROJ_PAD // tn),
        in_specs=[
            pl.BlockSpec((tm, D_MODEL), lambda i, n: (i, 0)),
            pl.BlockSpec((1, D_MODEL), lambda i, n: (0, 0)),
            pl.BlockSpec((D_MODEL, tn), lambda i, n: (0, n)),
        ],
        out_specs=pl.BlockSpec((tm, tn), lambda i, n: (i, n)),
        out_shape=jax.ShapeDtypeStruct((t_tok, RWKV_PROJ_PAD), F32),
        scratch_shapes=[pltpu.VMEM((tm, D_MODEL), BF16)],
        compiler_params=_cparams(("parallel", "arbitrary")),
        name="rwkv_proj",
    )(x, g, w_rwkv)


def _softmax_pv(s, mask, sink, vt):
    s = jnp.where(mask, s, NEG_BIG)
    m = jnp.maximum(jnp.max(s, axis=-1, keepdims=True), sink)
    p = jnp.exp(s - m)
    denom = jnp.sum(p, axis=-1, keepdims=True) + jnp.exp(sink - m)
    return _mm(p.astype(BF16), vt) * (1.0 / denom)


def _attn_prompt_kernel(q_ref, kp_ref, kc_ref, vp_ref, vc_ref, sink_ref, o_ref):
    i = pl.program_id(1)
    blk = q_ref.shape[0]
    kk = jnp.concatenate([kp_ref[...], kc_ref[...]], axis=0).astype(BF16)
    vv = jnp.concatenate([vp_ref[...], vc_ref[...]], axis=0).astype(BF16)
    a = lax.broadcasted_iota(jnp.int32, (blk, 2 * blk), 0)
    c = lax.broadcasted_iota(jnp.int32, (blk, 2 * blk), 1)
    mask = (c > a) & (c <= a + blk) & ((c >= blk) | (i > 0))
    lane_lo = lax.broadcasted_iota(jnp.int32, (blk, LANES), 1) < HEAD_DIM
    n_pairs = KV_WIDTH // LANES
    kts = [kk[:, pair * LANES:(pair + 1) * LANES] for pair in range(n_pairs)]
    vts = [vv[:, pair * LANES:(pair + 1) * LANES] for pair in range(n_pairs)]
    heads = [(pair, j, e) for pair in range(n_pairs) for j in range(4) for e in range(2)]
    scores = []
    for pair, j, e in heads:
        tile = pair * 4 + j
        qt = q_ref[:, tile * LANES:(tile + 1) * LANES]
        qm = jnp.where(lane_lo if e == 0 else jnp.logical_not(lane_lo), qt, jnp.zeros_like(qt))
        scores.append(_mm(qm, kts[pair], _NT))
    probs, scales = [], []
    for s, (pair, j, e) in zip(scores, heads):
        sink = sink_ref[8 * pair + 4 * e + j]
        s = jnp.where(mask, s, NEG_BIG)
        m = jnp.maximum(jnp.max(s, axis=-1, keepdims=True), sink)
        p = jnp.exp(s - m)
        probs.append(p.astype(BF16))
        scales.append(1.0 / (jnp.sum(p, axis=-1, keepdims=True) + jnp.exp(sink - m)))
    outs = [_mm(p, vts[pair]) * sc for p, sc, (pair, j, e) in zip(probs, scales, heads)]
    for tile in range(ATTN_WIDTH // LANES):
        o_ref[:, tile * LANES:(tile + 1) * LANES] = jnp.where(lane_lo, outs[2 * tile], outs[2 * tile + 1]).astype(BF16)


def _attn_prompt(q, k, v, sinks, batch, seq):
    blk = WINDOW
    nb = seq // blk
    cur = lambda b, i: (b * nb + i, 0)
    prev = lambda b, i: (b * nb + jnp.maximum(i - 1, 0), 0)
    return pl.pallas_call(
        _attn_prompt_kernel,
        grid=(batch, nb),
        in_specs=[
            pl.BlockSpec((blk, ATTN_WIDTH), cur),
            pl.BlockSpec((blk, KV_WIDTH), prev),
            pl.BlockSpec((blk, KV_WIDTH), cur),
            pl.BlockSpec((blk, KV_WIDTH), prev),
            pl.BlockSpec((blk, KV_WIDTH), cur),
            pl.BlockSpec(memory_space=pltpu.SMEM),
        ],
        out_specs=pl.BlockSpec((blk, ATTN_WIDTH), cur),
        out_shape=jax.ShapeDtypeStruct((batch * seq, ATTN_WIDTH), BF16),
        compiler_params=_cparams(("parallel", "arbitrary")),
        name="attn_prompt",
    )(q, k, k, v, v, sinks)


def _attn_sample_kernel(q_ref, kn_ref, vn_ref, ck_ref, cv_ref, sink_ref, o_ref, nk_ref, nv_ref):
    bb, t_len, _ = q_ref.shape
    wb = ck_ref.shape[1]
    n_keys = wb + t_len
    rows = 8 * t_len
    a = lax.broadcasted_iota(jnp.int32, (rows, n_keys), 0) % t_len
    c = lax.broadcasted_iota(jnp.int32, (rows, n_keys), 1)
    mask = ((c < wb) & (c > a + (wb - WINDOW))) | ((c >= wb) & (c - wb <= a))
    lane_lo = lax.broadcasted_iota(jnp.int32, (t_len, LANES), 1) < HEAD_DIM

    def body(b, carry):
        ck = ck_ref[b]
        cv = cv_ref[b]
        kn = kn_ref[b]
        vn = vn_ref[b]
        nk_ref[b, 0:wb - t_len, :] = ck[t_len:, :]
        nk_ref[b, wb - t_len:wb, :] = kn
        nv_ref[b, 0:wb - t_len, :] = cv[t_len:, :]
        nv_ref[b, wb - t_len:wb, :] = vn
        k_all = jnp.concatenate([ck, kn], axis=0).astype(BF16)
        v_all = jnp.concatenate([cv, vn], axis=0).astype(BF16)
        qb = q_ref[b]
        pairs = range(KV_WIDTH // LANES)
        scores = []
        for pair in pairs:
            stack = []
            for e in range(2):
                for j in range(4):
                    tile = pair * 4 + j
                    qt = qb[:, tile * LANES:(tile + 1) * LANES]
                    stack.append(jnp.where(lane_lo if e == 0 else jnp.logical_not(lane_lo), qt, jnp.zeros_like(qt)))
            qs = jnp.concatenate(stack, axis=0)
            scores.append(_mm(qs, k_all[:, pair * LANES:(pair + 1) * LANES], _NT))
        outs = [_softmax_pv(scores[pair], mask, sink_ref[pair], v_all[:, pair * LANES:(pair + 1) * LANES])
                for pair in pairs]
        for pair in pairs:
            o = outs[pair]
            for j in range(4):
                tile = pair * 4 + j
                lo = o[j * t_len:(j + 1) * t_len, :]
                hi = o[(4 + j) * t_len:(5 + j) * t_len, :]
                o_ref[b, :, tile * LANES:(tile + 1) * LANES] = jnp.where(lane_lo, lo, hi).astype(BF16)
        return carry

    lax.fori_loop(0, bb, body, 0)


def _attn_sample(q3, k3, v3, ck, cv, sink_rows, bb):
    dec_b, t_len, _ = q3.shape
    wb = ck.shape[1]
    blk3 = lambda w: pl.BlockSpec((bb, t_len, w), lambda i: (i, 0, 0))
    cache = pl.BlockSpec((bb, wb, KV_WIDTH), lambda i: (i, 0, 0))
    return pl.pallas_call(
        _attn_sample_kernel,
        grid=(dec_b // bb,),
        in_specs=[blk3(ATTN_WIDTH), blk3(KV_WIDTH), blk3(KV_WIDTH), cache, cache,
                  pl.BlockSpec((2, 8 * t_len, 1), lambda i: (0, 0, 0))],
        out_specs=[blk3(ATTN_WIDTH), cache, cache],
        out_shape=[
            jax.ShapeDtypeStruct((dec_b, t_len, ATTN_WIDTH), BF16),
            jax.ShapeDtypeStruct(ck.shape, F32),
            jax.ShapeDtypeStruct(cv.shape, F32),
        ],
        compiler_params=_cparams(("parallel",)),
        name="attn_sample",
    )(q3, k3, v3, ck, cv, sink_rows)


GROUP = 4 * HEAD_DIM


def _bf(x):
    return x.astype(BF16)


def _seg_sum4(x, seg_mat):
    tiles = [_split2(x[:, t * GROUP:(t + 1) * GROUP]) for t in range(x.shape[1] // GROUP)]
    return jnp.concatenate([_mm(h, seg_mat) + _mm(l, seg_mat) for h, l in tiles], axis=1)


def _stack_heads(x, head_masks):
    zero = jnp.zeros_like(x)
    return jnp.concatenate([jnp.where(m, x, zero) for m in head_masks], axis=0)


def _sum_blocks(x, r):
    return (x[0:r] + x[r:2 * r]) + (x[2 * r:3 * r] + x[3 * r:4 * r])


def _place_heads(blocks):
    rows = []
    for h, blk in enumerate(blocks):
        rows.append(jnp.concatenate([blk if j == h else jnp.zeros_like(blk) for j in range(len(blocks))], axis=1))
    return jnp.concatenate(rows, axis=0)


def _rwkv4_kernel(p_ref, s_in_ref, sh_ref, mu_ref, w0_ref, wd_ref, a0_ref, wa_ref, wg_ref, kk_ref, ka_ref,
                  rk_ref, lnw_ref, lnb_ref, seg_ref, o_ref, s_out_ref, sh_out_ref, state_scr, prev_scr,
                  *, nseq, tlen, npar):
    rows = nseq * tlen
    heads = GROUP // HEAD_DIM
    log_t = tlen.bit_length() - 1
    log_r = rows.bit_length() - 1
    log_h = HEAD_DIM.bit_length() - 1

    ri = lax.broadcasted_iota(jnp.int32, (rows, rows), 0)
    cj = lax.broadcasted_iota(jnp.int32, (rows, rows), 1)
    bri = lax.broadcasted_iota(jnp.int32, (heads * rows, heads * rows), 0)
    bcj = lax.broadcasted_iota(jnp.int32, (heads * rows, heads * rows), 1)
    same = ((bri >> log_r) == (bcj >> log_r)) & ((bri >> log_t) == (bcj >> log_t))
    lane = lax.broadcasted_iota(jnp.int32, (rows, GROUP), 1)
    sr = lax.broadcasted_iota(jnp.int32, (GROUP, GROUP), 0)
    sc = lax.broadcasted_iota(jnp.int32, (GROUP, GROUP), 1)
    masks = dict(
        tri=jnp.where(((ri >> log_t) == (cj >> log_t)) & (ri >= cj), 1.0, 0.0).astype(BF16),
        strict_bd=same & (bri > bcj),
        incl_bd=same & (bri >= bcj),
        eye_bd=jnp.where(bri == bcj, 1.0, 0.0).astype(F32),
        head_masks=[(lane >> log_h) == h for h in range(heads)],
        state_bd=(sr >> log_h) == (sc >> log_h),
    )
    refs = (p_ref, s_in_ref, sh_ref, mu_ref, w0_ref, wd_ref, a0_ref, wa_ref, wg_ref, kk_ref, ka_ref, rk_ref,
            lnw_ref, lnb_ref, seg_ref, o_ref, s_out_ref, sh_out_ref, state_scr, prev_scr)
    for j in range(npar):
        _rwkv4_block(j, refs, masks, nseq=nseq, tlen=tlen)


def _rwkv4_block(j, refs, masks, *, nseq, tlen):
    (p_ref, s_in_ref, sh_ref, mu_ref, w0_ref, wd_ref, a0_ref, wa_ref, wg_ref, kk_ref, ka_ref, rk_ref,
     lnw_ref, lnb_ref, seg_ref, o_ref, s_out_ref, sh_out_ref, state_scr, prev_scr) = refs
    ci = pl.program_id(1)
    rows = nseq * tlen
    n_groups = RWKV_WIDTH // GROUP
    heads = GROUP // HEAD_DIM
    log_t = tlen.bit_length() - 1
    carry = nseq == 1

    p = p_ref[j] if carry else p_ref[...]
    rowi = lax.broadcasted_iota(jnp.int32, (rows, 1), 0)
    rolled = pltpu.roll(p, 1, 0)
    if carry:
        @pl.when(ci == 0)
        def _():
            prev_scr[j] = sh_ref[j]
            for g in range(n_groups):
                state_scr[j * n_groups + g] = _place_heads([s_in_ref[j, heads * g + h] for h in range(heads)])

        p_prev = jnp.where(rowi == 0, prev_scr[j], rolled)
        prev_scr[j] = p[rows - 1:rows, :]
    else:
        p_prev = rolled
        for s in range(nseq):
            p_prev = jnp.where(rowi == s * tlen, sh_ref[s], p_prev)
            sh_out_ref[s] = p[(s + 1) * tlen - 1:(s + 1) * tlen, :]

    xs = p + (p_prev - p) * mu_ref[...]
    r = xs[:, 0:RWKV_WIDTH]
    k = xs[:, RWKV_WIDTH:2 * RWKV_WIDTH]
    v = xs[:, 2 * RWKV_WIDTH:3 * RWKV_WIDTH]
    xw = xs[:, XW_OFF:XW_OFF + LANES]
    xa = xs[:, XA_OFF:XA_OFF + LANES]
    xg = xs[:, XG_OFF:XG_OFF + XG_PAD]
    z = w0_ref[...] + _mm(_bf(jnp.tanh(xw)), wd_ref[...])
    w_log = -jax.nn.softplus(-z) - 0.5
    lw = -jnp.exp(w_log)
    a = jax.nn.sigmoid(a0_ref[...] + _mm(_bf(xa), wa_ref[...]))
    gate = _mm(_bf(jax.nn.sigmoid(xg)), wg_ref[...])
    seg = seg_ref[...]
    kk = k * kk_ref[...]
    kap = kk / jnp.maximum(jnp.sqrt(_seg_sum4(kk * kk, seg)), 1e-12)
    k2 = k * (1.0 + (a - 1.0) * ka_ref[...])
    b = kap * a
    bonus = _seg_sum4(r * k2 * rk_ref[...], seg) * v

    cum = _mm_exact_a(masks["tri"], lw)
    eg = jnp.exp(cum)
    kap_t = kap * jnp.exp(cum - lw)
    r_t = r * eg
    einv = jnp.exp(-cum)
    b_t = b * einv
    k_t = k2 * einv

    strict_bd, incl_bd, eye_bd = masks["strict_bd"], masks["incl_bd"], masks["eye_bd"]
    head_masks, state_bd = masks["head_masks"], masks["state_bd"]
    zero_bd = jnp.zeros((heads * rows, heads * rows), F32)
    zero_st = jnp.zeros((GROUP, GROUP), F32)

    groups = range(n_groups)
    hr = heads * rows
    sls = [slice(g * GROUP, (g + 1) * GROUP) for g in groups]
    kap_g = [kap_t[:, sl] for sl in sls]
    r_g = [r_t[:, sl] for sl in sls]
    b_g = [b_t[:, sl] for sl in sls]
    k_g = [k_t[:, sl] for sl in sls]
    v_g = [v[:, sl] for sl in sls]
    lkr = [jnp.concatenate([_stack_heads(_bf(kap_g[g]), head_masks), _stack_heads(_bf(r_g[g]), head_masks)], axis=0)
           for g in groups]
    gb = [_mm(lkr[g], _stack_heads(_bf(b_g[g]), head_masks), _NT) for g in groups]
    gk = [_mm(lkr[g], _stack_heads(_bf(k_g[g]), head_masks), _NT) for g in groups]
    n = [-jnp.where(strict_bd, gb[g][:hr], zero_bd) for g in groups]
    t_inv = [eye_bd + n[g] for g in groups]
    for _ in range(max(log_t - 1, 0)):
        nb = [_bf(n[g]) for g in groups]
        n = [_mm(nb[g], nb[g]) for g in groups]
        t_inv = [t_inv[g] + _mm(_bf(t_inv[g]), _bf(n[g])) for g in groups]
    a_k = [_bf(jnp.where(strict_bd, gk[g][:hr], zero_bd)) for g in groups]
    r_b = [_bf(jnp.where(incl_bd, gb[g][hr:], zero_bd)) for g in groups]
    r_k = [_bf(jnp.where(incl_bd, gk[g][hr:], zero_bd)) for g in groups]
    v_stack = [_stack_heads(_bf(v_g[g]), head_masks) for g in groups]

    if carry:
        s0 = [state_scr[j * n_groups + g] for g in groups]
        sk = [_mm(_bf(jnp.concatenate([kap_g[g], r_g[g]], axis=0)), _bf(s0[g]), _NT) for g in groups]
        k_s = [sk[g][:rows] for g in groups]
        r_s = [sk[g][rows:] for g in groups]
    else:
        s0, k_s, r_s = [], [], []
        for g in groups:
            s0_list, ks_list, rs_list = [], [], []
            for s in range(nseq):
                rs = slice(s * tlen, (s + 1) * tlen)
                st = _place_heads([s_in_ref[s, heads * g + h] for h in range(heads)])
                sk = _mm(_bf(jnp.concatenate([kap_g[g][rs], r_g[g][rs]], axis=0)), _bf(st), _NT)
                s0_list.append(st)
                ks_list.append(sk[:tlen])
                rs_list.append(sk[tlen:])
            s0.append(s0_list)
            k_s.append(jnp.concatenate(ks_list, axis=0))
            r_s.append(jnp.concatenate(rs_list, axis=0))

    w = [_sum_blocks(_mm(a_k[g], v_stack[g]), rows) for g in groups]
    rhs = [-(k_s[g] + w[g]) for g in groups]
    u_stack = [_mm(_bf(t_inv[g]), _stack_heads(_bf(rhs[g]), head_masks)) for g in groups]
    u = [_sum_blocks(u_stack[g], rows) for g in groups]
    y_stack = [_mm(r_b[g], _bf(u_stack[g])) + _mm(r_k[g], v_stack[g]) for g in groups]
    ys = [r_s[g] + _sum_blocks(y_stack[g], rows) for g in groups]

    for g in groups:
        if carry:
            d = _mm(_bf(jnp.concatenate([u[g], v_g[g]], axis=0)), _bf(jnp.concatenate([b_g[g], k_g[g]], axis=0)), _TN)
            state_scr[j * n_groups + g] = (s0[g] + jnp.where(state_bd, d, zero_st)) * eg[rows - 1:rows, sls[g]]
        else:
            for s in range(nseq):
                rs = slice(s * tlen, (s + 1) * tlen)
                d = _mm(_bf(jnp.concatenate([u[g][rs], v_g[g][rs]], axis=0)),
                        _bf(jnp.concatenate([b_g[g][rs], k_g[g][rs]], axis=0)), _TN)
                s_new = (s0[g][s] + jnp.where(state_bd, d, zero_st)) * eg[(s + 1) * tlen - 1:(s + 1) * tlen, sls[g]]
                for h in range(heads):
                    hs = slice(h * HEAD_DIM, (h + 1) * HEAD_DIM)
                    s_out_ref[s, heads * g + h] = s_new[hs, hs]

    y = jnp.concatenate(ys, axis=1)
    inv_n = 1.0 / HEAD_DIM
    mean = _seg_sum4(y, seg) * inv_n
    yc = y - mean
    var = _seg_sum4(yc * yc, seg) * inv_n
    yn = yc * lax.rsqrt(var + LNX_EPS) * lnw_ref[...] + lnb_ref[...]
    out = ((yn + bonus) * gate).astype(o_ref.dtype)
    if carry:
        o_ref[j] = out

        @pl.when(ci == pl.num_programs(1) - 1)
        def _():
            sh_out_ref[j] = prev_scr[j]
            for g in range(n_groups):
                st = state_scr[j * n_groups + g]
                for h in range(heads):
                    hs = slice(h * HEAD_DIM, (h + 1) * HEAD_DIM)
                    s_out_ref[j, heads * g + h] = st[hs, hs]
    else:
        o_ref[...] = out


def _rwkv_mix4(p2, s_in, shift3, prm, nseq, tlen, npar=1):
    batch = s_in.shape[0]
    t_len = p2.shape[0] // batch
    n_chunks = t_len // tlen
    rows = nseq * tlen
    n_heads = RWKV_WIDTH // HEAD_DIM
    per_step = nseq * npar
    const2 = lambda shape: pl.BlockSpec(shape, lambda b, c: (0, 0))
    vec = const2((1, RWKV_WIDTH))
    state_spec = pl.BlockSpec((per_step, n_heads, HEAD_DIM, HEAD_DIM), lambda b, c: (b, 0, 0, 0))
    shift_spec = pl.BlockSpec((per_step, 1, RWKV_PROJ_PAD), lambda b, c: (b, 0, 0))
    if nseq == 1:
        p_in = p2.reshape(batch, t_len, RWKV_PROJ_PAD)
        p_spec = pl.BlockSpec((npar, tlen, RWKV_PROJ_PAD), lambda b, c: (b, c, 0))
        o_spec = pl.BlockSpec((npar, tlen, RWKV_WIDTH), lambda b, c: (b, c, 0))
        o_shape = jax.ShapeDtypeStruct((batch, t_len, RWKV_WIDTH), BF16)
    else:
        p_in = p2
        p_spec = pl.BlockSpec((rows, RWKV_PROJ_PAD), lambda b, c: (b * n_chunks + c, 0))
        o_spec = pl.BlockSpec((rows, RWKV_WIDTH), lambda b, c: (b * n_chunks + c, 0))
        o_shape = jax.ShapeDtypeStruct((batch * t_len, RWKV_WIDTH), BF16)
    o, s_out, sh_out = pl.pallas_call(
        functools.partial(_rwkv4_kernel, nseq=nseq, tlen=tlen, npar=npar),
        grid=(batch // per_step, n_chunks),
        in_specs=[
            p_spec,
            state_spec,
            shift_spec,
            const2((1, RWKV_PROJ_PAD)),
            vec,
            const2((LANES, RWKV_WIDTH)),
            vec,
            const2((LANES, RWKV_WIDTH)),
            const2((XG_PAD, RWKV_WIDTH)),
            vec, vec, vec, vec, vec,
            const2((GROUP, GROUP)),
        ],
        out_specs=[o_spec, state_spec, shift_spec],
        out_shape=[
            o_shape,
            jax.ShapeDtypeStruct(s_in.shape, F32),
            jax.ShapeDtypeStruct(shift3.shape, F32),
        ],
        scratch_shapes=[pltpu.VMEM((npar * (RWKV_WIDTH // GROUP), GROUP, GROUP), F32),
                        pltpu.VMEM((npar, 1, RWKV_PROJ_PAD), F32)],
        compiler_params=_cparams(("parallel", "arbitrary")),
        name="rwkv_mix_t%d" % tlen,
    )(p_in, s_in, shift3, prm["mu"], prm["w0"], prm["wd"], prm["a0"], prm["wa"], prm["wg"], prm["k_k"],
      prm["k_a"], prm["r_k"], prm["lnx_w"], prm["lnx_b"], prm["seg"])
    return o.reshape(batch * t_len, RWKV_WIDTH), s_out, sh_out


def _out_router_kernel(xp_ref, xs_ref, oap_ref, oas_ref, orp_ref, ors_ref, wa_ref, wr_ref, g_ref, rw_ref, rb_ref,
                       h_ref, hn_ref, idx_ref, gate_ref, *, n_p_tiles):
    body = functools.partial(_out_router_body, wa_ref=wa_ref, wr_ref=wr_ref, g_ref=g_ref, rw_ref=rw_ref,
                             rb_ref=rb_ref, h_ref=h_ref, hn_ref=hn_ref, idx_ref=idx_ref, gate_ref=gate_ref)
    is_prompt = pl.program_id(0) < n_p_tiles
    pl.when(is_prompt)(functools.partial(body, xp_ref, oap_ref, orp_ref))
    pl.when(jnp.logical_not(is_prompt))(functools.partial(body, xs_ref, oas_ref, ors_ref))


def _out_router_body(x_ref, oa_ref, or_ref, *, wa_ref, wr_ref, g_ref, rw_ref, rb_ref, h_ref, hn_ref, idx_ref,
                     gate_ref):
    h = x_ref[...] + _mm(oa_ref[...], wa_ref[...]) + _mm(or_ref[...], wr_ref[...])
    h_ref[...] = h
    hn = _rms(h, g_ref[...])
    hh, hl = _split2(hn)
    bits = lax.bitcast_convert_type(hh.astype(F32), jnp.int32)
    half = D_MODEL // 2
    hn_ref[...] = ((bits[:, :half] >> 16) & jnp.int32(0xFFFF)) | (bits[:, half:] & jnp.int32(-65536))
    logits = _mm(hh, rw_ref[0]) + (_mm(hh, rw_ref[1]) + _mm(hl, rw_ref[0])) + rb_ref[...]
    lane = lax.broadcasted_iota(jnp.int32, logits.shape, 1)
    vals = []
    idxs = []
    cur = logits
    for _ in range(TOP_K):
        m = jnp.max(cur, axis=-1, keepdims=True)
        sel = jnp.min(jnp.where(cur == m, lane, LANES), axis=-1, keepdims=True)
        vals.append(m)
        idxs.append(sel)
        cur = jnp.where(lane == sel, -jnp.inf, cur)
    es = [jnp.exp(vj - vals[0]) for vj in vals]
    tot = es[0] + es[1] + es[2] + es[3]
    idx_out = jnp.zeros(logits.shape, jnp.int32)
    gate_out = jnp.zeros(logits.shape, F32)
    for j in range(TOP_K):
        idx_out = jnp.where(lane == j, idxs[j], idx_out)
        gate_out = jnp.where(lane == j, es[j] / tot, gate_out)
    idx_ref[...] = idx_out
    gate_ref[...] = gate_out


def _out_router(x_p, x_s, oa_p, oa_s, or_p, or_s, w_oa, w_or, g, rw_split, rb_pad, tm):
    n_p_tiles = x_p.shape[0] // tm
    t_tok = x_p.shape[0] + x_s.shape[0]
    row = lambda w: pl.BlockSpec((tm, w), lambda i: (i, 0))
    row_p = lambda w: pl.BlockSpec((tm, w), lambda i: (jnp.minimum(i, n_p_tiles - 1), 0))
    row_s = lambda w: pl.BlockSpec((tm, w), lambda i: (jnp.maximum(i - n_p_tiles, 0), 0))
    full = lambda shape: pl.BlockSpec(shape, lambda i: tuple(0 for _ in shape))
    return pl.pallas_call(
        functools.partial(_out_router_kernel, n_p_tiles=n_p_tiles),
        grid=(t_tok // tm,),
        in_specs=[row_p(D_MODEL), row_s(D_MODEL), row_p(ATTN_WIDTH), row_s(ATTN_WIDTH), row_p(RWKV_WIDTH),
                  row_s(RWKV_WIDTH), full((ATTN_WIDTH, D_MODEL)), full((RWKV_WIDTH, D_MODEL)), full((1, D_MODEL)),
                  full((2, D_MODEL, LANES)), full((1, LANES))],
        out_specs=[row(D_MODEL), row(D_MODEL // 2), row(LANES), row(LANES)],
        out_shape=[
            jax.ShapeDtypeStruct((t_tok, D_MODEL), F32),
            jax.ShapeDtypeStruct((t_tok, D_MODEL // 2), jnp.int32),
            jax.ShapeDtypeStruct((t_tok, LANES), jnp.int32),
            jax.ShapeDtypeStruct((t_tok, LANES), F32),
        ],
        compiler_params=_cparams(("arbitrary",)),
        name="out_router",
    )(x_p, x_s, oa_p, oa_s, or_p, or_s, w_oa, w_or, g, rw_split, rb_pad)


GATHER_UNROLL = 8
MOE_SUB = 256
GATHER_ROWS = 256


def _gather_kernel(nv_ref, tok_ref, tok_next_ref, hn_ref, o_ref, buf, sem):
    rows = buf.shape[1]
    i = pl.program_id(0)
    last = pl.num_programs(0) - 1
    slot = i % 2

    def issue(tok, dst_slot):
        def body(r8, carry):
            for u in range(GATHER_UNROLL):
                r = r8 * GATHER_UNROLL + u
                pltpu.make_async_copy(hn_ref.at[pl.ds(tok[r], 1)], buf.at[dst_slot, pl.ds(r, 1)],
                                      sem.at[dst_slot]).start()
            return carry

        lax.fori_loop(0, rows // GATHER_UNROLL, body, 0)

    @pl.when((i == 0) & (nv_ref[0] > 0))
    def _():
        issue(tok_ref, 0)

    @pl.when((i < last) & (nv_ref[jnp.minimum(i + 1, last)] > 0))
    def _():
        issue(tok_next_ref, 1 - slot)

    @pl.when(nv_ref[i] > 0)
    def _():
        pltpu.make_async_copy(hn_ref.at[pl.ds(0, rows)], buf.at[slot], sem.at[slot]).wait()
        words = buf[slot]
        half = words.shape[1]
        o_ref[:, :half] = lax.bitcast_convert_type(words << 16, F32).astype(o_ref.dtype)
        o_ref[:, half:] = lax.bitcast_convert_type(words & jnp.int32(-65536), F32).astype(o_ref.dtype)

    @pl.when(nv_ref[i] == 0)
    def _():
        o_ref[...] = jnp.zeros_like(o_ref)


def _gather_rows(sub_valid, row_tok, hn, rows_per_step):
    n_rows = row_tok.shape[0]
    n_steps = n_rows // rows_per_step
    grid_spec = pltpu.PrefetchScalarGridSpec(
        num_scalar_prefetch=1,
        grid=(n_steps,),
        in_specs=[
            pl.BlockSpec((rows_per_step,), lambda i, nv: (i,), memory_space=pltpu.SMEM),
            pl.BlockSpec((rows_per_step,), lambda i, nv: (jnp.minimum(i + 1, n_steps - 1),),
                         memory_space=pltpu.SMEM),
            pl.BlockSpec(memory_space=pl.ANY),
        ],
        out_specs=pl.BlockSpec((rows_per_step, D_MODEL), lambda i, nv: (i, 0)),
        scratch_shapes=[pltpu.VMEM((2, rows_per_step, D_MODEL // 2), jnp.int32), pltpu.SemaphoreType.DMA((2,))],
    )
    return pl.pallas_call(
        _gather_kernel,
        grid_spec=grid_spec,
        out_shape=jax.ShapeDtypeStruct((n_rows, D_MODEL), BF16),
        compiler_params=_cparams(("arbitrary",)),
        name="moe_gather",
    )(sub_valid, row_tok, row_tok, hn)


def _expert_kernel(be_ref, nv_ref, nr_ref, x_ref, wu_ref, bu_ref, wd_ref, bd_ref, sel_ref, o_ref):
    i = pl.program_id(0)
    f = pl.program_id(1)
    n_sub = x_ref.shape[0] // MOE_SUB
    used_sub = (nv_ref[i] + (MOE_SUB - 1)) // MOE_SUB

    def body(m_rows):
        rows = slice(0, m_rows)
        z = _mm(x_ref[rows, :], wu_ref[0].astype(BF16)) + bu_ref[0]
        zn = pltpu.roll(z, z.shape[1] - 1, 1)
        glu = jnp.minimum(z, SWIGLU_LIMIT)
        lin = jnp.clip(zn, -SWIGLU_LIMIT, SWIGLU_LIMIT)
        act = (glu * jax.nn.sigmoid(SWIGLU_ALPHA * glu) * (lin + 1.0)).astype(BF16)
        actc = _mm(act, sel_ref[...]).astype(BF16)
        wd = wd_ref[0].astype(BF16)

        @pl.when(f == 0)
        def _():
            o_ref[rows, :] = _mm(actc, wd) + bd_ref[0]
            if m_rows < o_ref.shape[0]:
                o_ref[m_rows:, :] = jnp.zeros((o_ref.shape[0] - m_rows, o_ref.shape[1]), o_ref.dtype)

        @pl.when(f > 0)
        def _():
            o_ref[rows, :] += _mm(actc, wd)

    for k in range(1, n_sub + 1):
        pl.when(used_sub == k)(functools.partial(body, k * MOE_SUB))

    @pl.when((used_sub == 0) & (f == 0))
    def _():
        o_ref[...] = jnp.zeros_like(o_ref)


def _experts(blk_expert, blk_valid, n_real, x_sorted, w_up, b_up3, w_down, b_down3, sel, bm, tf):
    n_rows = x_sorted.shape[0]
    n_f = D_FF // tf

    def real(i, nr):
        return jnp.minimum(i, nr[0] - 1)

    def f_eff(i, f, nr):
        return jnp.where(i < nr[0], f, n_f - 1)

    grid_spec = pltpu.PrefetchScalarGridSpec(
        num_scalar_prefetch=3,
        grid=(n_rows // bm, n_f),
        in_specs=[
            pl.BlockSpec((bm, D_MODEL), lambda i, f, be, nv, nr: (real(i, nr), 0)),
            pl.BlockSpec((1, D_MODEL, 2 * tf), lambda i, f, be, nv, nr: (be[i], 0, f_eff(i, f, nr))),
            pl.BlockSpec((1, 1, 2 * tf), lambda i, f, be, nv, nr: (be[i], 0, f_eff(i, f, nr))),
            pl.BlockSpec((1, tf, D_MODEL), lambda i, f, be, nv, nr: (be[i], f_eff(i, f, nr), 0)),
            pl.BlockSpec((1, 1, D_MODEL), lambda i, f, be, nv, nr: (be[i], 0, 0)),
            pl.BlockSpec((2 * tf, tf), lambda i, f, be, nv, nr: (0, 0)),
        ],
        out_specs=pl.BlockSpec((bm, D_MODEL), lambda i, f, be, nv, nr: (i, 0)),
    )
    return pl.pallas_call(
        _expert_kernel,
        grid_spec=grid_spec,
        out_shape=jax.ShapeDtypeStruct((n_rows, D_MODEL), F32),
        compiler_params=_cparams(("arbitrary", "arbitrary")),
        name="moe_experts",
    )(blk_expert, blk_valid, n_real, x_sorted, w_up, b_up3, w_down, b_down3, sel)


def _combine_kernel(dest_ref, dest_next_ref, rows_ref, h_ref, gate_ref, g_ref, op_ref, os_ref, buf, sem,
                    *, n_p_tiles):
    tm = h_ref.shape[0]
    i = pl.program_id(0)
    last = pl.num_programs(0) - 1
    slot = i % 2

    def issue(dest, dst_slot):
        def body(t2, carry):
            for u in range(2):
                t = t2 * 2 + u
                for j in range(TOP_K):
                    pltpu.make_async_copy(rows_ref.at[pl.ds(dest[t * TOP_K + j], 1)],
                                          buf.at[dst_slot, pl.ds(j * tm + t, 1)], sem.at[dst_slot]).start()
            return carry

        lax.fori_loop(0, tm // 2, body, 0)

    @pl.when(i == 0)
    def _():
        issue(dest_ref, 0)

    @pl.when(i < last)
    def _():
        issue(dest_next_ref, 1 - slot)

    pltpu.make_async_copy(rows_ref.at[pl.ds(0, TOP_K * tm)], buf.at[slot], sem.at[slot]).wait()
    gates = gate_ref[...]
    y = h_ref[...]
    for j in range(TOP_K):
        y = y + buf[slot, j * tm:(j + 1) * tm, :] * gates[:, j:j + 1]
    out = _rms(y, g_ref[...])
    is_prompt = pl.program_id(0) < n_p_tiles

    @pl.when(is_prompt)
    def _():
        op_ref[...] = out

    @pl.when(jnp.logical_not(is_prompt))
    def _():
        os_ref[...] = out


def _combine(dest, out_rows, h, gates, g, n_p, tm):
    t_tok = h.shape[0]
    n_p_tiles = n_p // tm
    n_steps = t_tok // tm
    return pl.pallas_call(
        functools.partial(_combine_kernel, n_p_tiles=n_p_tiles),
        grid=(n_steps,),
        in_specs=[
            pl.BlockSpec((tm * TOP_K,), lambda i: (i,), memory_space=pltpu.SMEM),
            pl.BlockSpec((tm * TOP_K,), lambda i: (jnp.minimum(i + 1, n_steps - 1),), memory_space=pltpu.SMEM),
            pl.BlockSpec(memory_space=pl.ANY),
            pl.BlockSpec((tm, D_MODEL), lambda i: (i, 0)),
            pl.BlockSpec((tm, LANES), lambda i: (i, 0)),
            pl.BlockSpec((1, D_MODEL), lambda i: (0, 0)),
        ],
        out_specs=[pl.BlockSpec((tm, D_MODEL), lambda i: (jnp.minimum(i, n_p_tiles - 1), 0)),
                   pl.BlockSpec((tm, D_MODEL), lambda i: (jnp.maximum(i - n_p_tiles, 0), 0))],
        out_shape=[jax.ShapeDtypeStruct((n_p, D_MODEL), F32),
                   jax.ShapeDtypeStruct((t_tok - n_p, D_MODEL), F32)],
        scratch_shapes=[pltpu.VMEM((2, TOP_K * tm, D_MODEL), F32), pltpu.SemaphoreType.DMA((2,))],
        compiler_params=_cparams(("arbitrary",)),
        name="moe_combine",
    )(dest, dest, out_rows, h, gates, g)


def _q_head_order():
    order = []
    for pair in range(2):
        for j in range(4):
            order += [8 * pair + j, 8 * pair + 4 + j]
    return order


def _q_col_perm():
    cols = []
    for h in _q_head_order():
        cols += list(range(h * HEAD_DIM, (h + 1) * HEAD_DIM))
    return jnp.asarray(cols, jnp.int32)


def _pad_rwkv_cols(m):
    def padw(a, w):
        return jnp.pad(a, [(0, 0)] * (a.ndim - 1) + [(0, w - a.shape[-1])])
    return jnp.concatenate([
        m[..., :XW_OFF],
        padw(m[..., XW_OFF:XW_OFF + DECAY_LORA], LANES),
        padw(m[..., XW_OFF + DECAY_LORA:XW_OFF + DECAY_LORA + ICLR_LORA], LANES),
        padw(m[..., XW_OFF + DECAY_LORA + ICLR_LORA:], XG_PAD),
    ], axis=-1)


def _unpad_rwkv_cols(m):
    return jnp.concatenate([
        m[..., :XW_OFF + DECAY_LORA],
        m[..., XA_OFF:XA_OFF + ICLR_LORA],
        m[..., XG_OFF:XG_OFF + GATE_LORA],
    ], axis=-1)


def _pad_rows(m, rows):
    return jnp.pad(m, ((0, rows - m.shape[0]), (0, 0)))


def _rope_tables(pos):
    half = HEAD_DIM // 2
    inv_freq = ROPE_THETA ** (-jnp.arange(half, dtype=F32) / half)
    ang = pos.astype(F32)[:, None] * inv_freq[None, :]
    cos = jnp.cos(ang)
    sin = jnp.sin(ang)
    cos_t = jnp.tile(jnp.concatenate([cos, cos], axis=-1), (1, LANES // HEAD_DIM))
    sin_t = jnp.tile(jnp.concatenate([-sin, sin], axis=-1), (1, LANES // HEAD_DIM))
    return cos_t, sin_t


def _route(top_idx, bm):
    n_tok = top_idx.shape[0]
    e_flat = top_idx.reshape(-1)
    onehot = (e_flat[:, None] == jnp.arange(N_EXPERTS, dtype=jnp.int32)[None, :]).astype(jnp.int32)
    csum = jnp.cumsum(onehot, axis=0)
    rank = jnp.sum(csum * onehot, axis=1) - 1
    counts = csum[-1]
    padded = ((counts + bm - 1) // bm) * bm
    p_end = jnp.cumsum(padded)
    p_start = p_end - padded
    dest = (p_start[e_flat] + rank).astype(jnp.int32)
    n_assign = n_tok * TOP_K
    n_blocks = -(-n_assign // bm) + N_EXPERTS
    n_rows = n_blocks * bm
    tok_flat = jnp.repeat(jnp.arange(n_tok, dtype=jnp.int32), TOP_K)
    row_tok = (jnp.arange(n_rows, dtype=jnp.int32) % n_tok).at[dest].set(tok_flat, unique_indices=True)
    blk_start = jnp.arange(n_blocks, dtype=jnp.int32) * bm
    blk_expert = jnp.minimum(jnp.searchsorted(p_end, blk_start, side="right"), N_EXPERTS - 1).astype(jnp.int32)
    blk_valid = jnp.clip(counts[blk_expert] - (blk_start - p_start[blk_expert]), 0, bm).astype(jnp.int32)
    sub_start = jnp.arange(n_rows // MOE_SUB, dtype=jnp.int32) * MOE_SUB
    sub_expert = blk_expert[sub_start // bm]
    sub_valid = jnp.clip(counts[sub_expert] - (sub_start - p_start[sub_expert]), 0, MOE_SUB).astype(jnp.int32)
    n_real = (p_end[-1:] // bm).astype(jnp.int32)
    return dest, row_tok, blk_expert, blk_valid, sub_valid, n_real


def _pick(n, prefs):
    for p in prefs:
        if n % p == 0:
            return p
    return n


def kernel(x_prompt, x_sample, cache_k, cache_v, state_wkv, state_shift, attn_norm_g, w_in, attn_sinks, mu_shift, decay_w0, decay_lora_up, iclr_a0, iclr_lora_up, gate_lora_up, k_k, k_a, r_k, lnx_w, lnx_b, w_out, ffn_norm_g, router_w, router_b, w_up, b_up, w_down, b_down, final_norm_g):
    depth = w_in.shape[0]
    assert depth == 1
    batch, seq, _ = x_prompt.shape
    dec_b, dec_t, _ = x_sample.shape
    n_p = batch * seq
    n_s = dec_b * dec_t
    l = 0

    qperm = _q_col_perm()
    w_in_l = w_in[l]
    w_attn = jnp.concatenate([w_in_l[:, :ATTN_WIDTH][:, qperm], w_in_l[:, ATTN_WIDTH:ATTN_WIDTH + 2 * KV_WIDTH]],
                             axis=1).astype(BF16)
    w_rwkv = _pad_rwkv_cols(w_in_l[:, ATTN_WIDTH + 2 * KV_WIDTH:]).astype(BF16)
    w_oa = w_out[l][:ATTN_WIDTH][qperm].astype(BF16)
    w_or = w_out[l][ATTN_WIDTH:].astype(BF16)
    g_attn = attn_norm_g[l][None, :]
    g_ffn = ffn_norm_g[l][None, :]
    sinks_true = attn_sinks[l].astype(F32)
    seg = (jnp.arange(GROUP)[:, None] // HEAD_DIM == jnp.arange(GROUP)[None, :] // HEAD_DIM).astype(BF16)
    prm = dict(
        mu=_pad_rwkv_cols(mu_shift[l][None, :]),
        w0=decay_w0[l][None, :], wd=_pad_rows(decay_lora_up[l], LANES).astype(BF16),
        a0=iclr_a0[l][None, :], wa=_pad_rows(iclr_lora_up[l], LANES).astype(BF16),
        wg=_pad_rows(gate_lora_up[l], XG_PAD).astype(BF16),
        k_k=k_k[l][None, :], k_a=k_a[l][None, :], r_k=r_k[l].reshape(1, RWKV_WIDTH),
        lnx_w=lnx_w[l][None, :], lnx_b=lnx_b[l][None, :], seg=seg)
    rw_pad = jnp.pad(router_w[l], ((0, 0), (0, LANES - N_EXPERTS)))
    rw_hi = rw_pad.astype(BF16)
    rw_lo = (rw_pad - rw_hi.astype(F32)).astype(BF16)
    rw_split = jnp.stack([rw_hi, rw_lo])
    rb_pad = jnp.concatenate([router_b[l], jnp.full((LANES - N_EXPERTS,), NEG_BIG, F32)])[None, :]
    b_up3 = b_up[l][:, None, :]
    b_dn = b_down[l][:, None, :]
    tf = 256
    sel = (jnp.arange(2 * tf)[:, None] == 2 * jnp.arange(tf)[None, :]).astype(BF16)

    tm_p = _pick(seq, (512, 256, 128))
    cos_p, sin_p = _rope_tables(jnp.arange(seq, dtype=jnp.int32))
    q_p, k_p, v_p = _attn_proj(x_prompt.reshape(n_p, D_MODEL), g_attn, w_attn, cos_p, sin_p, tm_p)
    tm_s = _pick(n_s, (512, 256, 128, 64, 32, 16, 8))
    pos_s = PAST_LEN + (jnp.arange(n_s, dtype=jnp.int32) % dec_t)
    cos_s, sin_s = _rope_tables(pos_s)
    q_s, k_s, v_s = _attn_proj(x_sample.reshape(n_s, D_MODEL), g_attn, w_attn, cos_s, sin_s, tm_s)
    tn = RWKV_PROJ_PAD // 4
    pr_p = _rwkv_proj(x_prompt.reshape(n_p, D_MODEL), g_attn, w_rwkv, _pick(n_p, (1024, 512, 256, 128)), tn)
    pr_s = _rwkv_proj(x_sample.reshape(n_s, D_MODEL), g_attn, w_rwkv, tm_s, tn)

    oa_p = _attn_prompt(q_p, k_p, v_p, sinks_true, batch, seq)
    wb = cache_k.shape[2]
    sink_rows = jnp.stack([
        jnp.repeat(jnp.stack([sinks_true[8 * pair + 4 * e + j] for e in range(2) for j in range(4)]), dec_t)
        for pair in range(2)])[:, :, None]
    oa_s, nk_s, nv_s = _attn_sample(
        q_s.reshape(dec_b, dec_t, ATTN_WIDTH), k_s.reshape(dec_b, dec_t, KV_WIDTH),
        v_s.reshape(dec_b, dec_t, KV_WIDTH), cache_k[l].reshape(dec_b, wb, KV_WIDTH),
        cache_v[l].reshape(dec_b, wb, KV_WIDTH), sink_rows, _pick(dec_b, (8, 4, 2, 1)))

    zero_state = jnp.zeros((batch, RWKV_WIDTH // HEAD_DIM, HEAD_DIM, HEAD_DIM), F32)
    zero_shift = jnp.zeros((batch, 1, RWKV_PROJ_PAD), F32)
    or_p, st_p, sh_p = _rwkv_mix4(pr_p, zero_state, zero_shift, prm, 1, _pick(seq, (64, 32, 16, 8)),
                                  npar=1)
    seq_per_step = _pick(dec_b, (64 // dec_t, 1)) if 64 % dec_t == 0 else 1
    or_s, st_s, sh_s = _rwkv_mix4(pr_s, state_wkv[l], _pad_rwkv_cols(state_shift[l])[:, None, :], prm,
                                  seq_per_step, dec_t)

    n_all = n_p + n_s
    tm_o = _pick(math.gcd(n_p, n_s), (512, 256, 128, 64, 32, 16, 8))
    h, hn, idx_pad, gate_pad = _out_router(
        x_prompt.reshape(n_p, D_MODEL), x_sample.reshape(n_s, D_MODEL), oa_p, oa_s.reshape(n_s, ATTN_WIDTH),
        or_p, or_s, w_oa, w_or, g_ffn, rw_split, rb_pad, tm_o)

    bm = 1024 if n_all * TOP_K >= 1024 * N_EXPERTS else 2 * MOE_SUB
    dest, row_tok, blk_expert, blk_valid, sub_valid, n_real = _route(idx_pad[:, :TOP_K], bm)
    gather_rows = GATHER_ROWS if bm % GATHER_ROWS == 0 else MOE_SUB
    step_valid = sub_valid.reshape(-1, gather_rows // MOE_SUB).sum(axis=1)
    x_sorted = _gather_rows(step_valid, row_tok, hn, gather_rows)
    out_rows = _experts(blk_expert, blk_valid, n_real, x_sorted, w_up[l], b_up3, w_down[l], b_dn, sel, bm, tf)
    y_p, y_s = _combine(dest, out_rows, h, gate_pad, final_norm_g[None, :], n_p,
                        _pick(math.gcd(n_p, n_s), (256, 128, 64, 32, 16, 8)))

    y_prompt = y_p.reshape(batch, seq, D_MODEL)
    y_sample = y_s.reshape(dec_b, dec_t, D_MODEL)
    kp = k_p.reshape(batch, seq, 4, HEAD_DIM)[:, -WINDOW:][None]
    vp = v_p.reshape(batch, seq, 4, HEAD_DIM)[:, -WINDOW:][None]
    wp = st_p[None]
    sp = _unpad_rwkv_cols(sh_p[:, 0, :])[None]
    ks = nk_s.reshape(dec_b, wb, 4, HEAD_DIM)[None]
    vs = nv_s.reshape(dec_b, wb, 4, HEAD_DIM)[None]
    ws = st_s[None]
    ss = _unpad_rwkv_cols(sh_s[:, 0, :])[None]
    return (y_prompt, y_sample, kp, vp, wp, sp, ks, vs, ws, ss)
```

```python
import functools
import math

import jax
import jax.numpy as jnp
from jax import lax
from jax.experimental import pallas as pl
from jax.experimental.pallas import tpu as pltpu

F32 = jnp.float32
BF16 = jnp.bfloat16

D_MODEL = 2048
HEAD_DIM = 64
LANES = 128
ATTN_WIDTH = 1024
KV_WIDTH = 256
ATTN_HEADS = 16
RWKV_WIDTH = 1024
DECAY_LORA = 64
ICLR_LORA = 64
GATE_LORA = 160
RWKV_PROJ = 3 * RWKV_WIDTH + DECAY_LORA + ICLR_LORA + GATE_LORA
XW_OFF = 3 * RWKV_WIDTH
XA_OFF = XW_OFF + LANES
XG_OFF = XA_OFF + LANES
XG_PAD = 2 * LANES
RWKV_PROJ_PAD = XG_OFF + XG_PAD
WINDOW = 128
ROPE_THETA = 10000.0
PAST_LEN = 8192
N_EXPERTS = 32
TOP_K = 4
D_FF = 2048
SWIGLU_ALPHA = 1.702
SWIGLU_LIMIT = 7.0
NORM_EPS = 1e-5
LNX_EPS = 64e-5
NEG_BIG = -1e30
VMEM_LIMIT = 56 * 1024 * 1024


def _cparams(sem):
    return pltpu.CompilerParams(dimension_semantics=sem, vmem_limit_bytes=VMEM_LIMIT)


_NN = (((1,), (0,)), ((), ()))
_NT = (((1,), (1,)), ((), ()))
_TN = (((0,), (0,)), ((), ()))


def _mm(a, b, dims=_NN):
    return lax.dot_general(a, b, dims, preferred_element_type=F32)


def _split2(x):
    hi = x.astype(BF16)
    lo = (x - hi.astype(F32)).astype(BF16)
    return hi, lo


def _rms(x, g):
    ms = jnp.mean(x * x, axis=-1, keepdims=True)
    return (x * lax.rsqrt(ms + NORM_EPS)) * g


def _attn_proj_kernel(x_ref, g_ref, w_ref, cos_ref, sin_ref, q_ref, k_ref, v_ref):
    xn = _rms(x_ref[...], g_ref[...]).astype(BF16)
    a = _mm(xn, w_ref[...])
    cos = cos_ref[...]
    sin = sin_ref[...]
    lane = lax.broadcasted_iota(jnp.int32, cos.shape, 1)
    first_half = (lane % HEAD_DIM) < (HEAD_DIM // 2)
    n_rot = (ATTN_WIDTH + KV_WIDTH) // LANES
    for j in range(n_rot):
        t = a[:, j * LANES:(j + 1) * LANES]
        swapped = jnp.where(first_half, pltpu.roll(t, LANES - HEAD_DIM // 2, 1), pltpu.roll(t, HEAD_DIM // 2, 1))
        rot = t * cos + swapped * sin
        if j < ATTN_WIDTH // LANES:
            q_ref[:, j * LANES:(j + 1) * LANES] = (rot * (HEAD_DIM ** -0.5)).astype(BF16)
        else:
            jj = j - ATTN_WIDTH // LANES
            k_ref[:, jj * LANES:(jj + 1) * LANES] = rot
    v_ref[...] = a[:, ATTN_WIDTH + KV_WIDTH:]


def _attn_proj(x, g, w_attn, cos_t, sin_t, tm):
    t_tok = x.shape[0]
    n_pos_blocks = cos_t.shape[0] // tm
    return pl.pallas_call(
        _attn_proj_kernel,
        grid=(t_tok // tm,),
        in_specs=[
            pl.BlockSpec((tm, D_MODEL), lambda i: (i, 0)),
            pl.BlockSpec((1, D_MODEL), lambda i: (0, 0)),
            pl.BlockSpec((D_MODEL, ATTN_WIDTH + 2 * KV_WIDTH), lambda i: (0, 0)),
            pl.BlockSpec((tm, LANES), lambda i: (i % n_pos_blocks, 0)),
            pl.BlockSpec((tm, LANES), lambda i: (i % n_pos_blocks, 0)),
        ],
        out_specs=[
            pl.BlockSpec((tm, ATTN_WIDTH), lambda i: (i, 0)),
            pl.BlockSpec((tm, KV_WIDTH), lambda i: (i, 0)),
            pl.BlockSpec((tm, KV_WIDTH), lambda i: (i, 0)),
        ],
        out_shape=[
            jax.ShapeDtypeStruct((t_tok, ATTN_WIDTH), BF16),
            jax.ShapeDtypeStruct((t_tok, KV_WIDTH), F32),
            jax.ShapeDtypeStruct((t_tok, KV_WIDTH), F32),
        ],
        compiler_params=_cparams(("parallel",)),
        name="attn_proj",
    )(x, g, w_attn, cos_t, sin_t)


def _rwkv_proj_kernel(x_ref, g_ref, w_ref, p_ref, xn_ref):
    @pl.when(pl.program_id(1) == 0)
    def _():
        xn_ref[...] = _rms(x_ref[...], g_ref[...]).astype(BF16)

    p_ref[...] = _mm(xn_ref[...], w_ref[...])


def _rwkv_proj(x, g, w_rwkv, tm, tn):
    t_tok = x.shape[0]
    return pl.pallas_call(
        _rwkv_proj_kernel,
        grid=(t_tok // tm, RWKV_PROJ_PAD // tn),
        in_specs=[
            pl.BlockSpec((tm, D_MODEL), lambda i, n: (i, 0)),
            pl.BlockSpec((1, D_MODEL), lambda i, n: (0, 0)),
            pl.BlockSpec((D_MODEL, tn), lambda i, n: (0, n)),
        ],
        out_specs=pl.BlockSpec((tm, tn), lambda i, n: (i, n)),
        out_shape=jax.ShapeDtypeStruct((t_tok, RWKV_PROJ_PAD), F32),
        scratch_shapes=[pltpu.VMEM((tm, D_MODEL), BF16)],
        compiler_params=_cparams(("parallel", "arbitrary")),
        name="rwkv_proj",
    )(x, g, w_rwkv)


def _softmax_pv(s, mask, sink, vt):
    s = jnp.where(mask, s, NEG_BIG)
    m = jnp.maximum(jnp.max(s, axis=-1, keepdims=True), sink)
    p = jnp.exp(s - m)
    denom = jnp.sum(p, axis=-1, keepdims=True) + jnp.exp(sink - m)
    return _mm(p.astype(BF16), vt) * (1.0 / denom)


def _attn_prompt_kernel(q_ref, kp_ref, kc_ref, vp_ref, vc_ref, sink_ref, o_ref):
    i = pl.program_id(1)
    blk = q_ref.shape[0]
    kk = jnp.concatenate([kp_ref[...], kc_ref[...]], axis=0).astype(BF16)
    vv = jnp.concatenate([vp_ref[...], vc_ref[...]], axis=0).astype(BF16)
    a = lax.broadcasted_iota(jnp.int32, (blk, 2 * blk), 0)
    c = lax.broadcasted_iota(jnp.int32, (blk, 2 * blk), 1)
    mask = (c > a) & (c <= a + blk) & ((c >= blk) | (i > 0))
    lane_lo = lax.broadcasted_iota(jnp.int32, (blk, LANES), 1) < HEAD_DIM
    n_pairs = KV_WIDTH // LANES
    kts = [kk[:, pair * LANES:(pair + 1) * LANES] for pair in range(n_pairs)]
    vts = [vv[:, pair * LANES:(pair + 1) * LANES] for pair in range(n_pairs)]
    heads = [(pair, j, e) for pair in range(n_pairs) for j in range(4) for e in range(2)]
    scores = []
    for pair, j, e in heads:
        tile = pair * 4 + j
        qt = q_ref[:, tile * LANES:(tile + 1) * LANES]
        qm = jnp.where(lane_lo if e == 0 else jnp.logical_not(lane_lo), qt, jnp.zeros_like(qt))
        scores.append(_mm(qm, kts[pair], _NT))
    probs, scales = [], []
    for s, (pair, j, e) in zip(scores, heads):
        sink = sink_ref[8 * pair + 4 * e + j]
        s = jnp.where(mask, s, NEG_BIG)
        m = jnp.maximum(jnp.max(s, axis=-1, keepdims=True), sink)
        p = jnp.exp(s - m)
        probs.append(p.astype(BF16))
        scales.append(1.0 / (jnp.sum(p, axis=-1, keepdims=True) + jnp.exp(sink - m)))
    outs = [_mm(p, vts[pair]) * sc for p, sc, (pair, j, e) in zip(probs, scales, heads)]
    for tile in range(ATTN_WIDTH // LANES):
        o_ref[:, tile * LANES:(tile + 1) * LANES] = jnp.where(lane_lo, outs[2 * tile], outs[2 * tile + 1]).astype(BF16)


def _attn_prompt(q, k, v, sinks, batch, seq):
    blk = WINDOW
    nb = seq // blk
    cur = lambda b, i: (b * nb + i, 0)
    prev = lambda b, i: (b * nb + jnp.maximum(i - 1, 0), 0)
    return pl.pallas_call(
        _attn_prompt_kernel,
        grid=(batch, nb),
        in_specs=[
            pl.BlockSpec((blk, ATTN_WIDTH), cur),
            pl.BlockSpec((blk, KV_WIDTH), prev),
            pl.BlockSpec((blk, KV_WIDTH), cur),
            pl.BlockSpec((blk, KV_WIDTH), prev),
            pl.BlockSpec((blk, KV_WIDTH), cur),
            pl.BlockSpec(memory_space=pltpu.SMEM),
        ],
        out_specs=pl.BlockSpec((blk, ATTN_WIDTH), cur),
        out_shape=jax.ShapeDtypeStruct((batch * seq, ATTN_WIDTH), BF16),
        compiler_params=_cparams(("parallel", "arbitrary")),
        name="attn_prompt",
    )(q, k, k, v, v, sinks)


def _attn_sample_kernel(q_ref, kn_ref, vn_ref, ck_ref, cv_ref, sink_ref, o_ref, nk_ref, nv_ref):
    bb, t_len, _ = q_ref.shape
    wb = ck_ref.shape[1]
    n_keys = wb + t_len
    rows = 8 * t_len
    a = lax.broadcasted_iota(jnp.int32, (rows, n_keys), 0) % t_len
    c = lax.broadcasted_iota(jnp.int32, (rows, n_keys), 1)
    mask = ((c < wb) & (c > a + (wb - WINDOW))) | ((c >= wb) & (c - wb <= a))
    lane_lo = lax.broadcasted_iota(jnp.int32, (t_len, LANES), 1) < HEAD_DIM

    def body(b, carry):
        ck = ck_ref[b]
        cv = cv_ref[b]
        kn = kn_ref[b]
        vn = vn_ref[b]
        nk_ref[b, 0:wb - t_len, :] = ck[t_len:, :]
        nk_ref[b, wb - t_len:wb, :] = kn
        nv_ref[b, 0:wb - t_len, :] = cv[t_len:, :]
        nv_ref[b, wb - t_len:wb, :] = vn
        k_all = jnp.concatenate([ck, kn], axis=0).astype(BF16)
        v_all = jnp.concatenate([cv, vn], axis=0).astype(BF16)
        qb = q_ref[b]
        pairs = range(KV_WIDTH // LANES)
        scores = []
        for pair in pairs:
            stack = []
            for e in range(2):
                for j in range(4):
                    tile = pair * 4 + j
                    qt = qb[:, tile * LANES:(tile + 1) * LANES]
                    stack.append(jnp.where(lane_lo if e == 0 else jnp.logical_not(lane_lo), qt, jnp.zeros_like(qt)))
            qs = jnp.concatenate(stack, axis=0)
            scores.append(_mm(qs, k_all[:, pair * LANES:(pair + 1) * LANES], _NT))
        outs = [_softmax_pv(scores[pair], mask, sink_ref[pair], v_all[:, pair * LANES:(pair + 1) * LANES])
                for pair in pairs]
        for pair in pairs:
            o = outs[pair]
            for j in range(4):
                tile = pair * 4 + j
                lo = o[j * t_len:(j + 1) * t_len, :]
                hi = o[(4 + j) * t_len:(5 + j) * t_len, :]
                o_ref[b, :, tile * LANES:(tile + 1) * LANES] = jnp.where(lane_lo, lo, hi).astype(BF16)
        return carry

    lax.fori_loop(0, bb, body, 0)


def _attn_sample(q3, k3, v3, ck, cv, sink_rows, bb):
    dec_b, t_len, _ = q3.shape
    wb = ck.shape[1]
    blk3 = lambda w: pl.BlockSpec((bb, t_len, w), lambda i: (i, 0, 0))
    cache = pl.BlockSpec((bb, wb, KV_WIDTH), lambda i: (i, 0, 0))
    return pl.pallas_call(
        _attn_sample_kernel,
        grid=(dec_b // bb,),
        in_specs=[blk3(ATTN_WIDTH), blk3(KV_WIDTH), blk3(KV_WIDTH), cache, cache,
                  pl.BlockSpec((2, 8 * t_len, 1), lambda i: (0, 0, 0))],
        out_specs=[blk3(ATTN_WIDTH), cache, cache],
        out_shape=[
            jax.ShapeDtypeStruct((dec_b, t_len, ATTN_WIDTH), BF16),
            jax.ShapeDtypeStruct(ck.shape, F32),
            jax.ShapeDtypeStruct(cv.shape, F32),
        ],
        compiler_params=_cparams(("parallel",)),
        name="attn_sample",
    )(q3, k3, v3, ck, cv, sink_rows)


GROUP = 4 * HEAD_DIM


def _bf(x):
    return x.astype(BF16)


def _seg_sum4(x, seg_mat):
    tiles = [_split2(x[:, t * GROUP:(t + 1) * GROUP]) for t in range(x.shape[1] // GROUP)]
    return jnp.concatenate([_mm(h, seg_mat) + _mm(l, seg_mat) for h, l in tiles], axis=1)


def _stack_heads(x, head_masks):
    zero = jnp.zeros_like(x)
    return jnp.concatenate([jnp.where(m, x, zero) for m in head_masks], axis=0)


def _sum_blocks(x, r):
    return (x[0:r] + x[r:2 * r]) + (x[2 * r:3 * r] + x[3 * r:4 * r])


def _place_heads(blocks):
    rows = []
    for h, blk in enumerate(blocks):
        rows.append(jnp.concatenate([blk if j == h else jnp.zeros_like(blk) for j in range(len(blocks))], axis=1))
    return jnp.concatenate(rows, axis=0)


def _rwkv4_kernel(p_ref, s_in_ref, sh_ref, mu_ref, w0_ref, wd_ref, a0_ref, wa_ref, wg_ref, kk_ref, ka_ref,
                  rk_ref, lnw_ref, lnb_ref, seg_ref, o_ref, s_out_ref, sh_out_ref, state_scr, prev_scr,
                  *, nseq, tlen, npar):
    rows = nseq * tlen
    heads = GROUP // HEAD_DIM
    log_t = tlen.bit_length() - 1
    log_r = rows.bit_length() - 1
    log_h = HEAD_DIM.bit_length() - 1

    ri = lax.broadcasted_iota(jnp.int32, (rows, rows), 0)
    cj = lax.broadcasted_iota(jnp.int32, (rows, rows), 1)
    bri = lax.broadcasted_iota(jnp.int32, (heads * rows, heads * rows), 0)
    bcj = lax.broadcasted_iota(jnp.int32, (heads * rows, heads * rows), 1)
    same = ((bri >> log_r) == (bcj >> log_r)) & ((bri >> log_t) == (bcj >> log_t))
    lane = lax.broadcasted_iota(jnp.int32, (rows, GROUP), 1)
    sr = lax.broadcasted_iota(jnp.int32, (GROUP, GROUP), 0)
    sc = lax.broadcasted_iota(jnp.int32, (GROUP, GROUP), 1)
    masks = dict(
        tri=jnp.where(((ri >> log_t) == (cj >> log_t)) & (ri >= cj), 1.0, 0.0).astype(BF16),
        strict_bd=same & (bri > bcj),
        incl_bd=same & (bri >= bcj),
        eye_bd=jnp.where(bri == bcj, 1.0, 0.0).astype(F32),
        head_masks=[(lane >> log_h) == h for h in range(heads)],
        state_bd=(sr >> log_h) == (sc >> log_h),
    )
    refs = (p_ref, s_in_ref, sh_ref, mu_ref, w0_ref, wd_ref, a0_ref, wa_ref, wg_ref, kk_ref, ka_ref, rk_ref,
            lnw_ref, lnb_ref, seg_ref, o_ref, s_out_ref, sh_out_ref, state_scr, prev_scr)
    for j in range(npar):
        _rwkv4_block(j, refs, masks, nseq=nseq, tlen=tlen)


def _rwkv4_block(j, refs, masks, *, nseq, tlen):
    (p_ref, s_in_ref, sh_ref, mu_ref, w0_ref, wd_ref, a0_ref, wa_ref, wg_ref, kk_ref, ka_ref, rk_ref,
     lnw_ref, lnb_ref, seg_ref, o_ref, s_out_ref, sh_out_ref, state_scr, prev_scr) = refs
    ci = pl.program_id(1)
    rows = nseq * tlen
    n_groups = RWKV_WIDTH // GROUP
    heads = GROUP // HEAD_DIM
    log_t = tlen.bit_length() - 1
    carry = nseq == 1

    p = p_ref[j] if carry else p_ref[...]
    rowi = lax.broadcasted_iota(jnp.int32, (rows, 1), 0)
    rolled = pltpu.roll(p, 1, 0)
    if carry:
        @pl.when(ci == 0)
        def _():
            prev_scr[j] = sh_ref[j]
            for g in range(n_groups):
                state_scr[j * n_groups + g] = _place_heads([s_in_ref[j, heads * g + h] for h in range(heads)])

        p_prev = jnp.where(rowi == 0, prev_scr[j], rolled)
        prev_scr[j] = p[rows - 1:rows, :]
    else:
        p_prev = rolled
        for s in range(nseq):
            p_prev = jnp.where(rowi == s * tlen, sh_ref[s], p_prev)
            sh_out_ref[s] = p[(s + 1) * tlen - 1:(s + 1) * tlen, :]

    xs = p + (p_prev - p) * mu_ref[...]
    r = xs[:, 0:RWKV_WIDTH]
    k = xs[:, RWKV_WIDTH:2 * RWKV_WIDTH]
    v = xs[:, 2 * RWKV_WIDTH:3 * RWKV_WIDTH]
    xw = xs[:, XW_OFF:XW_OFF + LANES]
    xa = xs[:, XA_OFF:XA_OFF + LANES]
    xg = xs[:, XG_OFF:XG_OFF + XG_PAD]
    z = w0_ref[...] + _mm(_bf(jnp.tanh(xw)), wd_ref[...])
    w_log = -jax.nn.softplus(-z) - 0.5
    lw = -jnp.exp(w_log)
    a = jax.nn.sigmoid(a0_ref[...] + _mm(_bf(xa), wa_ref[...]))
    gate = _mm(_bf(jax.nn.sigmoid(xg)), wg_ref[...])
    seg = seg_ref[...]
    kk = k * kk_ref[...]
    kap = kk / jnp.maximum(jnp.sqrt(_seg_sum4(kk * kk, seg)), 1e-12)
    k2 = k * (1.0 + (a - 1.0) * ka_ref[...])
    b = kap * a
    bonus = _seg_sum4(r * k2 * rk_ref[...], seg) * v

    lw_hi, lw_lo = _split2(lw)
    cum = _mm(masks["tri"], lw_hi) + _mm(masks["tri"], lw_lo)
    eg = jnp.exp(cum)
    kap_t = kap * jnp.exp(cum - lw)
    r_t = r * eg
    einv = jnp.exp(-cum)
    b_t = b * einv
    k_t = k2 * einv

    strict_bd, incl_bd, eye_bd = masks["strict_bd"], masks["incl_bd"], masks["eye_bd"]
    head_masks, state_bd = masks["head_masks"], masks["state_bd"]
    zero_bd = jnp.zeros((heads * rows, heads * rows), F32)
    zero_st = jnp.zeros((GROUP, GROUP), F32)

    groups = range(n_groups)
    hr = heads * rows
    sls = [slice(g * GROUP, (g + 1) * GROUP) for g in groups]
    kap_g = [kap_t[:, sl] for sl in sls]
    r_g = [r_t[:, sl] for sl in sls]
    b_g = [b_t[:, sl] for sl in sls]
    k_g = [k_t[:, sl] for sl in sls]
    v_g = [v[:, sl] for sl in sls]
    lkr = [jnp.concatenate([_stack_heads(_bf(kap_g[g]), head_masks), _stack_heads(_bf(r_g[g]), head_masks)], axis=0)
           for g in groups]
    gb = [_mm(lkr[g], _stack_heads(_bf(b_g[g]), head_masks), _NT) for g in groups]
    gk = [_mm(lkr[g], _stack_heads(_bf(k_g[g]), head_masks), _NT) for g in groups]
    n = [-jnp.where(strict_bd, gb[g][:hr], zero_bd) for g in groups]
    t_inv = [eye_bd + n[g] for g in groups]
    for _ in range(max(log_t - 1, 0)):
        nb = [_bf(n[g]) for g in groups]
        n = [_mm(nb[g], nb[g]) for g in groups]
        t_inv = [t_inv[g] + _mm(_bf(t_inv[g]), _bf(n[g])) for g in groups]
    a_k = [_bf(jnp.where(strict_bd, gk[g][:hr], zero_bd)) for g in groups]
    r_b = [_bf(jnp.where(incl_bd, gb[g][hr:], zero_bd)) for g in groups]
    r_k = [_bf(jnp.where(incl_bd, gk[g][hr:], zero_bd)) for g in groups]
    v_stack = [_stack_heads(_bf(v_g[g]), head_masks) for g in groups]

    if carry:
        s0 = [state_scr[j * n_groups + g] for g in groups]
        sk = [_mm(_bf(jnp.concatenate([kap_g[g], r_g[g]], axis=0)), _bf(s0[g]), _NT) for g in groups]
        k_s = [sk[g][:rows] for g in groups]
        r_s = [sk[g][rows:] for g in groups]
    else:
        s0, k_s, r_s = [], [], []
        for g in groups:
            s0_list, ks_list, rs_list = [], [], []
            for s in range(nseq):
                rs = slice(s * tlen, (s + 1) * tlen)
                st = _place_heads([s_in_ref[s, heads * g + h] for h in range(heads)])
                sk = _mm(_bf(jnp.concatenate([kap_g[g][rs], r_g[g][rs]], axis=0)), _bf(st), _NT)
                s0_list.append(st)
                ks_list.append(sk[:tlen])
                rs_list.append(sk[tlen:])
            s0.append(s0_list)
            k_s.append(jnp.concatenate(ks_list, axis=0))
            r_s.append(jnp.concatenate(rs_list, axis=0))

    w = [_sum_blocks(_mm(a_k[g], v_stack[g]), rows) for g in groups]
    rhs = [-(k_s[g] + w[g]) for g in groups]
    u_stack = [_mm(_bf(t_inv[g]), _stack_heads(_bf(rhs[g]), head_masks)) for g in groups]
    u = [_sum_blocks(u_stack[g], rows) for g in groups]
    y_stack = [_mm(r_b[g], _bf(u_stack[g])) + _mm(r_k[g], v_stack[g]) for g in groups]
    ys = [r_s[g] + _sum_blocks(y_stack[g], rows) for g in groups]

    for g in groups:
        if carry:
            d = _mm(_bf(jnp.concatenate([u[g], v_g[g]], axis=0)), _bf(jnp.concatenate([b_g[g], k_g[g]], axis=0)), _TN)
            state_scr[j * n_groups + g] = (s0[g] + jnp.where(state_bd, d, zero_st)) * eg[rows - 1:rows, sls[g]]
        else:
            for s in range(nseq):
                rs = slice(s * tlen, (s + 1) * tlen)
                d = _mm(_bf(jnp.concatenate([u[g][rs], v_g[g][rs]], axis=0)),
                        _bf(jnp.concatenate([b_g[g][rs], k_g[g][rs]], axis=0)), _TN)
                s_new = (s0[g][s] + jnp.where(state_bd, d, zero_st)) * eg[(s + 1) * tlen - 1:(s + 1) * tlen, sls[g]]
                for h in range(heads):
                    hs = slice(h * HEAD_DIM, (h + 1) * HEAD_DIM)
                    s_out_ref[s, heads * g + h] = s_new[hs, hs]

    y = jnp.concatenate(ys, axis=1)
    inv_n = 1.0 / HEAD_DIM
    mean = _seg_sum4(y, seg) * inv_n
    yc = y - mean
    var = _seg_sum4(yc * yc, seg) * inv_n
    yn = yc * lax.rsqrt(var + LNX_EPS) * lnw_ref[...] + lnb_ref[...]
    out = ((yn + bonus) * gate).astype(o_ref.dtype)
    if carry:
        o_ref[j] = out

        @pl.when(ci == pl.num_programs(1) - 1)
        def _():
            sh_out_ref[j] = prev_scr[j]
            for g in range(n_groups):
                st = state_scr[j * n_groups + g]
                for h in range(heads):
                    hs = slice(h * HEAD_DIM, (h + 1) * HEAD_DIM)
                    s_out_ref[j, heads * g + h] = st[hs, hs]
    else:
        o_ref[...] = out


def _rwkv_mix4(p2, s_in, shift3, prm, nseq, tlen, npar=1):
    batch = s_in.shape[0]
    t_len = p2.shape[0] // batch
    n_chunks = t_len // tlen
    rows = nseq * tlen
    n_heads = RWKV_WIDTH // HEAD_DIM
    per_step = nseq * npar
    const2 = lambda shape: pl.BlockSpec(shape, lambda b, c: (0, 0))
    vec = const2((1, RWKV_WIDTH))
    state_spec = pl.BlockSpec((per_step, n_heads, HEAD_DIM, HEAD_DIM), lambda b, c: (b, 0, 0, 0))
    shift_spec = pl.BlockSpec((per_step, 1, RWKV_PROJ_PAD), lambda b, c: (b, 0, 0))
    if nseq == 1:
        p_in = p2.reshape(batch, t_len, RWKV_PROJ_PAD)
        p_spec = pl.BlockSpec((npar, tlen, RWKV_PROJ_PAD), lambda b, c: (b, c, 0))
        o_spec = pl.BlockSpec((npar, tlen, RWKV_WIDTH), lambda b, c: (b, c, 0))
        o_shape = jax.ShapeDtypeStruct((batch, t_len, RWKV_WIDTH), BF16)
    else:
        p_in = p2
        p_spec = pl.BlockSpec((rows, RWKV_PROJ_PAD), lambda b, c: (b * n_chunks + c, 0))
        o_spec = pl.BlockSpec((rows, RWKV_WIDTH), lambda b, c: (b * n_chunks + c, 0))
        o_shape = jax.ShapeDtypeStruct((batch * t_len, RWKV_WIDTH), BF16)
    o, s_out, sh_out = pl.pallas_call(
        functools.partial(_rwkv4_kernel, nseq=nseq, tlen=tlen, npar=npar),
        grid=(batch // per_step, n_chunks),
        in_specs=[
            p_spec,
            state_spec,
            shift_spec,
            const2((1, RWKV_PROJ_PAD)),
            vec,
            const2((LANES, RWKV_WIDTH)),
            vec,
            const2((LANES, RWKV_WIDTH)),
            const2((XG_PAD, RWKV_WIDTH)),
            vec, vec, vec, vec, vec,
            const2((GROUP, GROUP)),
        ],
        out_specs=[o_spec, state_spec, shift_spec],
        out_shape=[
            o_shape,
            jax.ShapeDtypeStruct(s_in.shape, F32),
            jax.ShapeDtypeStruct(shift3.shape, F32),
        ],
        scratch_shapes=[pltpu.VMEM((npar * (RWKV_WIDTH // GROUP), GROUP, GROUP), F32),
                        pltpu.VMEM((npar, 1, RWKV_PROJ_PAD), F32)],
        compiler_params=_cparams(("parallel", "arbitrary")),
        name="rwkv_mix_t%d" % tlen,
    )(p_in, s_in, shift3, prm["mu"], prm["w0"], prm["wd"], prm["a0"], prm["wa"], prm["wg"], prm["k_k"],
      prm["k_a"], prm["r_k"], prm["lnx_w"], prm["lnx_b"], prm["seg"])
    return o.reshape(batch * t_len, RWKV_WIDTH), s_out, sh_out


def _out_router_kernel(xp_ref, xs_ref, oap_ref, oas_ref, orp_ref, ors_ref, wa_ref, wr_ref, g_ref, rw_ref, rb_ref,
                       h_ref, hn_ref, idx_ref, gate_ref, *, n_p_tiles):
    body = functools.partial(_out_router_body, wa_ref=wa_ref, wr_ref=wr_ref, g_ref=g_ref, rw_ref=rw_ref,
                             rb_ref=rb_ref, h_ref=h_ref, hn_ref=hn_ref, idx_ref=idx_ref, gate_ref=gate_ref)
    is_prompt = pl.program_id(0) < n_p_tiles
    pl.when(is_prompt)(functools.partial(body, xp_ref, oap_ref, orp_ref))
    pl.when(jnp.logical_not(is_prompt))(functools.partial(body, xs_ref, oas_ref, ors_ref))


def _out_router_body(x_ref, oa_ref, or_ref, *, wa_ref, wr_ref, g_ref, rw_ref, rb_ref, h_ref, hn_ref, idx_ref,
                     gate_ref):
    h = x_ref[...] + _mm(oa_ref[...], wa_ref[...]) + _mm(or_ref[...], wr_ref[...])
    h_ref[...] = h
    hn = _rms(h, g_ref[...])
    hh, hl = _split2(hn)
    bits = lax.bitcast_convert_type(hh.astype(F32), jnp.int32)
    half = D_MODEL // 2
    hn_ref[...] = ((bits[:, :half] >> 16) & jnp.int32(0xFFFF)) | (bits[:, half:] & jnp.int32(-65536))
    logits = _mm(hh, rw_ref[0]) + (_mm(hh, rw_ref[1]) + _mm(hl, rw_ref[0])) + rb_ref[...]
    lane = lax.broadcasted_iota(jnp.int32, logits.shape, 1)
    vals = []
    idxs = []
    cur = logits
    for _ in range(TOP_K):
        m = jnp.max(cur, axis=-1, keepdims=True)
        sel = jnp.min(jnp.where(cur == m, lane, LANES), axis=-1, keepdims=True)
        vals.append(m)
        idxs.append(sel)
        cur = jnp.where(lane == sel, -jnp.inf, cur)
    es = [jnp.exp(vj - vals[0]) for vj in vals]
    tot = es[0] + es[1] + es[2] + es[3]
    idx_out = jnp.zeros(logits.shape, jnp.int32)
    gate_out = jnp.zeros(logits.shape, F32)
    for j in range(TOP_K):
        idx_out = jnp.where(lane == j, idxs[j], idx_out)
        gate_out = jnp.where(lane == j, es[j] / tot, gate_out)
    idx_ref[...] = idx_out
    gate_ref[...] = gate_out


def _out_router(x_p, x_s, oa_p, oa_s, or_p, or_s, w_oa, w_or, g, rw_split, rb_pad, tm):
    n_p_tiles = x_p.shape[0] // tm
    t_tok = x_p.shape[0] + x_s.shape[0]
    row = lambda w: pl.BlockSpec((tm, w), lambda i: (i, 0))
    row_p = lambda w: pl.BlockSpec((tm, w), lambda i: (jnp.minimum(i, n_p_tiles - 1), 0))
    row_s = lambda w: pl.BlockSpec((tm, w), lambda i: (jnp.maximum(i - n_p_tiles, 0), 0))
    full = lambda shape: pl.BlockSpec(shape, lambda i: tuple(0 for _ in shape))
    return pl.pallas_call(
        functools.partial(_out_router_kernel, n_p_tiles=n_p_tiles),
        grid=(t_tok // tm,),
        in_specs=[row_p(D_MODEL), row_s(D_MODEL), row_p(ATTN_WIDTH), row_s(ATTN_WIDTH), row_p(RWKV_WIDTH),
                  row_s(RWKV_WIDTH), full((ATTN_WIDTH, D_MODEL)), full((RWKV_WIDTH, D_MODEL)), full((1, D_MODEL)),
                  full((2, D_MODEL, LANES)), full((1, LANES))],
        out_specs=[row(D_MODEL), row(D_MODEL // 2), row(LANES), row(LANES)],
        out_shape=[
            jax.ShapeDtypeStruct((t_tok, D_MODEL), F32),
            jax.ShapeDtypeStruct((t_tok, D_MODEL // 2), jnp.int32),
            jax.ShapeDtypeStruct((t_tok, LANES), jnp.int32),
            jax.ShapeDtypeStruct((t_tok, LANES), F32),
        ],
        compiler_params=_cparams(("arbitrary",)),
        name="out_router",
    )(x_p, x_s, oa_p, oa_s, or_p, or_s, w_oa, w_or, g, rw_split, rb_pad)


GATHER_UNROLL = 8
MOE_SUB = 256
GATHER_ROWS = 256


def _gather_kernel(nv_ref, tok_ref, tok_next_ref, hn_ref, o_ref, buf, sem):
    rows = buf.shape[1]
    i = pl.program_id(0)
    last = pl.num_programs(0) - 1
    slot = i % 2

    def issue(tok, dst_slot):
        def body(r8, carry):
            for u in range(GATHER_UNROLL):
                r = r8 * GATHER_UNROLL + u
                pltpu.make_async_copy(hn_ref.at[pl.ds(tok[r], 1)], buf.at[dst_slot, pl.ds(r, 1)],
                                      sem.at[dst_slot]).start()
            return carry

        lax.fori_loop(0, rows // GATHER_UNROLL, body, 0)

    @pl.when((i == 0) & (nv_ref[0] > 0))
    def _():
        issue(tok_ref, 0)

    @pl.when((i < last) & (nv_ref[jnp.minimum(i + 1, last)] > 0))
    def _():
        issue(tok_next_ref, 1 - slot)

    @pl.when(nv_ref[i] > 0)
    def _():
        pltpu.make_async_copy(hn_ref.at[pl.ds(0, rows)], buf.at[slot], sem.at[slot]).wait()
        words = buf[slot]
        half = words.shape[1]
        o_ref[:, :half] = lax.bitcast_convert_type(words << 16, F32).astype(o_ref.dtype)
        o_ref[:, half:] = lax.bitcast_convert_type(words & jnp.int32(-65536), F32).astype(o_ref.dtype)

    @pl.when(nv_ref[i] == 0)
    def _():
        o_ref[...] = jnp.zeros_like(o_ref)


def _gather_rows(sub_valid, row_tok, hn, rows_per_step):
    n_rows = row_tok.shape[0]
    n_steps = n_rows // rows_per_step
    grid_spec = pltpu.PrefetchScalarGridSpec(
        num_scalar_prefetch=1,
        grid=(n_steps,),
        in_specs=[
            pl.BlockSpec((rows_per_step,), lambda i, nv: (i,), memory_space=pltpu.SMEM),
            pl.BlockSpec((rows_per_step,), lambda i, nv: (jnp.minimum(i + 1, n_steps - 1),),
                         memory_space=pltpu.SMEM),
            pl.BlockSpec(memory_space=pl.ANY),
        ],
        out_specs=pl.BlockSpec((rows_per_step, D_MODEL), lambda i, nv: (i, 0)),
        scratch_shapes=[pltpu.VMEM((2, rows_per_step, D_MODEL // 2), jnp.int32), pltpu.SemaphoreType.DMA((2,))],
    )
    return pl.pallas_call(
        _gather_kernel,
        grid_spec=grid_spec,
        out_shape=jax.ShapeDtypeStruct((n_rows, D_MODEL), BF16),
        compiler_params=_cparams(("arbitrary",)),
        name="moe_gather",
    )(sub_valid, row_tok, row_tok, hn)


def _expert_kernel(be_ref, nv_ref, nr_ref, x_ref, wu_ref, bu_ref, wd_ref, bd_ref, sel_ref, o_ref):
    i = pl.program_id(0)
    f = pl.program_id(1)
    n_sub = x_ref.shape[0] // MOE_SUB
    used_sub = (nv_ref[i] + (MOE_SUB - 1)) // MOE_SUB

    def body(m_rows):
        rows = slice(0, m_rows)
        z = _mm(x_ref[rows, :], wu_ref[0].astype(BF16)) + bu_ref[0]
        zn = pltpu.roll(z, z.shape[1] - 1, 1)
        glu = jnp.minimum(z, SWIGLU_LIMIT)
        lin = jnp.clip(zn, -SWIGLU_LIMIT, SWIGLU_LIMIT)
        act = (glu * jax.nn.sigmoid(SWIGLU_ALPHA * glu) * (lin + 1.0)).astype(BF16)
        actc = _mm(act, sel_ref[...]).astype(BF16)
        wd = wd_ref[0].astype(BF16)

        @pl.when(f == 0)
        def _():
            o_ref[rows, :] = _mm(actc, wd) + bd_ref[0]
            if m_rows < o_ref.shape[0]:
                o_ref[m_rows:, :] = jnp.zeros((o_ref.shape[0] - m_rows, o_ref.shape[1]), o_ref.dtype)

        @pl.when(f > 0)
        def _():
            o_ref[rows, :] += _mm(actc, wd)

    for k in range(1, n_sub + 1):
        pl.when(used_sub == k)(functools.partial(body, k * MOE_SUB))

    @pl.when((used_sub == 0) & (f == 0))
    def _():
        o_ref[...] = jnp.zeros_like(o_ref)


def _experts(blk_expert, blk_valid, n_real, x_sorted, w_up, b_up3, w_down, b_down3, sel, bm, tf):
    n_rows = x_sorted.shape[0]
    n_f = D_FF // tf

    def real(i, nr):
        return jnp.minimum(i, nr[0] - 1)

    def f_eff(i, f, nr):
        return jnp.where(i < nr[0], f, n_f - 1)

    grid_spec = pltpu.PrefetchScalarGridSpec(
        num_scalar_prefetch=3,
        grid=(n_rows // bm, n_f),
        in_specs=[
            pl.BlockSpec((bm, D_MODEL), lambda i, f, be, nv, nr: (real(i, nr), 0)),
            pl.BlockSpec((1, D_MODEL, 2 * tf), lambda i, f, be, nv, nr: (be[i], 0, f_eff(i, f, nr))),
            pl.BlockSpec((1, 1, 2 * tf), lambda i, f, be, nv, nr: (be[i], 0, f_eff(i, f, nr))),
            pl.BlockSpec((1, tf, D_MODEL), lambda i, f, be, nv, nr: (be[i], f_eff(i, f, nr), 0)),
            pl.BlockSpec((1, 1, D_MODEL), lambda i, f, be, nv, nr: (be[i], 0, 0)),
            pl.BlockSpec((2 * tf, tf), lambda i, f, be, nv, nr: (0, 0)),
        ],
        out_specs=pl.BlockSpec((bm, D_MODEL), lambda i, f, be, nv, nr: (i, 0)),
    )
    return pl.pallas_call(
        _expert_kernel,
        grid_spec=grid_spec,
        out_shape=jax.ShapeDtypeStruct((n_rows, D_MODEL), F32),
        compiler_params=_cparams(("arbitrary", "arbitrary")),
        name="moe_experts",
    )(blk_expert, blk_valid, n_real, x_sorted, w_up, b_up3, w_down, b_down3, sel)


def _combine_kernel(dest_ref, dest_next_ref, rows_ref, h_ref, gate_ref, g_ref, op_ref, os_ref, buf, sem,
                    *, n_p_tiles):
    tm = h_ref.shape[0]
    i = pl.program_id(0)
    last = pl.num_programs(0) - 1
    slot = i % 2

    def issue(dest, dst_slot):
        def body(t2, carry):
            for u in range(2):
                t = t2 * 2 + u
                for j in range(TOP_K):
                    pltpu.make_async_copy(rows_ref.at[pl.ds(dest[t * TOP_K + j], 1)],
                                          buf.at[dst_slot, pl.ds(j * tm + t, 1)], sem.at[dst_slot]).start()
            return carry

        lax.fori_loop(0, tm // 2, body, 0)

    @pl.when(i == 0)
    def _():
        issue(dest_ref, 0)

    @pl.when(i < last)
    def _():
        issue(dest_next_ref, 1 - slot)

    pltpu.make_async_copy(rows_ref.at[pl.ds(0, TOP_K * tm)], buf.at[slot], sem.at[slot]).wait()
    gates = gate_ref[...]
    y = h_ref[...]
    for j in range(TOP_K):
        y = y + buf[slot, j * tm:(j + 1) * tm, :] * gates[:, j:j + 1]
    out = _rms(y, g_ref[...])
    is_prompt = pl.program_id(0) < n_p_tiles

    @pl.when(is_prompt)
    def _():
        op_ref[...] = out

    @pl.when(jnp.logical_not(is_prompt))
    def _():
        os_ref[...] = out


def _combine(dest, out_rows, h, gates, g, n_p, tm):
    t_tok = h.shape[0]
    n_p_tiles = n_p // tm
    n_steps = t_tok // tm
    return pl.pallas_call(
        functools.partial(_combine_kernel, n_p_tiles=n_p_tiles),
        grid=(n_steps,),
        in_specs=[
            pl.BlockSpec((tm * TOP_K,), lambda i: (i,), memory_space=pltpu.SMEM),
            pl.BlockSpec((tm * TOP_K,), lambda i: (jnp.minimum(i + 1, n_steps - 1),), memory_space=pltpu.SMEM),
            pl.BlockSpec(memory_space=pl.ANY),
            pl.BlockSpec((tm, D_MODEL), lambda i: (i, 0)),
            pl.BlockSpec((tm, LANES), lambda i: (i, 0)),
            pl.BlockSpec((1, D_MODEL), lambda i: (0, 0)),
        ],
        out_specs=[pl.BlockSpec((tm, D_MODEL), lambda i: (jnp.minimum(i, n_p_tiles - 1), 0)),
                   pl.BlockSpec((tm, D_MODEL), lambda i: (jnp.maximum(i - n_p_tiles, 0), 0))],
        out_shape=[jax.ShapeDtypeStruct((n_p, D_MODEL), F32),
                   jax.ShapeDtypeStruct((t_tok - n_p, D_MODEL), F32)],
        scratch_shapes=[pltpu.VMEM((2, TOP_K * tm, D_MODEL), F32), pltpu.SemaphoreType.DMA((2,))],
        compiler_params=_cparams(("arbitrary",)),
        name="moe_combine",
    )(dest, dest, out_rows, h, gates, g)


def _q_head_order():
    order = []
    for pair in range(2):
        for j in range(4):
            order += [8 * pair + j, 8 * pair + 4 + j]
    return order


def _q_col_perm():
    cols = []
    for h in _q_head_order():
        cols += list(range(h * HEAD_DIM, (h + 1) * HEAD_DIM))
    return jnp.asarray(cols, jnp.int32)


def _pad_rwkv_cols(m):
    def padw(a, w):
        return jnp.pad(a, [(0, 0)] * (a.ndim - 1) + [(0, w - a.shape[-1])])
    return jnp.concatenate([
        m[..., :XW_OFF],
        padw(m[..., XW_OFF:XW_OFF + DECAY_LORA], LANES),
        padw(m[..., XW_OFF + DECAY_LORA:XW_OFF + DECAY_LORA + ICLR_LORA], LANES),
        padw(m[..., XW_OFF + DECAY_LORA + ICLR_LORA:], XG_PAD),
    ], axis=-1)


def _unpad_rwkv_cols(m):
    return jnp.concatenate([
        m[..., :XW_OFF + DECAY_LORA],
        m[..., XA_OFF:XA_OFF + ICLR_LORA],
        m[..., XG_OFF:XG_OFF + GATE_LORA],
    ], axis=-1)


def _pad_rows(m, rows):
    return jnp.pad(m, ((0, rows - m.shape[0]), (0, 0)))


def _rope_tables(pos):
    half = HEAD_DIM // 2
    inv_freq = ROPE_THETA ** (-jnp.arange(half, dtype=F32) / half)
    ang = pos.astype(F32)[:, None] * inv_freq[None, :]
    cos = jnp.cos(ang)
    sin = jnp.sin(ang)
    cos_t = jnp.tile(jnp.concatenate([cos, cos], axis=-1), (1, LANES // HEAD_DIM))
    sin_t = jnp.tile(jnp.concatenate([-sin, sin], axis=-1), (1, LANES // HEAD_DIM))
    return cos_t, sin_t


def _route(top_idx, bm):
    n_tok = top_idx.shape[0]
    e_flat = top_idx.reshape(-1)
    onehot = (e_flat[:, None] == jnp.arange(N_EXPERTS, dtype=jnp.int32)[None, :]).astype(jnp.int32)
    csum = jnp.cumsum(onehot, axis=0)
    rank = jnp.sum(csum * onehot, axis=1) - 1
    counts = csum[-1]
    padded = ((counts + bm - 1) // bm) * bm
    p_end = jnp.cumsum(padded)
    p_start = p_end - padded
    dest = (p_start[e_flat] + rank).astype(jnp.int32)
    n_assign = n_tok * TOP_K
    n_blocks = -(-n_assign // bm) + N_EXPERTS
    n_rows = n_blocks * bm
    tok_flat = jnp.repeat(jnp.arange(n_tok, dtype=jnp.int32), TOP_K)
    row_tok = (jnp.arange(n_rows, dtype=jnp.int32) % n_tok).at[dest].set(tok_flat, unique_indices=True)
    blk_start = jnp.arange(n_blocks, dtype=jnp.int32) * bm
    blk_expert = jnp.minimum(jnp.searchsorted(p_end, blk_start, side="right"), N_EXPERTS - 1).astype(jnp.int32)
    blk_valid = jnp.clip(counts[blk_expert] - (blk_start - p_start[blk_expert]), 0, bm).astype(jnp.int32)
    sub_start = jnp.arange(n_rows // MOE_SUB, dtype=jnp.int32) * MOE_SUB
    sub_expert = blk_expert[sub_start // bm]
    sub_valid = jnp.clip(counts[sub_expert] - (sub_start - p_start[sub_expert]), 0, MOE_SUB).astype(jnp.int32)
    n_real = (p_end[-1:] // bm).astype(jnp.int32)
    return dest, row_tok, blk_expert, blk_valid, sub_valid, n_real


def _pick(n, prefs):
    for p in prefs:
        if n % p == 0:
            return p
    return n


def kernel(x_prompt, x_sample, cache_k, cache_v, state_wkv, state_shift, attn_norm_g, w_in, attn_sinks, mu_shift, decay_w0, decay_lora_up, iclr_a0, iclr_lora_up, gate_lora_up, k_k, k_a, r_k, lnx_w, lnx_b, w_out, ffn_norm_g, router_w, router_b, w_up, b_up, w_down, b_down, final_norm_g):
    depth = w_in.shape[0]
    assert depth == 1
    batch, seq, _ = x_prompt.shape
    dec_b, dec_t, _ = x_sample.shape
    n_p = batch * seq
    n_s = dec_b * dec_t
    l = 0

    qperm = _q_col_perm()
    w_in_l = w_in[l]
    w_attn = jnp.concatenate([w_in_l[:, :ATTN_WIDTH][:, qperm], w_in_l[:, ATTN_WIDTH:ATTN_WIDTH + 2 * KV_WIDTH]],
                             axis=1).astype(BF16)
    w_rwkv = _pad_rwkv_cols(w_in_l[:, ATTN_WIDTH + 2 * KV_WIDTH:]).astype(BF16)
    w_oa = w_out[l][:ATTN_WIDTH][qperm].astype(BF16)
    w_or = w_out[l][ATTN_WIDTH:].astype(BF16)
    g_attn = attn_norm_g[l][None, :]
    g_ffn = ffn_norm_g[l][None, :]
    sinks_true = attn_sinks[l].astype(F32)
    seg = (jnp.arange(GROUP)[:, None] // HEAD_DIM == jnp.arange(GROUP)[None, :] // HEAD_DIM).astype(BF16)
    prm = dict(
        mu=_pad_rwkv_cols(mu_shift[l][None, :]),
        w0=decay_w0[l][None, :], wd=_pad_rows(decay_lora_up[l], LANES).astype(BF16),
        a0=iclr_a0[l][None, :], wa=_pad_rows(iclr_lora_up[l], LANES).astype(BF16),
        wg=_pad_rows(gate_lora_up[l], XG_PAD).astype(BF16),
        k_k=k_k[l][None, :], k_a=k_a[l][None, :], r_k=r_k[l].reshape(1, RWKV_WIDTH),
        lnx_w=lnx_w[l][None, :], lnx_b=lnx_b[l][None, :], seg=seg)
    rw_pad = jnp.pad(router_w[l], ((0, 0), (0, LANES - N_EXPERTS)))
    rw_hi = rw_pad.astype(BF16)
    rw_lo = (rw_pad - rw_hi.astype(F32)).astype(BF16)
    rw_split = jnp.stack([rw_hi, rw_lo])
    rb_pad = jnp.concatenate([router_b[l], jnp.full((LANES - N_EXPERTS,), NEG_BIG, F32)])[None, :]
    b_up3 = b_up[l][:, None, :]
    b_dn = b_down[l][:, None, :]
    tf = 256
    sel = (jnp.arange(2 * tf)[:, None] == 2 * jnp.arange(tf)[None, :]).astype(BF16)

    tm_p = _pick(seq, (512, 256, 128))
    cos_p, sin_p = _rope_tables(jnp.arange(seq, dtype=jnp.int32))
    q_p, k_p, v_p = _attn_proj(x_prompt.reshape(n_p, D_MODEL), g_attn, w_attn, cos_p, sin_p, tm_p)
    tm_s = _pick(n_s, (512, 256, 128, 64, 32, 16, 8))
    pos_s = PAST_LEN + (jnp.arange(n_s, dtype=jnp.int32) % dec_t)
    cos_s, sin_s = _rope_tables(pos_s)
    q_s, k_s, v_s = _attn_proj(x_sample.reshape(n_s, D_MODEL), g_attn, w_attn, cos_s, sin_s, tm_s)
    tn = RWKV_PROJ_PAD // 4
    pr_p = _rwkv_proj(x_prompt.reshape(n_p, D_MODEL), g_attn, w_rwkv, _pick(n_p, (1024, 512, 256, 128)), tn)
    pr_s = _rwkv_proj(x_sample.reshape(n_s, D_MODEL), g_attn, w_rwkv, tm_s, tn)

    oa_p = _attn_prompt(q_p, k_p, v_p, sinks_true, batch, seq)
    wb = cache_k.shape[2]
    sink_rows = jnp.stack([
        jnp.repeat(jnp.stack([sinks_true[8 * pair + 4 * e + j] for e in range(2) for j in range(4)]), dec_t)
        for pair in range(2)])[:, :, None]
    oa_s, nk_s, nv_s = _attn_sample(
        q_s.reshape(dec_b, dec_t, ATTN_WIDTH), k_s.reshape(dec_b, dec_t, KV_WIDTH),
        v_s.reshape(dec_b, dec_t, KV_WIDTH), cache_k[l].reshape(dec_b, wb, KV_WIDTH),
        cache_v[l].reshape(dec_b, wb, KV_WIDTH), sink_rows, _pick(dec_b, (8, 4, 2, 1)))

    zero_state = jnp.zeros((batch, RWKV_WIDTH // HEAD_DIM, HEAD_DIM, HEAD_DIM), F32)
    zero_shift = jnp.zeros((batch, 1, RWKV_PROJ_PAD), F32)
    or_p, st_p, sh_p = _rwkv_mix4(pr_p, zero_state, zero_shift, prm, 1, _pick(seq, (64, 32, 16, 8)),
                                  npar=1)
    seq_per_step = _pick(dec_b, (64 // dec_t, 1)) if 64 % dec_t == 0 else 1
    or_s, st_s, sh_s = _rwkv_mix4(pr_s, state_wkv[l], _pad_rwkv_cols(state_shift[l])[:, None, :], prm,
                                  seq_per_step, dec_t)

    n_all = n_p + n_s
    tm_o = _pick(math.gcd(n_p, n_s), (512, 256, 128, 64, 32, 16, 8))
    h, hn, idx_pad, gate_pad = _out_router(
        x_prompt.reshape(n_p, D_MODEL), x_sample.reshape(n_s, D_MODEL), oa_p, oa_s.reshape(n_s, ATTN_WIDTH),
        or_p, or_s, w_oa, w_or, g_ffn, rw_split, rb_pad, tm_o)

    bm = 1024 if n_all * TOP_K >= 1024 * N_EXPERTS else 2 * MOE_SUB
    dest, row_tok, blk_expert, blk_valid, sub_valid, n_real = _route(idx_pad[:, :TOP_K], bm)
    gather_rows = GATHER_ROWS if bm % GATHER_ROWS == 0 else MOE_SUB
    step_valid = sub_valid.reshape(-1, gather_rows // MOE_SUB).sum(axis=1)
    x_sorted = _gather_rows(step_valid, row_tok, hn, gather_rows)
    out_rows = _experts(blk_expert, blk_valid, n_real, x_sorted, w_up[l], b_up3, w_down[l], b_dn, sel, bm, tf)
    y_p, y_s = _combine(dest, out_rows, h, gate_pad, final_norm_g[None, :], n_p,
                        _pick(math.gcd(n_p, n_s), (256, 128, 64, 32, 16, 8)))

    y_prompt = y_p.reshape(batch, seq, D_MODEL)
    y_sample = y_s.reshape(dec_b, dec_t, D_MODEL)
    kp = k_p.reshape(batch, seq, 4, HEAD_DIM)[:, -WINDOW:][None]
    vp = v_p.reshape(batch, seq, 4, HEAD_DIM)[:, -WINDOW:][None]
    wp = st_p[None]
    sp = _unpad_rwkv_cols(sh_p[:, 0, :])[None]
    ks = nk_s.reshape(dec_b, wb, 4, HEAD_DIM)[None]
    vs = nv_s.reshape(dec_b, wb, 4, HEAD_DIM)[None]
    ws = st_s[None]
    ss = _unpad_rwkv_cols(sh_s[:, 0, :])[None]
    return (y_prompt, y_sample, kp, vp, wp, sp, ks, vs, ws, ss)
```
